```python
import math
import jax
import jax.numpy as jnp
from jax import lax
import numpy as np

D_MODEL = 1024
BATCH = 8
SEQ = 2048
DEPTH = 4

GRID_W = 64
CTX_LEN = 256
N_BRANCH = 4
BRANCH_W = D_MODEL // 4
S5_GROUP = 16
S5_GROUPS = BRANCH_W // S5_GROUP
S5_STATE = 64
NA_HEAD_DIM = 64
NA_HEADS = BRANCH_W // NA_HEAD_DIM
NA_WIN_ROWS = 8
NA_WIN_COLS = 16
NA_SCALE = NA_HEAD_DIM ** -0.5
CONV_W = 3
CHUNK = 128
SGU_GROUPS = 4
SGU_GROUP_W = BRANCH_W // SGU_GROUPS
D_FF = 4 * D_MODEL
N_MOD = 6
EPS = 1e-6
NEG_INF = -1e30
N_SLOTS = 9
IN_W = N_SLOTS * BRANCH_W + N_BRANCH * D_MODEL
SPLIT_IDX = tuple(BRANCH_W * k for k in range(1, N_SLOTS + 1))

kernel_name = 'hybrid_diffusion_parallel_mixer'


def rms_norm(x, g):
    xf = x.astype(jnp.float32)
    y = xf * lax.rsqrt(jnp.mean(xf * xf, axis=-1, keepdims=True) + EPS)
    return (y * g.astype(jnp.float32)).astype(x.dtype)


def layer_norm(x, g, b):
    xf = x.astype(jnp.float32)
    mu = jnp.mean(xf, axis=-1, keepdims=True)
    var = jnp.mean(jnp.square(xf - mu), axis=-1, keepdims=True)
    y = (xf - mu) * lax.rsqrt(var + EPS)
    return (y * g.astype(jnp.float32) + b.astype(jnp.float32)).astype(x.dtype)


def modulate(h, shift, scale):
    return h * (1 + scale) + shift


def s5_discretize(lam_re, lam_im, log_dt, b_re, b_im):
    dt = jnp.exp(log_dt)[:, None]
    zr, zi = lam_re * dt, lam_im * dt
    mag = jnp.exp(zr)
    ar, ai = mag * jnp.cos(zi), mag * jnp.sin(zi)
    nr, ni = ar - 1.0, ai
    den = lam_re * lam_re + lam_im * lam_im
    cr = (nr * lam_re + ni * lam_im) / den
    ci = (ni * lam_re - nr * lam_im) / den
    bbr = cr[..., None] * b_re - ci[..., None] * b_im
    bbi = cr[..., None] * b_im + ci[..., None] * b_re
    return ar, ai, bbr, bbi


def _cplx_combine(e1, e2):
    a1r, a1i, b1r, b1i = e1
    a2r, a2i, b2r, b2i = e2
    return (a2r * a1r - a2i * a1i, a2r * a1i + a2i * a1r,
            a2r * b1r - a2i * b1i + b2r, a2r * b1i + a2i * b1r + b2i)


def s5_scan(u, ar, ai, bbr, bbi, s0, reverse):
    bur = jnp.einsum('blgh,gph->blgp', u, bbr)
    bui = jnp.einsum('blgh,gph->blgp', u, bbi)
    acr = jnp.broadcast_to(ar, bur.shape)
    aci = jnp.broadcast_to(ai, bur.shape)
    pr, pi_, hr, hi = lax.associative_scan(_cplx_combine, (acr, aci, bur, bui), axis=1, reverse=reverse)
    if s0 is not None:
        s0r, s0i = s0[0][:, None], s0[1][:, None]
        hr = hr + pr * s0r - pi_ * s0i
        hi = hi + pr * s0i + pi_ * s0r
    return hr, hi


def s5_readout(hr, hi, cr, ci):
    y = jnp.einsum('blgp,ghp->blgh', hr, cr) - jnp.einsum('blgp,ghp->blgh', hi, ci)
    return y.reshape(y.shape[0], y.shape[1], BRANCH_W)


def s5_glu(y, w, b):
    g = jax.nn.gelu(y)
    return g * jax.nn.sigmoid(g @ w.astype(jnp.float32) + b.astype(jnp.float32))


def s5_branch(u_lat, u_ctx, lam_re, lam_im, log_dt, b_re, b_im, c_re, c_im, d_skip, glu_w, glu_b, ctx_out):
    f32 = jnp.float32
    bn, seq_len, _ = u_lat.shape
    ctx_len = u_ctx.shape[1]
    ul = u_lat.astype(f32).reshape(bn, seq_len, S5_GROUPS, S5_GROUP)
    uc = u_ctx.astype(f32).reshape(bn, ctx_len, S5_GROUPS, S5_GROUP)
    d32 = d_skip.astype(f32)
    yl = u_lat.astype(f32) * d32
    yc = u_ctx.astype(f32) * d32 if ctx_out else None
    for direction in range(2):
        reverse = direction == 1
        ar, ai, bbr, bbi = s5_discretize(lam_re[direction].astype(f32), lam_im[direction].astype(f32),
                                         log_dt[direction].astype(f32), b_re[direction].astype(f32),
                                         b_im[direction].astype(f32))
        cr, ci = c_re[direction].astype(f32), c_im[direction].astype(f32)
        hcr, hci = s5_scan(uc, ar, ai, bbr, bbi, None, reverse)
        end = 0 if reverse else ctx_len - 1
        hlr, hli = s5_scan(ul, ar, ai, bbr, bbi, (hcr[:, end], hci[:, end]), reverse)
        yl = yl + s5_readout(hlr, hli, cr, ci)
        if ctx_out:
            yc = yc + s5_readout(hcr, hci, cr, ci)
    out_l = s5_glu(yl, glu_w, glu_b).astype(u_lat.dtype)
    out_c = s5_glu(yc, glu_w, glu_b).astype(u_ctx.dtype) if ctx_out else None
    return out_l, out_c


def neighborhood_attention(q, k, v, kc, vc, rpb):
    bn, seq_len, _ = q.shape
    rows = seq_len // GRID_W
    wr = min(NA_WIN_ROWS, rows)
    qg = q.reshape(bn, rows, GRID_W, NA_HEADS, NA_HEAD_DIM) * NA_SCALE
    kg = k.reshape(bn, rows, GRID_W, NA_HEADS, NA_HEAD_DIM)
    vg = v.reshape(bn, rows, GRID_W, NA_HEADS, NA_HEAD_DIM)
    kc = kc.reshape(bn, -1, NA_HEADS, NA_HEAD_DIM)
    vc = vc.reshape(bn, -1, NA_HEADS, NA_HEAD_DIM)
    r = jnp.arange(rows)
    row_start = jnp.clip(r - wr // 2, 0, rows - wr)
    row_idx = row_start[:, None] + jnp.arange(wr)[None, :]
    k_band = kg[:, row_idx]
    v_band = vg[:, row_idx]
    j = jnp.arange(GRID_W)
    col_start = jnp.clip(j - NA_WIN_COLS // 2, 0, GRID_W - NA_WIN_COLS)
    col_ok = (j[None, :] >= col_start[:, None]) & (j[None, :] < col_start[:, None] + NA_WIN_COLS)
    dr = row_idx - r[:, None] + NA_WIN_ROWS - 1
    dc = jnp.clip(j[None, :] - j[:, None] + NA_WIN_COLS - 1, 0, 2 * NA_WIN_COLS - 2)
    bias = rpb[:, dr[:, None, :, None], dc[None, :, None, :]]
    s_win = jnp.einsum('brqhd,brskhd->bhrqsk', qg, k_band).astype(jnp.float32) + bias.astype(jnp.float32)[None]
    s_win = jnp.where(col_ok[:, None, :], s_win, NEG_INF)
    n_win = wr * GRID_W
    s_win = s_win.reshape(bn, NA_HEADS, rows, GRID_W, n_win)
    s_ctx = jnp.einsum('brqhd,bchd->bhrqc', qg, kc).astype(jnp.float32)
    p = jax.nn.softmax(jnp.concatenate([s_win, s_ctx], axis=-1), axis=-1).astype(v.dtype)
    p_win = p[..., :n_win].reshape(bn, NA_HEADS, rows, GRID_W, wr, GRID_W)
    p_ctx = p[..., n_win:]
    o = jnp.einsum('bhrqsk,brskhd->brqhd', p_win, v_band) + jnp.einsum('bhrqc,bchd->brqhd', p_ctx, vc)
    return o.reshape(bn, seq_len, BRANCH_W)


def context_attention(qc, kc, vc):
    bn, ctx_len, _ = qc.shape
    q = qc.reshape(bn, ctx_len, NA_HEADS, NA_HEAD_DIM) * NA_SCALE
    k = kc.reshape(bn, ctx_len, NA_HEADS, NA_HEAD_DIM)
    v = vc.reshape(bn, ctx_len, NA_HEADS, NA_HEAD_DIM)
    s = jnp.einsum('bqhd,bkhd->bhqk', q, k).astype(jnp.float32)
    p = jax.nn.softmax(s, axis=-1).astype(v.dtype)
    return jnp.einsum('bhqk,bkhd->bqhd', p, v).reshape(bn, ctx_len, BRANCH_W)


def short_conv(b_gate, c_gate, xin, w):
    z = c_gate * xin
    y = lax.conv_general_dilated(z, w[:, None, :], window_strides=(1,),
                                 padding=((CONV_W // 2, CONV_W // 2),),
                                 dimension_numbers=('NWC', 'WIO', 'NWC'),
                                 feature_group_count=BRANCH_W)
    return b_gate * y


def spatial_gating(u, v, ln_g, ln_b, w_s, b_s):
    bn, seq_len, _ = v.shape
    vn = layer_norm(v, ln_g, ln_b)
    vg = vn.reshape(bn, seq_len // CHUNK, CHUNK, SGU_GROUPS, SGU_GROUP_W)
    z = jnp.einsum('gts,bnsgc->bntgc', w_s, vg) + b_s.T[None, None, :, :, None]
    return u * z.reshape(bn, seq_len, BRANCH_W)


def merge_branches(ys, gate_logits, w_branch, w_out):
    y = jnp.stack(ys, axis=-2)
    yp = jnp.einsum('blkw,kwd->blkd', y, w_branch)
    g = jax.nn.sigmoid(gate_logits.reshape(gate_logits.shape[0], gate_logits.shape[1], N_BRANCH, D_MODEL))
    return jnp.sum(g * yp, axis=-2) @ w_out


def token_mixers(al, ac, w_in, w_branch, w_out, s5_lam_re, s5_lam_im, s5_log_dt, s5_b_re, s5_b_im,
                 s5_c_re, s5_c_im, s5_d, s5_glu_w, s5_glu_b, na_rpb, conv_w, sgu_ln_g, sgu_ln_b,
                 sgu_w, sgu_b, ctx_out):
    pl = jnp.split(al @ w_in, SPLIT_IDX, axis=-1)
    if ctx_out:
        pc = jnp.split(ac @ w_in, SPLIT_IDX, axis=-1)
    else:
        pc = jnp.split(ac @ w_in[:, :4 * BRANCH_W], SPLIT_IDX[:3], axis=-1)
    ya_l, ya_c = s5_branch(pl[0], pc[0], s5_lam_re, s5_lam_im, s5_log_dt, s5_b_re, s5_b_im,
                           s5_c_re, s5_c_im, s5_d, s5_glu_w, s5_glu_b, ctx_out)
    yb_l = neighborhood_attention(pl[1], pl[2], pl[3], pc[2], pc[3], na_rpb)
    yc_l = short_conv(pl[4], pl[5], pl[6], conv_w)
    yd_l = spatial_gating(pl[7], pl[8], sgu_ln_g, sgu_ln_b, sgu_w, sgu_b)
    out_l = merge_branches((ya_l, yb_l, yc_l, yd_l), pl[9], w_branch, w_out)
    if not ctx_out:
        return out_l, None
    yb_c = context_attention(pc[1], pc[2], pc[3])
    yc_c = short_conv(pc[4], pc[5], pc[6], conv_w)
    yd_c = spatial_gating(pc[7], pc[8], sgu_ln_g, sgu_ln_b, sgu_w, sgu_b)
    out_c = merge_branches((ya_c, yb_c, yc_c, yd_c), pc[9], w_branch, w_out)
    return out_l, out_c


def sq_relu_mlp(h, w1, w2):
    return jnp.square(jax.nn.relu(h @ w1)) @ w2


def setup_inputs(seed: int = 0) -> dict:
    key = jax.random.key(seed)
    ks = jax.random.split(key, 32)
    f32 = jnp.float32

    def nrm(k, shape, scale):
        return jax.random.normal(k, shape, f32) * scale

    n_idx = jnp.arange(S5_STATE, dtype=f32)
    s5_shape = (DEPTH, 2, S5_GROUPS, S5_STATE)
    return {
        'x': nrm(ks[0], (BATCH, SEQ, D_MODEL), 1.0),
        'c': nrm(ks[1], (BATCH, D_MODEL), 1.0),
        'ctx': nrm(ks[2], (BATCH, CTX_LEN, D_MODEL), 1.0),
        'c_ctx': nrm(ks[3], (D_MODEL,), 1.0),
        'ada_w': nrm(ks[4], (DEPTH, D_MODEL, N_MOD * D_MODEL), 0.5 * D_MODEL ** -0.5),
        'ada_b': nrm(ks[5], (DEPTH, N_MOD * D_MODEL), 0.02),
        'norm_g': 1.0 + nrm(ks[6], (DEPTH, 2, D_MODEL), 0.02),
        'final_g': 1.0 + nrm(ks[7], (D_MODEL,), 0.02),
        'w_in': nrm(ks[8], (DEPTH, D_MODEL, IN_W), D_MODEL ** -0.5),
        'w_branch': nrm(ks[9], (DEPTH, N_BRANCH, BRANCH_W, D_MODEL), BRANCH_W ** -0.5),
        'w_out': nrm(ks[10], (DEPTH, D_MODEL, D_MODEL), D_MODEL ** -0.5),
        's5_lam_re': -0.5 * jnp.exp(nrm(ks[11], s5_shape, 0.05)),
        's5_lam_im': math.pi * n_idx + nrm(ks[12], s5_shape, 0.01),
        's5_log_dt': jax.random.uniform(ks[13], (DEPTH, 2, S5_GROUPS), f32, math.log(1e-3), math.log(1e-1)),
        's5_b_re': nrm(ks[14], (DEPTH, 2, S5_GROUPS, S5_STATE, S5_GROUP), (2 * S5_GROUP) ** -0.5),
        's5_b_im': nrm(ks[15], (DEPTH, 2, S5_GROUPS, S5_STATE, S5_GROUP), (2 * S5_GROUP) ** -0.5),
        's5_c_re': nrm(ks[16], (DEPTH, 2, S5_GROUPS, S5_GROUP, S5_STATE), (2 * S5_STATE) ** -0.5),
        's5_c_im': nrm(ks[17], (DEPTH, 2, S5_GROUPS, S5_GROUP, S5_STATE), (2 * S5_STATE) ** -0.5),
        's5_d': nrm(ks[18], (DEPTH, BRANCH_W), 1.0),
        's5_glu_w': nrm(ks[19], (DEPTH, BRANCH_W, BRANCH_W), BRANCH_W ** -0.5),
        's5_glu_b': nrm(ks[20], (DEPTH, BRANCH_W), 0.02),
        'na_rpb': nrm(ks[21], (DEPTH, NA_HEADS, 2 * NA_WIN_ROWS - 1, 2 * NA_WIN_COLS - 1), 0.1),
        'conv_w': nrm(ks[22], (DEPTH, CONV_W, BRANCH_W), CONV_W ** -0.5),
        'sgu_ln_g': 1.0 + nrm(ks[23], (DEPTH, BRANCH_W), 0.02),
        'sgu_ln_b': nrm(ks[24], (DEPTH, BRANCH_W), 0.02),
        'sgu_w': nrm(ks[25], (DEPTH, SGU_GROUPS, CHUNK, CHUNK), CHUNK ** -0.5),
        'sgu_b': 1.0 + nrm(ks[26], (DEPTH, SGU_GROUPS, CHUNK), 0.02),
        'w_ff1': nrm(ks[27], (DEPTH, D_MODEL, D_FF), D_MODEL ** -0.5),
        'w_ff2': nrm(ks[28], (DEPTH, D_FF, D_MODEL), D_FF ** -0.5),
    }


def reference(x, c, ctx, c_ctx, ada_w, ada_b, norm_g, final_g, w_in, w_branch, w_out,
              s5_lam_re, s5_lam_im, s5_log_dt, s5_b_re, s5_b_im, s5_c_re, s5_c_im, s5_d,
              s5_glu_w, s5_glu_b, na_rpb, conv_w, sgu_ln_g, sgu_ln_b, sgu_w, sgu_b, w_ff1, w_ff2):
    silu_c = jax.nn.silu(c)
    silu_cc = jax.nn.silu(c_ctx)
    hl, hc = x, ctx
    for i in range(DEPTH):
        ctx_out = i < DEPTH - 1
        ml = jnp.split((silu_c @ ada_w[i] + ada_b[i])[:, None, :], N_MOD, axis=-1)
        mc = jnp.split(silu_cc @ ada_w[i] + ada_b[i], N_MOD, axis=-1)
        al = modulate(rms_norm(hl, norm_g[i, 0]), ml[0], ml[1])
        ac = modulate(rms_norm(hc, norm_g[i, 0]), mc[0], mc[1])
        yl, yc = token_mixers(al, ac, w_in[i], w_branch[i], w_out[i], s5_lam_re[i], s5_lam_im[i],
                              s5_log_dt[i], s5_b_re[i], s5_b_im[i], s5_c_re[i], s5_c_im[i], s5_d[i],
                              s5_glu_w[i], s5_glu_b[i], na_rpb[i], conv_w[i], sgu_ln_g[i], sgu_ln_b[i],
                              sgu_w[i], sgu_b[i], ctx_out)
        hl = hl + ml[2] * yl
        bl = modulate(rms_norm(hl, norm_g[i, 1]), ml[3], ml[4])
        hl = hl + ml[5] * sq_relu_mlp(bl, w_ff1[i], w_ff2[i])
        if ctx_out:
            hc = hc + mc[2] * yc
            bc = modulate(rms_norm(hc, norm_g[i, 1]), mc[3], mc[4])
            hc = hc + mc[5] * sq_relu_mlp(bc, w_ff1[i], w_ff2[i])
    return rms_norm(hl, final_g)
```

```python
import functools
import math

import jax
import jax.numpy as jnp
from jax import lax
from jax.experimental import pallas as pl
from jax.experimental.pallas import tpu as pltpu

F32 = jnp.float32
BF16 = jnp.bfloat16

D_MODEL = 1024
BATCH = 8
SEQ = 2048
DEPTH = 4
GRID_W = 64
GRID_H = SEQ // GRID_W
CTX_LEN = 256
TOK = CTX_LEN + SEQ
N_BRANCH = 4
BRANCH_W = D_MODEL // 4
S5_GROUP = 16
S5_GROUPS = BRANCH_W // S5_GROUP
S5_STATE = 64
S5_LANES = S5_GROUPS * S5_STATE
NA_HEAD_DIM = 64
NA_HEADS = BRANCH_W // NA_HEAD_DIM
NA_WIN_ROWS = 8
NA_WIN_COLS = 16
NA_SCALE = NA_HEAD_DIM ** -0.5
CONV_W = 3
CHUNK = 128
SGU_GROUPS = 4
SGU_GROUP_W = BRANCH_W // SGU_GROUPS
D_FF = 4 * D_MODEL
N_MOD = 6
EPS = 1e-6
NEG_INF = -1e30
N_SLOTS = 9
SLOT_W = N_SLOTS * BRANCH_W
GATE_W = N_BRANCH * D_MODEL

MOD_ROWS = 16
MOD_CTX_ROW = BATCH
TILE = 768
N_TILES = TOK // TILE
S5_T = 128
S5_ROWS = S5_T * BATCH
S5_STEPS = TOK // S5_T
S5_CTX_STEPS = CTX_LEN // S5_T
NA_QB = 4 * GRID_W
NA_WIN_KROWS = 12
NA_WIN_KEYS = NA_WIN_KROWS * GRID_W
NA_STEPS = TOK // NA_QB
MIB = 1024 * 1024


def _params(vmem_mib, n_axes):
    return pltpu.CompilerParams(
        dimension_semantics=("arbitrary",) * n_axes,
        vmem_limit_bytes=vmem_mib * MIB,
    )


def _dot(a, b):
    return jnp.dot(a, b, preferred_element_type=F32)


def _dot_nt(a, b):
    return lax.dot_general(a, b, (((1,), (1,)), ((), ())), preferred_element_type=F32)


def _rms_modulate(h, g, shift, scale):
    y = h * lax.rsqrt(jnp.mean(h * h, axis=-1, keepdims=True) + EPS) * g
    return y * (1.0 + scale) + shift


def _row_mods(j, ml_ref, mc_ref, ks):
    rows = j * TILE + lax.broadcasted_iota(jnp.int32, (TILE, 1), 0)
    is_ctx = rows < CTX_LEN
    out = []
    for k in ks:
        sl = slice(k * D_MODEL, (k + 1) * D_MODEL)
        out.append(jnp.where(is_ctx, mc_ref[0, :, sl], ml_ref[0, :, sl]))
    return out


MOD_NT = 1536


def _mod_kernel(cc_ref, w_ref, b_ref, o_ref):
    s = jax.nn.silu(cc_ref[...])
    w = w_ref[0]
    s_hi = s.astype(BF16)
    s_lo = (s - s_hi.astype(F32)).astype(BF16)
    w_hi = w.astype(BF16)
    w_lo = (w - w_hi.astype(F32)).astype(BF16)
    o_ref[0] = _dot(s_hi, w_hi) + _dot(s_lo, w_hi) + _dot(s_hi, w_lo) + b_ref[0]


def _mod_call(cc, ada_w, ada_b):
    n = N_MOD * D_MODEL
    return pl.pallas_call(
        _mod_kernel,
        grid=(DEPTH, n // MOD_NT),
        in_specs=[
            pl.BlockSpec((MOD_ROWS, D_MODEL), lambda i, j: (0, 0)),
            pl.BlockSpec((1, D_MODEL, MOD_NT), lambda i, j: (i, 0, j)),
            pl.BlockSpec((1, 1, MOD_NT), lambda i, j: (i, 0, j)),
        ],
        out_specs=pl.BlockSpec((1, MOD_ROWS, MOD_NT), lambda i, j: (i, 0, j)),
        out_shape=jax.ShapeDtypeStruct((DEPTH, MOD_ROWS, n), F32),
        compiler_params=_params(40, 2),
        name="mod",
    )(cc, ada_w, ada_b.reshape(DEPTH, 1, n))


def _proj_kernel(h_ref, ml_ref, mc_ref, g_ref, w_ref, u_ref, qkv_ref, loc_ref):
    j = pl.program_id(1)
    shift, scale = _row_mods(j, ml_ref, mc_ref, (0, 1))
    a = _rms_modulate(h_ref[0], g_ref[...], shift, scale).astype(BF16)
    u_ref[...] = _dot(a, w_ref[:, 0:BRANCH_W]).astype(BF16)
    qkv_ref[0] = _dot(a, w_ref[:, BRANCH_W:4 * BRANCH_W]).astype(BF16)
    p = _dot(a, w_ref[:, 4 * BRANCH_W:SLOT_W])
    loc_ref[0, :, 0:BRANCH_W] = p[:, 0:BRANCH_W]
    loc_ref[0, :, BRANCH_W:2 * BRANCH_W] = p[:, BRANCH_W:2 * BRANCH_W] * p[:, 2 * BRANCH_W:3 * BRANCH_W]
    loc_ref[0, :, 2 * BRANCH_W:4 * BRANCH_W] = p[:, 3 * BRANCH_W:5 * BRANCH_W]


def _proj_call(h, mods, g, w_slots):
    return pl.pallas_call(
        _proj_kernel,
        grid=(BATCH, N_TILES),
        in_specs=[
            pl.BlockSpec((1, TILE, D_MODEL), lambda b, j: (b, j, 0)),
            pl.BlockSpec((1, 1, N_MOD * D_MODEL), lambda b, j: (b, 0, 0)),
            pl.BlockSpec((1, 1, N_MOD * D_MODEL), lambda b, j: (MOD_CTX_ROW, 0, 0)),
            pl.BlockSpec((1, D_MODEL), lambda b, j: (0, 0)),
            pl.BlockSpec((D_MODEL, SLOT_W), lambda b, j: (0, 0)),
        ],
        out_specs=[
            pl.BlockSpec((TILE, BRANCH_W), lambda b, j: (j, b)),
            pl.BlockSpec((1, TILE, 3 * BRANCH_W), lambda b, j: (b, j, 0)),
            pl.BlockSpec((1, TILE, 4 * BRANCH_W), lambda b, j: (b, j, 0)),
        ],
        out_shape=[
            jax.ShapeDtypeStruct((TOK, BATCH * BRANCH_W), BF16),
            jax.ShapeDtypeStruct((BATCH, TOK, 3 * BRANCH_W), BF16),
            jax.ShapeDtypeStruct((BATCH, TOK, 4 * BRANCH_W), F32),
        ],
        compiler_params=_params(48, 2),
        name="proj",
    )(h, mods, mods, g, w_slots)


def _s5_prep_kernel(lre_ref, lim_ref, ldt_ref, bre_ref, bim_ref, cre_ref, cim_ref, a_ref, bb_ref, cm_ref):
    lam_re = lre_ref[0, 0]
    lam_im = lim_ref[0, 0]
    dt = jnp.exp(ldt_ref[0, 0])
    zr, zi = lam_re * dt, lam_im * dt
    mag = jnp.exp(zr)
    ar, ai = mag * jnp.cos(zi), mag * jnp.sin(zi)
    nr, ni = ar - 1.0, ai
    den = lam_re * lam_re + lam_im * lam_im
    cr = (nr * lam_re + ni * lam_im) / den
    ci = (ni * lam_re - nr * lam_im) / den
    bre, bim = bre_ref[0, 0], bim_ref[0, 0]
    bb_ref[0, 0, :, 0:S5_LANES] = (cr * bre - ci * bim).astype(BF16)
    bb_ref[0, 0, :, S5_LANES:2 * S5_LANES] = (cr * bim + ci * bre).astype(BF16)
    a_ref[0, 0, :, 0:S5_LANES] = jnp.broadcast_to(ar, (BATCH, S5_LANES))
    a_ref[0, 0, :, S5_LANES:2 * S5_LANES] = jnp.broadcast_to(ai, (BATCH, S5_LANES))
    cm_ref[0, 0, 0:S5_LANES, :] = cre_ref[0, 0].astype(BF16)
    cm_ref[0, 0, S5_LANES:2 * S5_LANES, :] = (-cim_ref[0, 0]).astype(BF16)


def _s5_prep_call(lam_re, lam_im, log_dt, bre, bim, cre, cim):
    vec = pl.BlockSpec((1, 1, 1, S5_LANES), lambda i, d: (i, d, 0, 0))
    bmat = pl.BlockSpec((1, 1, BRANCH_W, S5_LANES), lambda i, d: (i, d, 0, 0))
    cmat = pl.BlockSpec((1, 1, S5_LANES, BRANCH_W), lambda i, d: (i, d, 0, 0))
    return pl.pallas_call(
        _s5_prep_kernel,
        grid=(DEPTH, 2),
        in_specs=[vec, vec, vec, bmat, bmat, cmat, cmat],
        out_specs=[
            pl.BlockSpec((1, 1, BATCH, 2 * S5_LANES), lambda i, d: (i, d, 0, 0)),
            pl.BlockSpec((1, 1, BRANCH_W, 2 * S5_LANES), lambda i, d: (i, d, 0, 0)),
            pl.BlockSpec((1, 1, 2 * S5_LANES, BRANCH_W), lambda i, d: (i, d, 0, 0)),
        ],
        out_shape=[
            jax.ShapeDtypeStruct((DEPTH, 2, BATCH, 2 * S5_LANES), F32),
            jax.ShapeDtypeStruct((DEPTH, 2, BRANCH_W, 2 * S5_LANES), BF16),
            jax.ShapeDtypeStruct((DEPTH, 2, 2 * S5_LANES, BRANCH_W), BF16),
        ],
        compiler_params=_params(32, 2),
        name="s5_prep",
    )(lam_re, lam_im, log_dt, bre, bim, cre, cim)


def _s5_rev_block(i):
    return jnp.where(i < S5_CTX_STEPS, S5_CTX_STEPS - 1 - i, S5_STEPS + S5_CTX_STEPS - 1 - i)


def _s5_scan_kernel(uf_ref, ur_ref, a_ref, bb_ref, cm_ref, yf_ref, yr_ref, x_ref, st_ref):
    i = pl.program_id(0)

    @pl.when(i == 0)
    def _():
        st_ref[...] = jnp.zeros_like(st_ref)

    mm_rows = 256
    for d, (u_ref, y_ref) in enumerate(((uf_ref, yf_ref), (ur_ref, yr_ref))):
        for c in range(S5_ROWS // mm_rows):
            rs = slice(c * mm_rows, (c + 1) * mm_rows)
            x_ref[rs, :] = _dot(u_ref[rs, :], bb_ref[0, d])
        ar = a_ref[0, d, :, 0:S5_LANES]
        ai = a_ref[0, d, :, S5_LANES:2 * S5_LANES]

        def step(t, carry, d=d, ar=ar, ai=ai):
            hr, hi = carry
            tt = t if d == 0 else S5_T - 1 - t
            r = pl.multiple_of(tt * BATCH, BATCH)
            xr = x_ref[pl.ds(r, BATCH), 0:S5_LANES]
            xi = x_ref[pl.ds(r, BATCH), S5_LANES:2 * S5_LANES]
            nr = ar * hr - ai * hi + xr
            ni = ar * hi + ai * hr + xi
            x_ref[pl.ds(r, BATCH), 0:S5_LANES] = nr
            x_ref[pl.ds(r, BATCH), S5_LANES:2 * S5_LANES] = ni
            return nr, ni

        hr, hi = lax.fori_loop(
            0, S5_T, step, (st_ref[d, :, 0:S5_LANES], st_ref[d, :, S5_LANES:2 * S5_LANES]), unroll=2)
        st_ref[d, :, 0:S5_LANES] = hr
        st_ref[d, :, S5_LANES:2 * S5_LANES] = hi
        for c in range(S5_ROWS // mm_rows):
            rs = slice(c * mm_rows, (c + 1) * mm_rows)
            y_ref[rs, :] = _dot(x_ref[rs, :].astype(BF16), cm_ref[0, d])


def _s5_scan_call(layer, u_tm, a_bar, bb, cm):
    row_blk = lambda f: pl.BlockSpec((S5_ROWS, BRANCH_W), lambda i: (f(i), 0))
    fwd, rev = (lambda i: i), _s5_rev_block
    return pl.pallas_call(
        _s5_scan_kernel,
        grid=(S5_STEPS,),
        in_specs=[
            row_blk(fwd),
            row_blk(rev),
            pl.BlockSpec((1, 2, BATCH, 2 * S5_LANES), lambda i: (layer, 0, 0, 0)),
            pl.BlockSpec((1, 2, BRANCH_W, 2 * S5_LANES), lambda i: (layer, 0, 0, 0)),
            pl.BlockSpec((1, 2, 2 * S5_LANES, BRANCH_W), lambda i: (layer, 0, 0, 0)),
        ],
        out_specs=[row_blk(fwd), row_blk(rev)],
        out_shape=[jax.ShapeDtypeStruct((TOK * BATCH, BRANCH_W), F32)] * 2,
        scratch_shapes=[
            pltpu.VMEM((S5_ROWS, 2 * S5_LANES), F32),
            pltpu.VMEM((2, BATCH, 2 * S5_LANES), F32),
        ],
        compiler_params=_params(40, 1),
        name="s5_scan",
    )(u_tm, u_tm, a_bar, bb, cm)


def _s5_glu_kernel(yf_ref, yr_ref, u_ref, d_ref, w_ref, b_ref, o_ref):
    y = yf_ref[...] + yr_ref[...] + u_ref[...].astype(F32) * d_ref[...]
    g = jax.nn.gelu(y)
    o_ref[...] = (g * jax.nn.sigmoid(_dot(g.astype(BF16), w_ref[...]) + b_ref[...])).astype(BF16)


def _s5_glu_call(yf, yr, u_tm, d_skip, w, b):
    rows = pl.BlockSpec((S5_ROWS, BRANCH_W), lambda i: (i, 0))
    vec = pl.BlockSpec((1, BRANCH_W), lambda i: (0, 0))
    return pl.pallas_call(
        _s5_glu_kernel,
        grid=(S5_STEPS,),
        in_specs=[rows, rows, rows, vec, pl.BlockSpec((BRANCH_W, BRANCH_W), lambda i: (0, 0)), vec],
        out_specs=rows,
        out_shape=jax.ShapeDtypeStruct((TOK * BATCH, BRANCH_W), BF16),
        compiler_params=_params(32, 1),
        name="s5_glu",
    )(yf, yr, u_tm, d_skip, w, b)


NA_CASE_FIRST, NA_CASE_INNER, NA_CASE_LAST, NA_CASE_CTX = 0, 1, 2, 3
N_RPB = (2 * NA_WIN_ROWS - 1) * (2 * NA_WIN_COLS - 1)


def _na_bias_kernel(rpb_ref, o_ref):
    layer, h = pl.program_id(0), pl.program_id(1)
    base = (layer * NA_HEADS + h) * N_RPB
    qq = lax.broadcasted_iota(jnp.int32, (GRID_W, 2 * GRID_W), 0)
    lane = lax.broadcasted_iota(jnp.int32, (GRID_W, 2 * GRID_W), 1)
    kk = lane % GRID_W
    second = lane >= GRID_W
    dcm = jnp.clip(kk - qq + NA_WIN_COLS - 1, 0, 2 * NA_WIN_COLS - 2)
    cs = jnp.clip(qq - NA_WIN_COLS // 2, 0, GRID_W - NA_WIN_COLS)
    col_ok = (kk >= cs) & (kk < cs + NA_WIN_COLS)
    neg = jnp.full((GRID_W, 2 * GRID_W), NEG_INF, F32)
    n_dr, n_dc = 2 * NA_WIN_ROWS - 1, 2 * NA_WIN_COLS - 1
    table = []
    for dr in range(n_dr):
        t = neg
        for dc in range(n_dc):
            t = jnp.where(dcm == dc, rpb_ref[base + dr * n_dc + dc], t)
        table.append(jnp.where(col_ok, t, NEG_INF))
    cases = {
        NA_CASE_FIRST: (lambda i: (0, NA_WIN_ROWS), NA_WIN_ROWS - 1),
        NA_CASE_INNER: (lambda i: (i, i + NA_WIN_ROWS), NA_WIN_ROWS // 2 - 1),
        NA_CASE_LAST: (lambda i: (NA_WIN_KROWS - NA_WIN_ROWS, NA_WIN_KROWS), NA_WIN_ROWS // 2 - 1 - 4),
    }
    for case, (rng, off) in cases.items():
        for i in range(4):
            lo, hi = rng(i)
            for m in range(NA_WIN_KROWS // 2):
                halves = []
                for jj in (2 * m, 2 * m + 1):
                    dr = jj - i + off
                    halves.append(table[dr] if lo <= jj < hi else neg)
                o_ref[0, case, 0, i * GRID_W:(i + 1) * GRID_W, m * 2 * GRID_W:(m + 1) * 2 * GRID_W] = (
                    jnp.where(second, halves[1], halves[0]))
    o_ref[0, NA_CASE_CTX, 0] = jnp.full((NA_QB, NA_WIN_KEYS), NEG_INF, F32)


def _na_bias_call(rpb):
    return pl.pallas_call(
        _na_bias_kernel,
        grid=(DEPTH, NA_HEADS),
        in_specs=[pl.BlockSpec(memory_space=pltpu.SMEM)],
        out_specs=pl.BlockSpec((1, 4, 1, NA_QB, NA_WIN_KEYS), lambda i, h: (i, 0, h, 0, 0)),
        out_shape=jax.ShapeDtypeStruct((DEPTH, 4, NA_HEADS, NA_QB, NA_WIN_KEYS), F32),
        compiler_params=_params(32, 2),
        name="na_bias",
    )(rpb.reshape(-1))


def _na_case(j):
    return jnp.where(j == 0, NA_CASE_CTX,
                     jnp.where(j == 1, NA_CASE_FIRST, jnp.where(j == NA_STEPS - 1, NA_CASE_LAST, NA_CASE_INNER)))


def _na_kernel(q_ref, k_ref, v_ref, bias_ref, o_ref):
    j = pl.program_id(1)
    row0 = (j - 1) * 4
    start = jnp.clip(row0 - NA_WIN_ROWS // 2, 0, GRID_H - NA_WIN_KROWS)
    k0 = pl.multiple_of(CTX_LEN + start * GRID_W, GRID_W)
    q = q_ref[0] * NA_SCALE
    kw, vw = k_ref[0, pl.ds(k0, NA_WIN_KEYS), :], v_ref[0, pl.ds(k0, NA_WIN_KEYS), :]
    kc, vc = k_ref[0, 0:CTX_LEN, :], v_ref[0, 0:CTX_LEN, :]
    head = lax.broadcasted_iota(jnp.int32, (1, BRANCH_W), 1) // NA_HEAD_DIM
    acc = jnp.zeros((NA_QB, BRANCH_W), F32)
    for h in range(NA_HEADS):
        in_head = head == h
        qh = jnp.where(in_head, q, jnp.zeros_like(q))
        sw = _dot_nt(qh, kw) + bias_ref[0, 0, h]
        sc = _dot_nt(qh, kc)
        m = jnp.maximum(jnp.max(sw, axis=-1, keepdims=True), jnp.max(sc, axis=-1, keepdims=True))
        pw, pc = jnp.exp(sw - m), jnp.exp(sc - m)
        norm = jnp.sum(pw, axis=-1, keepdims=True) + jnp.sum(pc, axis=-1, keepdims=True)
        o = _dot(pw.astype(BF16), vw) + _dot(pc.astype(BF16), vc)
        acc = jnp.where(in_head, o * (1.0 / norm), acc)
    o_ref[0] = acc.astype(BF16)


def _na_call(layer, qkv, bias):
    return pl.pallas_call(
        _na_kernel,
        grid=(BATCH, NA_STEPS),
        in_specs=[
            pl.BlockSpec((1, NA_QB, BRANCH_W), lambda b, j: (b, j, 0)),
            pl.BlockSpec((1, TOK, BRANCH_W), lambda b, j: (b, 0, 1)),
            pl.BlockSpec((1, TOK, BRANCH_W), lambda b, j: (b, 0, 2)),
            pl.BlockSpec((1, 1, NA_HEADS, NA_QB, NA_WIN_KEYS), lambda b, j: (layer, _na_case(j), 0, 0, 0)),
        ],
        out_specs=pl.BlockSpec((1, NA_QB, BRANCH_W), lambda b, j: (b, j, 0)),
        out_shape=jax.ShapeDtypeStruct((BATCH, TOK, BRANCH_W), BF16),
        compiler_params=_params(40, 2),
        name="na",
    )(qkv, qkv, qkv, bias)


HALO = 8


def _local_kernel(cb_ref, z_ref, zp_ref, zn_ref, su_ref, sv_ref, cw_ref, lg_ref, lb_ref, sw_ref, sb_ref,
                  yc_ref, yd_ref):
    j = pl.program_id(1)
    rows = j * TILE + lax.broadcasted_iota(jnp.int32, (TILE, 1), 0)
    local = lax.broadcasted_iota(jnp.int32, (TILE, 1), 0)
    z = z_ref[0]
    prev = jnp.where(local == 0, zp_ref[0, HALO - 1:HALO, :], pltpu.roll(z, 1, 0))
    nxt = jnp.where(local == TILE - 1, zn_ref[0, 0:1, :], pltpu.roll(z, TILE - 1, 0))
    prev = jnp.where((rows == 0) | (rows == CTX_LEN), 0.0, prev)
    nxt = jnp.where((rows == CTX_LEN - 1) | (rows == TOK - 1), 0.0, nxt)
    y = cw_ref[0:1, :] * prev + cw_ref[1:2, :] * z + cw_ref[2:3, :] * nxt
    yc_ref[0] = (cb_ref[0] * y).astype(BF16)

    v = sv_ref[0]
    mu = jnp.mean(v, axis=-1, keepdims=True)
    var = jnp.mean(jnp.square(v - mu), axis=-1, keepdims=True)
    vn = ((v - mu) * lax.rsqrt(var + EPS) * lg_ref[...] + lb_ref[...]).astype(BF16)
    group = lax.broadcasted_iota(jnp.int32, (1, BRANCH_W), 1) // SGU_GROUP_W
    for c in range(TILE // CHUNK):
        rs = slice(c * CHUNK, (c + 1) * CHUNK)
        vc = vn[rs, :]
        zc = sb_ref[...]
        for g in range(SGU_GROUPS):
            zc = zc + jnp.where(group == g, _dot(sw_ref[g], vc), 0.0)
        yd_ref[0, rs, :] = (su_ref[0, rs, :] * zc).astype(BF16)


def _local_call(loc, conv_w, ln_g, ln_b, sgu_w, sgu_b):
    col = lambda k: pl.BlockSpec((1, TILE, BRANCH_W), lambda b, j: (b, j, k))
    halo_per_tile = TILE // HALO
    n_halo = TOK // HALO
    vec = pl.BlockSpec((1, BRANCH_W), lambda b, j: (0, 0))
    return pl.pallas_call(
        _local_kernel,
        grid=(BATCH, N_TILES),
        in_specs=[
            col(0),
            col(1),
            pl.BlockSpec((1, HALO, BRANCH_W), lambda b, j: (b, jnp.maximum(j * halo_per_tile - 1, 0), 1)),
            pl.BlockSpec((1, HALO, BRANCH_W), lambda b, j: (b, jnp.minimum((j + 1) * halo_per_tile, n_halo - 1), 1)),
            col(2),
            col(3),
            pl.BlockSpec((CONV_W, BRANCH_W), lambda b, j: (0, 0)),
            vec,
            vec,
            pl.BlockSpec((SGU_GROUPS, CHUNK, CHUNK), lambda b, j: (0, 0, 0)),
            pl.BlockSpec((CHUNK, BRANCH_W), lambda b, j: (0, 0)),
        ],
        out_specs=[pl.BlockSpec((1, TILE, BRANCH_W), lambda b, j: (b, j, 0))] * 2,
        out_shape=[jax.ShapeDtypeStruct((BATCH, TOK, BRANCH_W), BF16)] * 2,
        compiler_params=_params(32, 2),
        name="local",
    )(loc, loc, loc, loc, loc, loc, conv_w, ln_g, ln_b, sgu_w, sgu_b)


def _merge_kernel(h_ref, ml_ref, mc_ref, g_ref, ya_ref, yb_ref, yc_ref, yd_ref, wg_ref, wb_ref, wo_ref, o_ref):
    j = pl.program_id(1)
    shift, scale, gate = _row_mods(j, ml_ref, mc_ref, (0, 1, 2))
    h = h_ref[0]
    a = _rms_modulate(h, g_ref[...], shift, scale).astype(BF16)
    ys = (ya_ref[...], yb_ref[0], yc_ref[0], yd_ref[0])
    acc = jnp.zeros((TILE, D_MODEL), F32)
    for k in range(N_BRANCH):
        logits = _dot(a, wg_ref[:, k * D_MODEL:(k + 1) * D_MODEL])
        acc = acc + jax.nn.sigmoid(logits) * _dot(ys[k], wb_ref[k])
    o_ref[0] = h + gate * _dot(acc.astype(BF16), wo_ref[...])


def _merge_call(h, mods, g, ya_tm, yb, yc, yd, w_gate, w_branch, w_out):
    tok = pl.BlockSpec((1, TILE, D_MODEL), lambda b, j: (b, j, 0))
    br = pl.BlockSpec((1, TILE, BRANCH_W), lambda b, j: (b, j, 0))
    return pl.pallas_call(
        _merge_kernel,
        grid=(BATCH, N_TILES),
        in_specs=[
            tok,
            pl.BlockSpec((1, 1, N_MOD * D_MODEL), lambda b, j: (b, 0, 0)),
            pl.BlockSpec((1, 1, N_MOD * D_MODEL), lambda b, j: (MOD_CTX_ROW, 0, 0)),
            pl.BlockSpec((1, D_MODEL), lambda b, j: (0, 0)),
            pl.BlockSpec((TILE, BRANCH_W), lambda b, j: (j, b)),
            br, br, br,
            pl.BlockSpec((D_MODEL, GATE_W), lambda b, j: (0, 0)),
            pl.BlockSpec((N_BRANCH, BRANCH_W, D_MODEL), lambda b, j: (0, 0, 0)),
            pl.BlockSpec((D_MODEL, D_MODEL), lambda b, j: (0, 0)),
        ],
        out_specs=tok,
        out_shape=jax.ShapeDtypeStruct((BATCH, TOK, D_MODEL), F32),
        compiler_params=_params(56, 2),
        name="merge",
    )(h, mods, mods, g, ya_tm, yb, yc, yd, w_gate, w_branch, w_out)


FF_CHUNK = 1024


def _ffn_kernel(h_ref, ml_ref, mc_ref, g_ref, w1_ref, w2_ref, o_ref):
    j = pl.program_id(1)
    shift, scale, gate = _row_mods(j, ml_ref, mc_ref, (3, 4, 5))
    h = h_ref[0]
    a = _rms_modulate(h, g_ref[...], shift, scale).astype(BF16)
    acc = jnp.zeros((TILE, D_MODEL), F32)
    for c in range(D_FF // FF_CHUNK):
        cs = slice(c * FF_CHUNK, (c + 1) * FF_CHUNK)
        t = jnp.maximum(_dot(a, w1_ref[:, cs]), 0.0)
        acc = acc + _dot((t * t).astype(BF16), w2_ref[cs, :])
    o_ref[0] = h + gate * acc


def _ffn_call(h, mods, g, w1, w2):
    tok = pl.BlockSpec((1, TILE, D_MODEL), lambda b, j: (b, j, 0))
    return pl.pallas_call(
        _ffn_kernel,
        grid=(BATCH, N_TILES),
        in_specs=[
            tok,
            pl.BlockSpec((1, 1, N_MOD * D_MODEL), lambda b, j: (b, 0, 0)),
            pl.BlockSpec((1, 1, N_MOD * D_MODEL), lambda b, j: (MOD_CTX_ROW, 0, 0)),
            pl.BlockSpec((1, D_MODEL), lambda b, j: (0, 0)),
            pl.BlockSpec((D_MODEL, D_FF), lambda b, j: (0, 0), pipeline_mode=pl.Buffered(1)),
            pl.BlockSpec((D_FF, D_MODEL), lambda b, j: (0, 0), pipeline_mode=pl.Buffered(1)),
        ],
        out_specs=tok,
        out_shape=jax.ShapeDtypeStruct((BATCH, TOK, D_MODEL), F32),
        compiler_params=_params(56, 2),
        name="ffn",
    )(h, mods, mods, g, w1, w2)


FINAL_ROWS = 256


def _final_kernel(h_ref, g_ref, o_ref):
    h = h_ref[0]
    o_ref[0] = h * lax.rsqrt(jnp.mean(h * h, axis=-1, keepdims=True) + EPS) * g_ref[...]


def _final_call(h, g):
    skip = CTX_LEN // FINAL_ROWS
    return pl.pallas_call(
        _final_kernel,
        grid=(BATCH, SEQ // FINAL_ROWS),
        in_specs=[
            pl.BlockSpec((1, FINAL_ROWS, D_MODEL), lambda b, j: (b, j + skip, 0)),
            pl.BlockSpec((1, D_MODEL), lambda b, j: (0, 0)),
        ],
        out_specs=pl.BlockSpec((1, FINAL_ROWS, D_MODEL), lambda b, j: (b, j, 0)),
        out_shape=jax.ShapeDtypeStruct((BATCH, SEQ, D_MODEL), F32),
        compiler_params=_params(32, 2),
        name="final_norm",
    )(h, g)


def _block_diag(x, eye, spec):
    return jnp.einsum(spec, x, eye)


def kernel(x, c, ctx, c_ctx, ada_w, ada_b, norm_g, final_g, w_in, w_branch, w_out, s5_lam_re, s5_lam_im, s5_log_dt, s5_b_re, s5_b_im, s5_c_re, s5_c_im, s5_d, s5_glu_w, s5_glu_b, na_rpb, conv_w, sgu_ln_g, sgu_ln_b, sgu_w, sgu_b, w_ff1, w_ff2):
    h = jnp.concatenate([ctx, x], axis=1)
    cc = jnp.concatenate([c, c_ctx[None], jnp.zeros((MOD_ROWS - BATCH - 1, D_MODEL), F32)], axis=0)
    eye = jnp.eye(S5_GROUPS, dtype=F32)
    bre = _block_diag(s5_b_re, eye, 'ldgph,gk->ldghkp').reshape(DEPTH, 2, BRANCH_W, S5_LANES)
    bim = _block_diag(s5_b_im, eye, 'ldgph,gk->ldghkp').reshape(DEPTH, 2, BRANCH_W, S5_LANES)
    cre = _block_diag(s5_c_re, eye, 'ldghp,gk->ldkpgh').reshape(DEPTH, 2, S5_LANES, BRANCH_W)
    cim = _block_diag(s5_c_im, eye, 'ldghp,gk->ldkpgh').reshape(DEPTH, 2, S5_LANES, BRANCH_W)
    lam_re = s5_lam_re.reshape(DEPTH, 2, 1, S5_LANES)
    lam_im = s5_lam_im.reshape(DEPTH, 2, 1, S5_LANES)
    log_dt = jnp.repeat(s5_log_dt, S5_STATE, axis=-1).reshape(DEPTH, 2, 1, S5_LANES)
    sgu_bias = jnp.repeat(jnp.swapaxes(sgu_b, 1, 2), SGU_GROUP_W, axis=2)
    w_in_b, w_branch_b, w_out_b = w_in.astype(BF16), w_branch.astype(BF16), w_out.astype(BF16)
    w_ff1_b, w_ff2_b = w_ff1.astype(BF16), w_ff2.astype(BF16)
    glu_w_b, sgu_w_b = s5_glu_w.astype(BF16), sgu_w.astype(BF16)

    mods = _mod_call(cc, ada_w, ada_b)
    a_bar, bb, cm = _s5_prep_call(lam_re, lam_im, log_dt, bre, bim, cre, cim)
    na_bias = _na_bias_call(na_rpb)

    for i in range(DEPTH):
        m = mods[i].reshape(MOD_ROWS, 1, N_MOD * D_MODEL)
        u_tm, qkv, loc = _proj_call(h, m, norm_g[i, 0:1], w_in_b[i, :, :SLOT_W])
        u_rows = u_tm.reshape(TOK * BATCH, BRANCH_W)
        yf, yr = _s5_scan_call(i, u_rows, a_bar, bb, cm)
        ya = _s5_glu_call(yf, yr, u_rows, s5_d[i:i + 1], glu_w_b[i], s5_glu_b[i:i + 1])
        yb = _na_call(i, qkv, na_bias)
        yc, yd = _local_call(loc, conv_w[i], sgu_ln_g[i:i + 1], sgu_ln_b[i:i + 1], sgu_w_b[i], sgu_bias[i])
        h = _merge_call(h, m, norm_g[i, 0:1], ya.reshape(TOK, BATCH * BRANCH_W), yb, yc, yd,
                        w_in_b[i, :, SLOT_W:], w_branch_b[i], w_out_b[i])
        h = _ffn_call(h, m, norm_g[i, 1:2], w_ff1_b[i], w_ff2_b[i])
    return _final_call(h, final_g[None])
```

```python
import functools
import math

import jax
import jax.numpy as jnp
from jax import lax
from jax.experimental import pallas as pl
from jax.experimental.pallas import tpu as pltpu

F32 = jnp.float32
BF16 = jnp.bfloat16

D_MODEL = 1024
BATCH = 8
SEQ = 2048
DEPTH = 4
GRID_W = 64
GRID_H = SEQ // GRID_W
CTX_LEN = 256
TOK = SEQ + CTX_LEN
N_BRANCH = 4
BRANCH_W = D_MODEL // 4
S5_GROUP = 16
S5_GROUPS = BRANCH_W // S5_GROUP
S5_STATE = 64
S5_LANES = S5_GROUPS * S5_STATE
NA_HEAD_DIM = 64
NA_HEADS = BRANCH_W // NA_HEAD_DIM
NA_WIN_ROWS = 8
NA_WIN_COLS = 16
NA_SCALE = NA_HEAD_DIM ** -0.5
CONV_W = 3
CHUNK = 128
SGU_GROUPS = 4
SGU_GROUP_W = BRANCH_W // SGU_GROUPS
D_FF = 4 * D_MODEL
N_MOD = 6
MOD_W = N_MOD * D_MODEL
EPS = 1e-6
NEG_INF = -1e30
N_SLOTS = 9
SLOT_W = N_SLOTS * BRANCH_W
ATT_W = 4 * BRANCH_W
LOC_W = 5 * BRANCH_W
GATE_W = N_BRANCH * D_MODEL

LANES = 128
MOD_ROWS = 16
MOD_CTX_ROW = BATCH
TILE = 768
N_TILES = TOK // TILE
TILE_LATENT = 1024
S5_T = 128
S5_ROWS = S5_T * BATCH
S5_STEPS = TOK // S5_T
S5_LAT_STEPS = SEQ // S5_T
S5_SLABS = BRANCH_W // LANES
NA_QB = 4 * GRID_W
NA_WIN_KROWS = 12
NA_WIN_KEYS = NA_WIN_KROWS * GRID_W
NA_LAT_STEPS = SEQ // NA_QB
NA_STEPS = TOK // NA_QB
MIB = 1024 * 1024


def _params(vmem_mib, n_axes):
    return pltpu.CompilerParams(
        dimension_semantics=("arbitrary",) * n_axes,
        vmem_limit_bytes=vmem_mib * MIB,
    )


def _resident(shape, index_map):
    return pl.BlockSpec(shape, index_map, pipeline_mode=pl.Buffered(1))


def _dot(a, b):
    return jnp.dot(a, b, preferred_element_type=F32)


def _dot_nt(a, b):
    return lax.dot_general(a, b, (((1,), (1,)), ((), ())), preferred_element_type=F32)


def _rms_modulate(h, g, shift, scale):
    y = h * lax.rsqrt(jnp.mean(h * h, axis=-1, keepdims=True) + EPS) * g
    return y * (1.0 + scale) + shift


def _row_mods(j, tile, ml_ref, mc_ref, ks):
    if mc_ref is None:
        return [ml_ref[0, :, k * D_MODEL:(k + 1) * D_MODEL] for k in ks]
    rows = j * tile + lax.broadcasted_iota(jnp.int32, (tile, 1), 0)
    is_ctx = rows >= SEQ
    return [jnp.where(is_ctx, mc_ref[0, :, k * D_MODEL:(k + 1) * D_MODEL], ml_ref[0, :, k * D_MODEL:(k + 1) * D_MODEL])
            for k in ks]


def _mod_specs(layer, with_ctx=True):
    specs = [pl.BlockSpec((1, 1, MOD_W), lambda b, j: (layer * MOD_ROWS + b, 0, 0))]
    if with_ctx:
        specs.append(pl.BlockSpec((1, 1, MOD_W), lambda b, j: (layer * MOD_ROWS + MOD_CTX_ROW, 0, 0)))
    return specs


def _gain_spec(layer, which):
    return pl.BlockSpec((1, 1, D_MODEL), lambda b, j: (2 * layer + which, 0, 0))


MOD_NT = 1536


def _mod_kernel(cc_ref, w_ref, b_ref, o_ref):
    s = jax.nn.silu(cc_ref[...])
    w = w_ref[0]
    s_hi = s.astype(BF16)
    s_lo = (s - s_hi.astype(F32)).astype(BF16)
    w_hi = w.astype(BF16)
    w_lo = (w - w_hi.astype(F32)).astype(BF16)
    o_ref[0] = _dot(s_hi, w_hi) + _dot(s_lo, w_hi) + _dot(s_hi, w_lo) + b_ref[0]


def _mod_call(cc, ada_w, ada_b):
    return pl.pallas_call(
        _mod_kernel,
        grid=(DEPTH, MOD_W // MOD_NT),
        in_specs=[
            pl.BlockSpec((MOD_ROWS, D_MODEL), lambda i, j: (0, 0)),
            pl.BlockSpec((1, D_MODEL, MOD_NT), lambda i, j: (i, 0, j)),
            pl.BlockSpec((1, 1, MOD_NT), lambda i, j: (i, 0, j)),
        ],
        out_specs=pl.BlockSpec((1, MOD_ROWS, MOD_NT), lambda i, j: (i, 0, j)),
        out_shape=jax.ShapeDtypeStruct((DEPTH, MOD_ROWS, MOD_W), F32),
        compiler_params=_params(40, 2),
        name="mod",
    )(cc, ada_w, ada_b.reshape(DEPTH, 1, MOD_W))


def _proj_kernel(h_ref, ml_ref, mc_ref, g_ref, w_ref, att_ref, loc_ref):
    j = pl.program_id(1)
    shift, scale = _row_mods(j, TILE, ml_ref, mc_ref, (0, 1))
    a = _rms_modulate(h_ref[0], g_ref[0], shift, scale).astype(BF16)
    att_ref[0] = _dot(a, w_ref[0, :, 0:ATT_W]).astype(BF16)
    p = _dot(a, w_ref[0, :, ATT_W:SLOT_W])
    loc_ref[0, :, 0:BRANCH_W] = p[:, 0:BRANCH_W]
    loc_ref[0, :, BRANCH_W:2 * BRANCH_W] = p[:, BRANCH_W:2 * BRANCH_W] * p[:, 2 * BRANCH_W:3 * BRANCH_W]
    loc_ref[0, :, 2 * BRANCH_W:4 * BRANCH_W] = p[:, 3 * BRANCH_W:5 * BRANCH_W]


def _proj_call(layer, h, mods, gains, w_slots):
    return pl.pallas_call(
        _proj_kernel,
        grid=(BATCH, N_TILES),
        in_specs=[
            pl.BlockSpec((1, TILE, D_MODEL), lambda b, j: (b, j, 0)),
            *_mod_specs(layer),
            _gain_spec(layer, 0),
            _resident((1, D_MODEL, SLOT_W), lambda b, j: (layer, 0, 0)),
        ],
        out_specs=[
            pl.BlockSpec((1, TILE, ATT_W), lambda b, j: (b, j, 0)),
            pl.BlockSpec((1, TILE, 4 * BRANCH_W), lambda b, j: (b, j, 0)),
        ],
        out_shape=[
            jax.ShapeDtypeStruct((BATCH, TOK, ATT_W), BF16),
            jax.ShapeDtypeStruct((BATCH, TOK, 4 * BRANCH_W), F32),
        ],
        compiler_params=_params(48, 2),
        name="proj",
    )(h, mods, mods, gains, w_slots)


def _s5_prep_kernel(lre_ref, lim_ref, ldt_ref, bre_ref, bim_ref, cre_ref, cim_ref, a_ref, bb_ref, cm_ref):
    lam_re = lre_ref[0, 0]
    lam_im = lim_ref[0, 0]
    dt = jnp.exp(ldt_ref[0, 0])
    zr, zi = lam_re * dt, lam_im * dt
    mag = jnp.exp(zr)
    ar, ai = mag * jnp.cos(zi), mag * jnp.sin(zi)
    nr, ni = ar - 1.0, ai
    den = lam_re * lam_re + lam_im * lam_im
    cr = (nr * lam_re + ni * lam_im) / den
    ci = (ni * lam_re - nr * lam_im) / den
    bre, bim = bre_ref[0, 0], bim_ref[0, 0]
    bb_ref[0, 0, :, 0:S5_LANES] = (cr * bre - ci * bim).astype(BF16)
    bb_ref[0, 0, :, S5_LANES:2 * S5_LANES] = (cr * bim + ci * bre).astype(BF16)
    a_ref[0, 0, :, 0:S5_LANES] = jnp.broadcast_to(ar, (BATCH, S5_LANES))
    a_ref[0, 0, :, S5_LANES:2 * S5_LANES] = jnp.broadcast_to(ai, (BATCH, S5_LANES))
    cm_ref[0, 0, 0:S5_LANES, :] = cre_ref[0, 0].astype(BF16)
    cm_ref[0, 0, S5_LANES:2 * S5_LANES, :] = (-cim_ref[0, 0]).astype(BF16)


def _s5_prep_call(lam_re, lam_im, log_dt, bre, bim, cre, cim):
    vec = pl.BlockSpec((1, 1, 1, S5_LANES), lambda i, d: (i, d, 0, 0))
    bmat = pl.BlockSpec((1, 1, BRANCH_W, S5_LANES), lambda i, d: (i, d, 0, 0))
    cmat = pl.BlockSpec((1, 1, S5_LANES, BRANCH_W), lambda i, d: (i, d, 0, 0))
    return pl.pallas_call(
        _s5_prep_kernel,
        grid=(DEPTH, 2),
        in_specs=[vec, vec, vec, bmat, bmat, cmat, cmat],
        out_specs=[
            pl.BlockSpec((1, 1, BATCH, 2 * S5_LANES), lambda i, d: (i, d, 0, 0)),
            pl.BlockSpec((1, 1, BRANCH_W, 2 * S5_LANES), lambda i, d: (i, d, 0, 0)),
            pl.BlockSpec((1, 1, 2 * S5_LANES, BRANCH_W), lambda i, d: (i, d, 0, 0)),
        ],
        out_shape=[
            jax.ShapeDtypeStruct((DEPTH, 2, BATCH, 2 * S5_LANES), F32),
            jax.ShapeDtypeStruct((DEPTH, 2, BRANCH_W, 2 * S5_LANES), BF16),
            jax.ShapeDtypeStruct((DEPTH, 2, 2 * S5_LANES, BRANCH_W), BF16),
        ],
        compiler_params=_params(32, 2),
        name="s5_prep",
    )(lam_re, lam_im, log_dt, bre, bim, cre, cim)


def _s5_fwd_block(i):
    return jnp.where(i < S5_STEPS - S5_LAT_STEPS, S5_LAT_STEPS + i, i - (S5_STEPS - S5_LAT_STEPS))


def _s5_rev_block(i):
    return S5_STEPS - 1 - i


def _to_time_major(u_ref, slab_ref):
    for b in range(BATCH):
        ub = u_ref[b].astype(F32)
        for s in range(S5_SLABS):
            slab_ref[s, pl.ds(b, S5_T, stride=BATCH), :] = ub[:, s * LANES:(s + 1) * LANES]


def _s5_scan_kernel(uf_ref, ur_ref, a_ref, bb_ref, cm_ref, yf_ref, yr_ref, x_ref, st_ref, us_ref):
    i = pl.program_id(0)

    @pl.when(i == 0)
    def _():
        st_ref[...] = jnp.zeros_like(st_ref)

    mm_rows = 256
    for d, (u_ref, y_ref) in enumerate(((uf_ref, yf_ref), (ur_ref, yr_ref))):
        _to_time_major(u_ref, us_ref)
        for c in range(S5_ROWS // mm_rows):
            rs = slice(c * mm_rows, (c + 1) * mm_rows)
            u = jnp.concatenate([us_ref[s, rs, :] for s in range(S5_SLABS)], axis=1).astype(BF16)
            x_ref[rs, :] = _dot(u, bb_ref[0, d])
        ar = a_ref[0, d, :, 0:S5_LANES]
        ai = a_ref[0, d, :, S5_LANES:2 * S5_LANES]

        def step(t, carry, d=d, ar=ar, ai=ai):
            hr, hi = carry
            tt = t if d == 0 else S5_T - 1 - t
            r = pl.multiple_of(tt * BATCH, BATCH)
            xr = x_ref[pl.ds(r, BATCH), 0:S5_LANES]
            xi = x_ref[pl.ds(r, BATCH), S5_LANES:2 * S5_LANES]
            nr = ar * hr - ai * hi + xr
            ni = ar * hi + ai * hr + xi
            x_ref[pl.ds(r, BATCH), 0:S5_LANES] = nr
            x_ref[pl.ds(r, BATCH), S5_LANES:2 * S5_LANES] = ni
            return nr, ni

        hr, hi = lax.fori_loop(
            0, S5_T, step, (st_ref[d, :, 0:S5_LANES], st_ref[d, :, S5_LANES:2 * S5_LANES]), unroll=2)
        st_ref[d, :, 0:S5_LANES] = hr
        st_ref[d, :, S5_LANES:2 * S5_LANES] = hi
        for c in range(S5_ROWS // mm_rows):
            rs = slice(c * mm_rows, (c + 1) * mm_rows)
            y_ref[rs, :] = _dot(x_ref[rs, :].astype(BF16), cm_ref[0, d])


def _s5_scan_call(layer, att, a_bar, bb, cm):
    u_blk = lambda f: pl.BlockSpec((BATCH, S5_T, BRANCH_W), lambda i: (0, f(i), 0))
    y_blk = lambda f: pl.BlockSpec((S5_ROWS, BRANCH_W), lambda i: (f(i), 0))
    return pl.pallas_call(
        _s5_scan_kernel,
        grid=(S5_STEPS,),
        in_specs=[
            u_blk(_s5_fwd_block),
            u_blk(_s5_rev_block),
            pl.BlockSpec((1, 2, BATCH, 2 * S5_LANES), lambda i: (layer, 0, 0, 0)),
            pl.BlockSpec((1, 2, BRANCH_W, 2 * S5_LANES), lambda i: (layer, 0, 0, 0)),
            pl.BlockSpec((1, 2, 2 * S5_LANES, BRANCH_W), lambda i: (layer, 0, 0, 0)),
        ],
        out_specs=[y_blk(_s5_fwd_block), y_blk(_s5_rev_block)],
        out_shape=[jax.ShapeDtypeStruct((TOK * BATCH, BRANCH_W), F32)] * 2,
        scratch_shapes=[
            pltpu.VMEM((S5_ROWS, 2 * S5_LANES), F32),
            pltpu.VMEM((2, BATCH, 2 * S5_LANES), F32),
            pltpu.VMEM((S5_SLABS, S5_ROWS, LANES), F32),
        ],
        compiler_params=_params(40, 1),
        name="s5_scan",
    )(att, att, a_bar, bb, cm)


def _s5_glu_kernel(yf_ref, yr_ref, u_ref, d_ref, w_ref, b_ref, o_ref, ys_ref):
    y = yf_ref[...] + yr_ref[...]
    for s in range(S5_SLABS):
        ys_ref[s] = y[:, s * LANES:(s + 1) * LANES]
    for b in range(BATCH):
        yb = jnp.concatenate([ys_ref[s, pl.ds(b, S5_T, stride=BATCH), :] for s in range(S5_SLABS)], axis=1)
        g = jax.nn.gelu(yb + u_ref[b].astype(F32) * d_ref[0])
        o_ref[b] = (g * jax.nn.sigmoid(_dot(g.astype(BF16), w_ref[0]) + b_ref[0])).astype(BF16)


def _s5_glu_call(layer, yf, yr, att, d_skip, w, b):
    rows = pl.BlockSpec((S5_ROWS, BRANCH_W), lambda i: (i, 0))
    vec = pl.BlockSpec((1, 1, BRANCH_W), lambda i: (layer, 0, 0))
    tok = pl.BlockSpec((BATCH, S5_T, BRANCH_W), lambda i: (0, i, 0))
    return pl.pallas_call(
        _s5_glu_kernel,
        grid=(S5_STEPS,),
        in_specs=[rows, rows, tok, vec, pl.BlockSpec((1, BRANCH_W, BRANCH_W), lambda i: (layer, 0, 0)), vec],
        out_specs=tok,
        out_shape=jax.ShapeDtypeStruct((BATCH, TOK, BRANCH_W), BF16),
        scratch_shapes=[pltpu.VMEM((S5_SLABS, S5_ROWS, LANES), F32)],
        compiler_params=_params(32, 1),
        name="s5_glu",
    )(yf, yr, att, d_skip, w, b)


NA_CASE_FIRST, NA_CASE_INNER, NA_CASE_LAST, NA_CASE_CTX = 0, 1, 2, 3
N_RPB = (2 * NA_WIN_ROWS - 1) * (2 * NA_WIN_COLS - 1)


def _na_window_start(row0):
    return min(max(row0 - NA_WIN_ROWS // 2, 0), GRID_H - NA_WIN_KROWS)


def _na_bias_kernel(rpb_ref, o_ref):
    layer, h = pl.program_id(0), pl.program_id(1)
    base = (layer * NA_HEADS + h) * N_RPB
    qq = lax.broadcasted_iota(jnp.int32, (GRID_W, 2 * GRID_W), 0)
    lane = lax.broadcasted_iota(jnp.int32, (GRID_W, 2 * GRID_W), 1)
    kk = jnp.bitwise_and(lane, GRID_W - 1)
    second = lane >= GRID_W
    dcm = jnp.clip(kk - qq + NA_WIN_COLS - 1, 0, 2 * NA_WIN_COLS - 2)
    cs = jnp.clip(qq - NA_WIN_COLS // 2, 0, GRID_W - NA_WIN_COLS)
    col_ok = (kk >= cs) & (kk < cs + NA_WIN_COLS)
    neg = jnp.full((GRID_W, 2 * GRID_W), NEG_INF, F32)
    n_dr, n_dc = 2 * NA_WIN_ROWS - 1, 2 * NA_WIN_COLS - 1
    table = []
    for dr in range(n_dr):
        t = neg
        for dc in range(n_dc):
            t = jnp.where(dcm == dc, rpb_ref[base + dr * n_dc + dc], t)
        table.append(jnp.where(col_ok, t, NEG_INF))
    for case, row0 in ((NA_CASE_FIRST, 0), (NA_CASE_INNER, 4), (NA_CASE_LAST, GRID_H - 4)):
        start = _na_window_start(row0)
        for i in range(4):
            r = row0 + i
            row_start = min(max(r - NA_WIN_ROWS // 2, 0), GRID_H - NA_WIN_ROWS)
            for m in range(NA_WIN_KROWS // 2):
                halves = []
                for jj in (2 * m, 2 * m + 1):
                    kr = start + jj
                    valid = row_start <= kr < row_start + NA_WIN_ROWS
                    halves.append(table[kr - r + NA_WIN_ROWS - 1] if valid else neg)
                o_ref[0, case, 0, i * GRID_W:(i + 1) * GRID_W, m * 2 * GRID_W:(m + 1) * 2 * GRID_W] = (
                    jnp.where(second, halves[1], halves[0]))
    o_ref[0, NA_CASE_CTX, 0] = jnp.full((NA_QB, NA_WIN_KEYS), NEG_INF, F32)


def _na_bias_call(rpb):
    return pl.pallas_call(
        _na_bias_kernel,
        grid=(DEPTH, NA_HEADS),
        in_specs=[pl.BlockSpec(memory_space=pltpu.SMEM)],
        out_specs=pl.BlockSpec((1, 4, 1, NA_QB, NA_WIN_KEYS), lambda i, h: (i, 0, h, 0, 0)),
        out_shape=jax.ShapeDtypeStruct((DEPTH, 4, NA_HEADS, NA_QB, NA_WIN_KEYS), F32),
        compiler_params=_params(32, 2),
        name="na_bias",
    )(rpb.reshape(-1))


def _na_case(j):
    return jnp.where(j == NA_LAT_STEPS, NA_CASE_CTX,
                     jnp.where(j == 0, NA_CASE_FIRST, jnp.where(j == NA_LAT_STEPS - 1, NA_CASE_LAST, NA_CASE_INNER)))


def _na_kernel(q_ref, k_ref, v_ref, bias_ref, o_ref):
    j = pl.program_id(1)
    row0 = jnp.minimum(j, NA_LAT_STEPS - 1) * 4
    start = jnp.clip(row0 - NA_WIN_ROWS // 2, 0, GRID_H - NA_WIN_KROWS)
    k0 = pl.multiple_of(start * GRID_W, GRID_W)
    q = q_ref[0] * NA_SCALE
    kw, vw = k_ref[0, pl.ds(k0, NA_WIN_KEYS), :], v_ref[0, pl.ds(k0, NA_WIN_KEYS), :]
    kc, vc = k_ref[0, SEQ:TOK, :], v_ref[0, SEQ:TOK, :]
    head = lax.shift_right_logical(lax.broadcasted_iota(jnp.int32, (1, BRANCH_W), 1), int(math.log2(NA_HEAD_DIM)))
    acc = jnp.zeros((NA_QB, BRANCH_W), F32)
    for h in range(NA_HEADS):
        in_head = head == h
        qh = jnp.where(in_head, q, jnp.zeros_like(q))
        sw = _dot_nt(qh, kw) + bias_ref[0, 0, h]
        sc = _dot_nt(qh, kc)
        m = jnp.maximum(jnp.max(sw, axis=-1, keepdims=True), jnp.max(sc, axis=-1, keepdims=True))
        pw, pc = jnp.exp(sw - m), jnp.exp(sc - m)
        norm = jnp.sum(pw, axis=-1, keepdims=True) + jnp.sum(pc, axis=-1, keepdims=True)
        o = _dot(pw.astype(BF16), vw) + _dot(pc.astype(BF16), vc)
        acc = jnp.where(in_head, o * (1.0 / norm), acc)
    o_ref[0] = acc.astype(BF16)


def _na_call(layer, att, bias):
    return pl.pallas_call(
        _na_kernel,
        grid=(BATCH, NA_STEPS),
        in_specs=[
            pl.BlockSpec((1, NA_QB, BRANCH_W), lambda b, j: (b, j, 1)),
            pl.BlockSpec((1, TOK, BRANCH_W), lambda b, j: (b, 0, 2)),
            pl.BlockSpec((1, TOK, BRANCH_W), lambda b, j: (b, 0, 3)),
            pl.BlockSpec((1, 1, NA_HEADS, NA_QB, NA_WIN_KEYS), lambda b, j: (layer, _na_case(j), 0, 0, 0)),
        ],
        out_specs=pl.BlockSpec((1, NA_QB, BRANCH_W), lambda b, j: (b, j, 0)),
        out_shape=jax.ShapeDtypeStruct((BATCH, TOK, BRANCH_W), BF16),
        compiler_params=_params(40, 2),
        name="na",
    )(att, att, att, bias)


HALO = 8


def _local_kernel(cb_ref, z_ref, zp_ref, zn_ref, su_ref, sv_ref, cw_ref, lg_ref, lb_ref, sw_ref, sb_ref,
                  yc_ref, yd_ref):
    j = pl.program_id(1)
    local = lax.broadcasted_iota(jnp.int32, (TILE, 1), 0)
    rows = j * TILE + local
    z = z_ref[0]
    prev = jnp.where(local == 0, zp_ref[0, HALO - 1:HALO, :], pltpu.roll(z, 1, 0))
    nxt = jnp.where(local == TILE - 1, zn_ref[0, 0:1, :], pltpu.roll(z, TILE - 1, 0))
    prev = jnp.where((rows == 0) | (rows == SEQ), 0.0, prev)
    nxt = jnp.where((rows == SEQ - 1) | (rows == TOK - 1), 0.0, nxt)
    y = cw_ref[0, 0:1, :] * prev + cw_ref[0, 1:2, :] * z + cw_ref[0, 2:3, :] * nxt
    yc_ref[0] = (cb_ref[0] * y).astype(BF16)

    v = sv_ref[0]
    mu = jnp.mean(v, axis=-1, keepdims=True)
    var = jnp.mean(jnp.square(v - mu), axis=-1, keepdims=True)
    vn = ((v - mu) * lax.rsqrt(var + EPS) * lg_ref[0] + lb_ref[0]).astype(BF16)
    group = lax.shift_right_logical(lax.broadcasted_iota(jnp.int32, (1, BRANCH_W), 1), int(math.log2(SGU_GROUP_W)))
    for c in range(TILE // CHUNK):
        rs = slice(c * CHUNK, (c + 1) * CHUNK)
        vc = vn[rs, :]
        zc = sb_ref[0]
        for g in range(SGU_GROUPS):
            zc = zc + jnp.where(group == g, _dot(sw_ref[0, g], vc), 0.0)
        yd_ref[0, rs, :] = (su_ref[0, rs, :] * zc).astype(BF16)


def _local_call(layer, loc, conv_w, ln_g, ln_b, sgu_w, sgu_b):
    col = lambda k: pl.BlockSpec((1, TILE, BRANCH_W), lambda b, j: (b, j, k))
    halo_per_tile = TILE // HALO
    n_halo = TOK // HALO
    vec = pl.BlockSpec((1, 1, BRANCH_W), lambda b, j: (layer, 0, 0))
    return pl.pallas_call(
        _local_kernel,
        grid=(BATCH, N_TILES),
        in_specs=[
            col(0),
            col(1),
            pl.BlockSpec((1, HALO, BRANCH_W), lambda b, j: (b, jnp.maximum(j * halo_per_tile - 1, 0), 1)),
            pl.BlockSpec((1, HALO, BRANCH_W), lambda b, j: (b, jnp.minimum((j + 1) * halo_per_tile, n_halo - 1), 1)),
            col(2),
            col(3),
            pl.BlockSpec((1, CONV_W, BRANCH_W), lambda b, j: (layer, 0, 0)),
            vec,
            vec,
            pl.BlockSpec((1, SGU_GROUPS, CHUNK, CHUNK), lambda b, j: (layer, 0, 0, 0)),
            pl.BlockSpec((1, CHUNK, BRANCH_W), lambda b, j: (layer, 0, 0)),
        ],
        out_specs=[pl.BlockSpec((1, TILE, BRANCH_W), lambda b, j: (b, j, 0))] * 2,
        out_shape=[jax.ShapeDtypeStruct((BATCH, TOK, BRANCH_W), BF16)] * 2,
        compiler_params=_params(32, 2),
        name="local",
    )(loc, loc, loc, loc, loc, loc, conv_w, ln_g, ln_b, sgu_w, sgu_b)


def _merge_kernel(*refs, tile, with_ctx):
    h_ref, ml_ref = refs[0], refs[1]
    mc_ref = refs[2] if with_ctx else None
    g_ref, ya_ref, yb_ref, yc_ref, yd_ref, wg_ref, wb_ref, wo_ref, o_ref = refs[2 + with_ctx:]
    j = pl.program_id(1)
    shift, scale, gate = _row_mods(j, tile, ml_ref, mc_ref, (0, 1, 2))
    h = h_ref[0]
    a = _rms_modulate(h, g_ref[0], shift, scale).astype(BF16)
    ys = (ya_ref[0], yb_ref[0], yc_ref[0], yd_ref[0])
    acc = jnp.zeros((tile, D_MODEL), F32)
    for k in range(N_BRANCH):
        logits = _dot(a, wg_ref[0, :, k * D_MODEL:(k + 1) * D_MODEL])
        acc = acc + jax.nn.sigmoid(logits) * _dot(ys[k], wb_ref[0, k])
    o_ref[0] = h + gate * _dot(acc.astype(BF16), wo_ref[0])


def _merge_call(layer, h, mods, gains, ya, yb, yc, yd, w_gate, w_branch, w_out, *, tile, n_tiles, with_ctx):
    tok = pl.BlockSpec((1, tile, D_MODEL), lambda b, j: (b, j, 0))
    br = pl.BlockSpec((1, tile, BRANCH_W), lambda b, j: (b, j, 0))
    mod_specs = _mod_specs(layer, with_ctx)
    return pl.pallas_call(
        functools.partial(_merge_kernel, tile=tile, with_ctx=with_ctx),
        grid=(BATCH, n_tiles),
        in_specs=[
            tok,
            *mod_specs,
            _gain_spec(layer, 0),
            br, br, br, br,
            _resident((1, D_MODEL, GATE_W), lambda b, j: (layer, 0, 0)),
            _resident((1, N_BRANCH, BRANCH_W, D_MODEL), lambda b, j: (layer, 0, 0, 0)),
            _resident((1, D_MODEL, D_MODEL), lambda b, j: (layer, 0, 0)),
        ],
        out_specs=tok,
        out_shape=jax.ShapeDtypeStruct((BATCH, tile * n_tiles, D_MODEL), F32),
        compiler_params=_params(56, 2),
        name="merge",
    )(h, *([mods] * len(mod_specs)), gains, ya, yb, yc, yd, w_gate, w_branch, w_out)


FF_CHUNK = 1024


def _ffn_kernel(*refs, tile, with_ctx, final):
    h_ref, ml_ref = refs[0], refs[1]
    mc_ref = refs[2] if with_ctx else None
    g_ref, w1_ref, w2_ref = refs[2 + with_ctx:5 + with_ctx]
    fg_ref = refs[5 + with_ctx] if final else None
    o_ref = refs[-1]
    j = pl.program_id(1)
    shift, scale, gate = _row_mods(j, tile, ml_ref, mc_ref, (3, 4, 5))
    h = h_ref[0]
    a = _rms_modulate(h, g_ref[0], shift, scale).astype(BF16)
    acc = jnp.zeros((tile, D_MODEL), F32)
    for c in range(D_FF // FF_CHUNK):
        cs = slice(c * FF_CHUNK, (c + 1) * FF_CHUNK)
        t = jnp.maximum(_dot(a, w1_ref[0, :, cs]), 0.0)
        acc = acc + _dot((t * t).astype(BF16), w2_ref[0, cs, :])
    out = h + gate * acc
    if final:
        out = out * lax.rsqrt(jnp.mean(out * out, axis=-1, keepdims=True) + EPS) * fg_ref[...]
    o_ref[0] = out


def _ffn_call(layer, h, mods, gains, w1, w2, final_g=None, *, tile, n_tiles, with_ctx):
    tok = pl.BlockSpec((1, tile, D_MODEL), lambda b, j: (b, j, 0))
    final = final_g is not None
    mod_specs = _mod_specs(layer, with_ctx)
    in_specs = [
        tok,
        *mod_specs,
        _gain_spec(layer, 1),
        _resident((1, D_MODEL, D_FF), lambda b, j: (layer, 0, 0)),
        _resident((1, D_FF, D_MODEL), lambda b, j: (layer, 0, 0)),
    ]
    args = [h, *([mods] * len(mod_specs)), gains, w1, w2]
    if final:
        in_specs.append(pl.BlockSpec((1, D_MODEL), lambda b, j: (0, 0)))
        args.append(final_g)
    return pl.pallas_call(
        functools.partial(_ffn_kernel, tile=tile, with_ctx=with_ctx, final=final),
        grid=(BATCH, n_tiles),
        in_specs=in_specs,
        out_specs=tok,
        out_shape=jax.ShapeDtypeStruct((BATCH, tile * n_tiles, D_MODEL), F32),
        compiler_params=_params(58, 2),
        name="ffn_final" if final else "ffn",
    )(*args)


def kernel(x, c, ctx, c_ctx, ada_w, ada_b, norm_g, final_g, w_in, w_branch, w_out, s5_lam_re, s5_lam_im, s5_log_dt, s5_b_re, s5_b_im, s5_c_re, s5_c_im, s5_d, s5_glu_w, s5_glu_b, na_rpb, conv_w, sgu_ln_g, sgu_ln_b, sgu_w, sgu_b, w_ff1, w_ff2):
    h = jnp.concatenate([x, ctx], axis=1)
    cc = jnp.concatenate([c, c_ctx[None], jnp.zeros((MOD_ROWS - BATCH - 1, D_MODEL), F32)], axis=0)
    eye = jnp.eye(S5_GROUPS, dtype=F32)
    bre = jnp.einsum('ldgph,gk->ldghkp', s5_b_re, eye).reshape(DEPTH, 2, BRANCH_W, S5_LANES)
    bim = jnp.einsum('ldgph,gk->ldghkp', s5_b_im, eye).reshape(DEPTH, 2, BRANCH_W, S5_LANES)
    cre = jnp.einsum('ldghp,gk->ldkpgh', s5_c_re, eye).reshape(DEPTH, 2, S5_LANES, BRANCH_W)
    cim = jnp.einsum('ldghp,gk->ldkpgh', s5_c_im, eye).reshape(DEPTH, 2, S5_LANES, BRANCH_W)
    lam_re = s5_lam_re.reshape(DEPTH, 2, 1, S5_LANES)
    lam_im = s5_lam_im.reshape(DEPTH, 2, 1, S5_LANES)
    log_dt = jnp.repeat(s5_log_dt, S5_STATE, axis=-1).reshape(DEPTH, 2, 1, S5_LANES)
    sgu_bias = jnp.repeat(jnp.swapaxes(sgu_b, 1, 2), SGU_GROUP_W, axis=2)
    w_slots, w_gate = w_in[:, :, :SLOT_W].astype(BF16), w_in[:, :, SLOT_W:].astype(BF16)
    w_branch_b, w_out_b = w_branch.astype(BF16), w_out.astype(BF16)
    w_ff1_b, w_ff2_b = w_ff1.astype(BF16), w_ff2.astype(BF16)
    glu_w_b, sgu_w_b = s5_glu_w.astype(BF16), sgu_w.astype(BF16)
    gains = norm_g.reshape(DEPTH * 2, 1, D_MODEL)
    vec3 = lambda p: p.reshape(DEPTH, 1, BRANCH_W)

    mods = _mod_call(cc, ada_w, ada_b).reshape(DEPTH * MOD_ROWS, 1, MOD_W)
    a_bar, bb, cm = _s5_prep_call(lam_re, lam_im, log_dt, bre, bim, cre, cim)
    na_bias = _na_bias_call(na_rpb)

    for i in range(DEPTH):
        last = i == DEPTH - 1
        att, loc = _proj_call(i, h, mods, gains, w_slots)
        yf, yr = _s5_scan_call(i, att, a_bar, bb, cm)
        ya = _s5_glu_call(i, yf, yr, att, vec3(s5_d), glu_w_b, vec3(s5_glu_b))
        yb = _na_call(i, att, na_bias)
        yc, yd = _local_call(i, loc, conv_w, vec3(sgu_ln_g), vec3(sgu_ln_b), sgu_w_b, sgu_bias)
        tiling = dict(tile=TILE_LATENT, n_tiles=SEQ // TILE_LATENT, with_ctx=False) if last else dict(
            tile=TILE, n_tiles=N_TILES, with_ctx=True)
        h = _merge_call(i, h, mods, gains, ya, yb, yc, yd, w_gate, w_branch_b, w_out_b, **tiling)
        h = _ffn_call(i, h, mods, gains, w_ff1_b, w_ff2_b, final_g[None] if last else None, **tiling)
    return h
```

```python
import functools
import math

import jax
import jax.numpy as jnp
from jax import lax
from jax.experimental import pallas as pl
from jax.experimental.pallas import tpu as pltpu

F32 = jnp.float32
BF16 = jnp.bfloat16

D_MODEL = 1024
BATCH = 8
SEQ = 2048
DEPTH = 4
GRID_W = 64
GRID_H = SEQ // GRID_W
CTX_LEN = 256
TOK = SEQ + CTX_LEN
N_BRANCH = 4
BRANCH_W = D_MODEL // 4
S5_GROUP = 16
S5_GROUPS = BRANCH_W // S5_GROUP
S5_STATE = 64
S5_LANES = S5_GROUPS * S5_STATE
NA_HEAD_DIM = 64
NA_HEADS = BRANCH_W // NA_HEAD_DIM
NA_WIN_ROWS = 8
NA_WIN_COLS = 16
NA_SCALE = NA_HEAD_DIM ** -0.5
CONV_W = 3
CHUNK = 128
SGU_GROUPS = 4
SGU_GROUP_W = BRANCH_W // SGU_GROUPS
D_FF = 4 * D_MODEL
N_MOD = 6
MOD_W = N_MOD * D_MODEL
EPS = 1e-6
NEG_INF = -1e30
N_SLOTS = 9
SLOT_W = N_SLOTS * BRANCH_W
ATT_W = 4 * BRANCH_W
LOC_W = 5 * BRANCH_W
GATE_W = N_BRANCH * D_MODEL

LANES = 128
MOD_ROWS = 16
MOD_CTX_ROW = BATCH
TILE = 768
N_TILES = TOK // TILE
TILE_LATENT = 1024
SUB = 256
S5_T = 128
S5_ROWS = S5_T * BATCH
S5_STEPS = TOK // S5_T
S5_LAT_STEPS = SEQ // S5_T
S5_SLABS = BRANCH_W // LANES
NA_QB = 4 * GRID_W
NA_WIN_KROWS = 12
NA_WIN_KEYS = NA_WIN_KROWS * GRID_W
NA_LAT_STEPS = SEQ // NA_QB
NA_STEPS = TOK // NA_QB
MIB = 1024 * 1024


def _params(vmem_mib, n_axes):
    return pltpu.CompilerParams(
        dimension_semantics=("arbitrary",) * n_axes,
        vmem_limit_bytes=vmem_mib * MIB,
    )


def _resident(shape, index_map):
    return pl.BlockSpec(shape, index_map, pipeline_mode=pl.Buffered(1))


def _dot(a, b):
    return jnp.dot(a, b, preferred_element_type=F32)


def _dot_nt(a, b):
    return lax.dot_general(a, b, (((1,), (1,)), ((), ())), preferred_element_type=F32)


def _norm_modulate(h, gain, shift):
    return h * lax.rsqrt(jnp.mean(h * h, axis=-1, keepdims=True) + EPS) * gain + shift


def _sub_mods(j, row0, tile, ml_ref, mc_ref, ks):
    vec = lambda ref, k: ref[0, :, k * D_MODEL:(k + 1) * D_MODEL]
    ctx_row0 = SEQ % tile
    if mc_ref is None or row0 < ctx_row0:
        return [vec(ml_ref, k) for k in ks]
    in_ctx = j == TOK // tile - 1
    return [jnp.where(in_ctx, vec(mc_ref, k), vec(ml_ref, k)) for k in ks]


def _mod_specs(layer, with_ctx=True):
    specs = [pl.BlockSpec((1, 1, MOD_W), lambda b, j: (layer * MOD_ROWS + b, 0, 0))]
    if with_ctx:
        specs.append(pl.BlockSpec((1, 1, MOD_W), lambda b, j: (layer * MOD_ROWS + MOD_CTX_ROW, 0, 0)))
    return specs


def _gain_spec(layer, which):
    return pl.BlockSpec((1, 1, D_MODEL), lambda b, j: (2 * layer + which, 0, 0))


MOD_NT = 1536


def _mod_kernel(cc_ref, w_ref, b_ref, o_ref):
    s = jax.nn.silu(cc_ref[...])
    w = w_ref[0]
    s_hi = s.astype(BF16)
    s_lo = (s - s_hi.astype(F32)).astype(BF16)
    w_hi = w.astype(BF16)
    w_lo = (w - w_hi.astype(F32)).astype(BF16)
    o_ref[0] = _dot(s_hi, w_hi) + _dot(s_lo, w_hi) + _dot(s_hi, w_lo) + b_ref[0]


def _mod_call(cc, ada_w, ada_b):
    return pl.pallas_call(
        _mod_kernel,
        grid=(DEPTH, MOD_W // MOD_NT),
        in_specs=[
            pl.BlockSpec((MOD_ROWS, D_MODEL), lambda i, j: (0, 0)),
            pl.BlockSpec((1, D_MODEL, MOD_NT), lambda i, j: (i, 0, j)),
            pl.BlockSpec((1, 1, MOD_NT), lambda i, j: (i, 0, j)),
        ],
        out_specs=pl.BlockSpec((1, MOD_ROWS, MOD_NT), lambda i, j: (i, 0, j)),
        out_shape=jax.ShapeDtypeStruct((DEPTH, MOD_ROWS, MOD_W), F32),
        compiler_params=_params(40, 2),
        name="mod",
    )(cc, ada_w, ada_b.reshape(DEPTH, 1, MOD_W))


def _proj_kernel(h_ref, ml_ref, mc_ref, g_ref, w_ref, att_ref, loc_ref):
    j = pl.program_id(1)
    for r0 in range(0, TILE, SUB):
        rs = slice(r0, r0 + SUB)
        shift, scale = _sub_mods(j, r0, TILE, ml_ref, mc_ref, (0, 1))
        a = _norm_modulate(h_ref[0, rs, :], g_ref[0] * (1.0 + scale), shift).astype(BF16)
        att_ref[0, rs, :] = _dot(a, w_ref[0, :, 0:ATT_W]).astype(BF16)
        p = _dot(a, w_ref[0, :, ATT_W:SLOT_W])
        loc_ref[0, rs, 0:BRANCH_W] = p[:, 0:BRANCH_W]
        loc_ref[0, rs, BRANCH_W:2 * BRANCH_W] = p[:, BRANCH_W:2 * BRANCH_W] * p[:, 2 * BRANCH_W:3 * BRANCH_W]
        loc_ref[0, rs, 2 * BRANCH_W:4 * BRANCH_W] = p[:, 3 * BRANCH_W:5 * BRANCH_W]


def _proj_call(layer, h, mods, gains, w_in):
    return pl.pallas_call(
        _proj_kernel,
        grid=(BATCH, N_TILES),
        in_specs=[
            pl.BlockSpec((1, TILE, D_MODEL), lambda b, j: (b, j, 0)),
            *_mod_specs(layer),
            _gain_spec(layer, 0),
            _resident((1, D_MODEL, SLOT_W), lambda b, j: (layer, 0, 0)),
        ],
        out_specs=[
            pl.BlockSpec((1, TILE, ATT_W), lambda b, j: (b, j, 0)),
            pl.BlockSpec((1, TILE, 4 * BRANCH_W), lambda b, j: (b, j, 0)),
        ],
        out_shape=[
            jax.ShapeDtypeStruct((BATCH, TOK, ATT_W), BF16),
            jax.ShapeDtypeStruct((BATCH, TOK, 4 * BRANCH_W), F32),
        ],
        compiler_params=_params(48, 2),
        name="proj",
    )(h, mods, mods, gains, w_in)


def _s5_prep_kernel(lre_ref, lim_ref, ldt_ref, bre_ref, bim_ref, cre_ref, cim_ref, a_ref, bb_ref, cm_ref):
    lam_re = lre_ref[0, 0]
    lam_im = lim_ref[0, 0]
    dt = jnp.exp(ldt_ref[0, 0])
    zr, zi = lam_re * dt, lam_im * dt
    mag = jnp.exp(zr)
    ar, ai = mag * jnp.cos(zi), mag * jnp.sin(zi)
    nr, ni = ar - 1.0, ai
    den = lam_re * lam_re + lam_im * lam_im
    cr = (nr * lam_re + ni * lam_im) / den
    ci = (ni * lam_re - nr * lam_im) / den
    b_row_g = lax.shift_right_logical(lax.broadcasted_iota(jnp.int32, (BRANCH_W, 1), 0), int(math.log2(S5_GROUP)))
    b_col_g = lax.shift_right_logical(lax.broadcasted_iota(jnp.int32, (1, S5_LANES), 1), int(math.log2(S5_STATE)))
    b_own = b_row_g == b_col_g
    expand_b = lambda ref: jnp.where(b_own, jnp.concatenate([ref[0, 0]] * (S5_LANES // LANES), axis=1), 0.0)
    bre, bim = expand_b(bre_ref), expand_b(bim_ref)
    bb_ref[0, 0, :, 0:S5_LANES] = (cr * bre - ci * bim).astype(BF16)
    bb_ref[0, 0, :, S5_LANES:2 * S5_LANES] = (cr * bim + ci * bre).astype(BF16)
    a_ref[0, 0, :, 0:S5_LANES] = jnp.broadcast_to(ar, (BATCH, S5_LANES))
    a_ref[0, 0, :, S5_LANES:2 * S5_LANES] = jnp.broadcast_to(ai, (BATCH, S5_LANES))
    c_row_g = lax.shift_right_logical(lax.broadcasted_iota(jnp.int32, (S5_LANES, 1), 0), int(math.log2(S5_STATE)))
    c_col_g = lax.shift_right_logical(lax.broadcasted_iota(jnp.int32, (1, BRANCH_W), 1), int(math.log2(S5_GROUP)))
    c_own = c_row_g == c_col_g
    expand_c = lambda ref: jnp.where(c_own, jnp.concatenate([ref[0, 0]] * (BRANCH_W // LANES), axis=1), 0.0)
    cm_ref[0, 0, 0:S5_LANES, :] = expand_c(cre_ref).astype(BF16)
    cm_ref[0, 0, S5_LANES:2 * S5_LANES, :] = (-expand_c(cim_ref)).astype(BF16)


def _s5_prep_call(lam_re, lam_im, log_dt, bre, bim, cre, cim):
    vec = pl.BlockSpec((1, 1, 1, S5_LANES), lambda i, d: (i, d, 0, 0))
    bmat = pl.BlockSpec((1, 1, BRANCH_W, LANES), lambda i, d: (i, d, 0, 0))
    cmat = pl.BlockSpec((1, 1, S5_LANES, LANES), lambda i, d: (i, d, 0, 0))
    return pl.pallas_call(
        _s5_prep_kernel,
        grid=(DEPTH, 2),
        in_specs=[vec, vec, vec, bmat, bmat, cmat, cmat],
        out_specs=[
            pl.BlockSpec((1, 1, BATCH, 2 * S5_LANES), lambda i, d: (i, d, 0, 0)),
            pl.BlockSpec((1, 1, BRANCH_W, 2 * S5_LANES), lambda i, d: (i, d, 0, 0)),
            pl.BlockSpec((1, 1, 2 * S5_LANES, BRANCH_W), lambda i, d: (i, d, 0, 0)),
        ],
        out_shape=[
            jax.ShapeDtypeStruct((DEPTH, 2, BATCH, 2 * S5_LANES), F32),
            jax.ShapeDtypeStruct((DEPTH, 2, BRANCH_W, 2 * S5_LANES), BF16),
            jax.ShapeDtypeStruct((DEPTH, 2, 2 * S5_LANES, BRANCH_W), BF16),
        ],
        compiler_params=_params(32, 2),
        name="s5_prep",
    )(lam_re, lam_im, log_dt, bre, bim, cre, cim)


def _s5_fwd_block(i):
    return jnp.where(i < S5_STEPS - S5_LAT_STEPS, S5_LAT_STEPS + i, i - (S5_STEPS - S5_LAT_STEPS))


def _s5_rev_block(i):
    return S5_STEPS - 1 - i


def _to_time_major(u_ref, slab_ref):
    for b in range(BATCH):
        ub = u_ref[b].astype(F32)
        for s in range(S5_SLABS):
            slab_ref[s, pl.ds(b, S5_T, stride=BATCH), :] = ub[:, s * LANES:(s + 1) * LANES]


def _s5_scan_kernel(uf_ref, ur_ref, a_ref, bb_ref, cm_ref, yf_ref, yr_ref, xf_ref, xr_ref, st_ref, usf_ref, usr_ref):
    i = pl.program_id(0)

    @pl.when(i == 0)
    def _():
        st_ref[...] = jnp.zeros_like(st_ref)

    mm_rows = 256
    dirs = ((uf_ref, yf_ref, xf_ref, usf_ref), (ur_ref, yr_ref, xr_ref, usr_ref))
    for d, (u_ref, y_ref, x_ref, us_ref) in enumerate(dirs):
        _to_time_major(u_ref, us_ref)
        for c in range(S5_ROWS // mm_rows):
            rs = slice(c * mm_rows, (c + 1) * mm_rows)
            u = jnp.concatenate([us_ref[s, rs, :] for s in range(S5_SLABS)], axis=1).astype(BF16)
            x_ref[rs, :] = _dot(u, bb_ref[0, d])
    for d, (u_ref, y_ref, x_ref, us_ref) in enumerate(dirs):
        ar = a_ref[0, d, :, 0:S5_LANES]
        ai = a_ref[0, d, :, S5_LANES:2 * S5_LANES]
        hr, hi = st_ref[d, :, 0:S5_LANES], st_ref[d, :, S5_LANES:2 * S5_LANES]
        for t in range(S5_T):
            r = (t if d == 0 else S5_T - 1 - t) * BATCH
            xr = x_ref[r:r + BATCH, 0:S5_LANES]
            xi = x_ref[r:r + BATCH, S5_LANES:2 * S5_LANES]
            hr, hi = ar * hr - ai * hi + xr, ar * hi + ai * hr + xi
            x_ref[r:r + BATCH, 0:S5_LANES] = hr
            x_ref[r:r + BATCH, S5_LANES:2 * S5_LANES] = hi
        st_ref[d, :, 0:S5_LANES] = hr
        st_ref[d, :, S5_LANES:2 * S5_LANES] = hi
        for c in range(S5_ROWS // mm_rows):
            rs = slice(c * mm_rows, (c + 1) * mm_rows)
            y_ref[rs, :] = _dot(x_ref[rs, :].astype(BF16), cm_ref[0, d])


def _s5_scan_call(layer, att, a_bar, bb, cm):
    u_blk = lambda f: pl.BlockSpec((BATCH, S5_T, BRANCH_W), lambda i: (0, f(i), 0))
    y_blk = lambda f: pl.BlockSpec((S5_ROWS, BRANCH_W), lambda i: (f(i), 0))
    return pl.pallas_call(
        _s5_scan_kernel,
        grid=(S5_STEPS,),
        in_specs=[
            u_blk(_s5_fwd_block),
            u_blk(_s5_rev_block),
            pl.BlockSpec((1, 2, BATCH, 2 * S5_LANES), lambda i: (layer, 0, 0, 0)),
            pl.BlockSpec((1, 2, BRANCH_W, 2 * S5_LANES), lambda i: (layer, 0, 0, 0)),
            pl.BlockSpec((1, 2, 2 * S5_LANES, BRANCH_W), lambda i: (layer, 0, 0, 0)),
        ],
        out_specs=[y_blk(_s5_fwd_block), y_blk(_s5_rev_block)],
        out_shape=[jax.ShapeDtypeStruct((TOK * BATCH, BRANCH_W), F32)] * 2,
        scratch_shapes=[
            pltpu.VMEM((S5_ROWS, 2 * S5_LANES), F32),
            pltpu.VMEM((S5_ROWS, 2 * S5_LANES), F32),
            pltpu.VMEM((2, BATCH, 2 * S5_LANES), F32),
            pltpu.VMEM((S5_SLABS, S5_ROWS, LANES), F32),
            pltpu.VMEM((S5_SLABS, S5_ROWS, LANES), F32),
        ],
        compiler_params=_params(48, 1),
        name="s5_scan",
    )(att, att, a_bar, bb, cm)


def _s5_glu_kernel(yf_ref, yr_ref, u_ref, d_ref, w_ref, b_ref, o_ref, ys_ref):
    y = yf_ref[...] + yr_ref[...]
    for s in range(S5_SLABS):
        ys_ref[s] = y[:, s * LANES:(s + 1) * LANES]
    for b in range(BATCH):
        yb = jnp.concatenate([ys_ref[s, pl.ds(b, S5_T, stride=BATCH), :] for s in range(S5_SLABS)], axis=1)
        g = jax.nn.gelu(yb + u_ref[b].astype(F32) * d_ref[0])
        o_ref[b] = (g * jax.nn.sigmoid(_dot(g.astype(BF16), w_ref[0]) + b_ref[0])).astype(BF16)


def _s5_glu_call(layer, yf, yr, att, d_skip, w, b):
    rows = pl.BlockSpec((S5_ROWS, BRANCH_W), lambda i: (i, 0))
    vec = pl.BlockSpec((1, 1, BRANCH_W), lambda i: (layer, 0, 0))
    tok = pl.BlockSpec((BATCH, S5_T, BRANCH_W), lambda i: (0, i, 0))
    return pl.pallas_call(
        _s5_glu_kernel,
        grid=(S5_STEPS,),
        in_specs=[rows, rows, tok, vec, pl.BlockSpec((1, BRANCH_W, BRANCH_W), lambda i: (layer, 0, 0)), vec],
        out_specs=tok,
        out_shape=jax.ShapeDtypeStruct((BATCH, TOK, BRANCH_W), BF16),
        scratch_shapes=[pltpu.VMEM((S5_SLABS, S5_ROWS, LANES), F32)],
        compiler_params=_params(32, 1),
        name="s5_glu",
    )(yf, yr, att, d_skip, w, b)


NA_CASE_FIRST, NA_CASE_INNER, NA_CASE_LAST, NA_CASE_CTX = 0, 1, 2, 3
N_RPB = (2 * NA_WIN_ROWS - 1) * (2 * NA_WIN_COLS - 1)


def _na_window_start(row0):
    return min(max(row0 - NA_WIN_ROWS // 2, 0), GRID_H - NA_WIN_KROWS)


def _na_bias_kernel(rpb_ref, o_ref):
    layer, h = pl.program_id(0), pl.program_id(1)
    base = (layer * NA_HEADS + h) * N_RPB
    qq = lax.broadcasted_iota(jnp.int32, (GRID_W, 2 * GRID_W), 0)
    lane = lax.broadcasted_iota(jnp.int32, (GRID_W, 2 * GRID_W), 1)
    kk = jnp.bitwise_and(lane, GRID_W - 1)
    second = lane >= GRID_W
    dcm = jnp.clip(kk - qq + NA_WIN_COLS - 1, 0, 2 * NA_WIN_COLS - 2)
    cs = jnp.clip(qq - NA_WIN_COLS // 2, 0, GRID_W - NA_WIN_COLS)
    col_ok = (kk >= cs) & (kk < cs + NA_WIN_COLS)
    neg = jnp.full((GRID_W, 2 * GRID_W), NEG_INF, F32)
    n_dr, n_dc = 2 * NA_WIN_ROWS - 1, 2 * NA_WIN_COLS - 1
    table = []
    for dr in range(n_dr):
        t = neg
        for dc in range(n_dc):
            t = jnp.where(dcm == dc, rpb_ref[base + dr * n_dc + dc], t)
        table.append(jnp.where(col_ok, t, NEG_INF))
    for case, row0 in ((NA_CASE_FIRST, 0), (NA_CASE_INNER, 4), (NA_CASE_LAST, GRID_H - 4)):
        start = _na_window_start(row0)
        for i in range(4):
            r = row0 + i
            row_start = min(max(r - NA_WIN_ROWS // 2, 0), GRID_H - NA_WIN_ROWS)
            for m in range(NA_WIN_KROWS // 2):
                halves = []
                for jj in (2 * m, 2 * m + 1):
                    kr = start + jj
                    valid = row_start <= kr < row_start + NA_WIN_ROWS
                    halves.append(table[kr - r + NA_WIN_ROWS - 1] if valid else neg)
                o_ref[0, case, 0, i * GRID_W:(i + 1) * GRID_W, m * 2 * GRID_W:(m + 1) * 2 * GRID_W] = (
                    jnp.where(second, halves[1], halves[0]))
    o_ref[0, NA_CASE_CTX, 0] = jnp.full((NA_QB, NA_WIN_KEYS), NEG_INF, F32)


def _na_bias_call(rpb):
    return pl.pallas_call(
        _na_bias_kernel,
        grid=(DEPTH, NA_HEADS),
        in_specs=[pl.BlockSpec(memory_space=pltpu.SMEM)],
        out_specs=pl.BlockSpec((1, 4, 1, NA_QB, NA_WIN_KEYS), lambda i, h: (i, 0, h, 0, 0)),
        out_shape=jax.ShapeDtypeStruct((DEPTH, 4, NA_HEADS, NA_QB, NA_WIN_KEYS), F32),
        compiler_params=_params(32, 2),
        name="na_bias",
    )(rpb.reshape(-1))


def _na_case(j):
    return jnp.where(j == NA_LAT_STEPS, NA_CASE_CTX,
                     jnp.where(j == 0, NA_CASE_FIRST, jnp.where(j == NA_LAT_STEPS - 1, NA_CASE_LAST, NA_CASE_INNER)))


def _na_kernel(q_ref, k_ref, v_ref, bias_ref, o_ref):
    j = pl.program_id(1)
    row0 = jnp.minimum(j, NA_LAT_STEPS - 1) * 4
    start = jnp.clip(row0 - NA_WIN_ROWS // 2, 0, GRID_H - NA_WIN_KROWS)
    k0 = pl.multiple_of(start * GRID_W, GRID_W)
    q = q_ref[0] * NA_SCALE
    kw, vw = k_ref[0, pl.ds(k0, NA_WIN_KEYS), :], v_ref[0, pl.ds(k0, NA_WIN_KEYS), :]
    kc, vc = k_ref[0, SEQ:TOK, :], v_ref[0, SEQ:TOK, :]
    head = lax.shift_right_logical(lax.broadcasted_iota(jnp.int32, (1, BRANCH_W), 1), int(math.log2(NA_HEAD_DIM)))
    acc = jnp.zeros((NA_QB, BRANCH_W), F32)
    for h in range(NA_HEADS):
        in_head = head == h
        qh = jnp.where(in_head, q, jnp.zeros_like(q))
        sw = _dot_nt(qh, kw) + bias_ref[0, 0, h]
        sc = _dot_nt(qh, kc)
        m = jnp.maximum(jnp.max(sw, axis=-1, keepdims=True), jnp.max(sc, axis=-1, keepdims=True))
        pw, pc = jnp.exp(sw - m), jnp.exp(sc - m)
        norm = jnp.sum(pw, axis=-1, keepdims=True) + jnp.sum(pc, axis=-1, keepdims=True)
        o = _dot(pw.astype(BF16), vw) + _dot(pc.astype(BF16), vc)
        acc = jnp.where(in_head, o * (1.0 / norm), acc)
    o_ref[0] = acc.astype(BF16)


def _na_call(layer, att, bias):
    return pl.pallas_call(
        _na_kernel,
        grid=(BATCH, NA_STEPS),
        in_specs=[
            pl.BlockSpec((1, NA_QB, BRANCH_W), lambda b, j: (b, j, 1)),
            pl.BlockSpec((1, TOK, BRANCH_W), lambda b, j: (b, 0, 2)),
            pl.BlockSpec((1, TOK, BRANCH_W), lambda b, j: (b, 0, 3)),
            pl.BlockSpec((1, 1, NA_HEADS, NA_QB, NA_WIN_KEYS), lambda b, j: (layer, _na_case(j), 0, 0, 0)),
        ],
        out_specs=pl.BlockSpec((1, NA_QB, BRANCH_W), lambda b, j: (b, j, 0)),
        out_shape=jax.ShapeDtypeStruct((BATCH, TOK, BRANCH_W), BF16),
        compiler_params=_params(40, 2),
        name="na",
    )(att, att, att, bias)


HALO = 8


def _local_kernel(cb_ref, z_ref, zp_ref, zn_ref, su_ref, sv_ref, cw_ref, lg_ref, lb_ref, sw_ref, sb_ref,
                  yc_ref, yd_ref):
    j = pl.program_id(1)
    local = lax.broadcasted_iota(jnp.int32, (TILE, 1), 0)
    rows = j * TILE + local
    z = z_ref[0]
    prev = jnp.where(local == 0, zp_ref[0, HALO - 1:HALO, :], pltpu.roll(z, 1, 0))
    nxt = jnp.where(local == TILE - 1, zn_ref[0, 0:1, :], pltpu.roll(z, TILE - 1, 0))
    prev = jnp.where((rows == 0) | (rows == SEQ), 0.0, prev)
    nxt = jnp.where((rows == SEQ - 1) | (rows == TOK - 1), 0.0, nxt)
    y = cw_ref[0, 0:1, :] * prev + cw_ref[0, 1:2, :] * z + cw_ref[0, 2:3, :] * nxt
    yc_ref[0] = (cb_ref[0] * y).astype(BF16)

    v = sv_ref[0]
    mu = jnp.mean(v, axis=-1, keepdims=True)
    var = jnp.mean(jnp.square(v - mu), axis=-1, keepdims=True)
    vn = ((v - mu) * lax.rsqrt(var + EPS) * lg_ref[0] + lb_ref[0]).astype(BF16)
    group = lax.shift_right_logical(lax.broadcasted_iota(jnp.int32, (1, BRANCH_W), 1), int(math.log2(SGU_GROUP_W)))
    zero = jnp.zeros((CHUNK, BRANCH_W), BF16)
    for c in range(TILE // CHUNK):
        rs = slice(c * CHUNK, (c + 1) * CHUNK)
        vc = vn[rs, :]
        stacked = jnp.concatenate([jnp.where(group == g, vc, zero) for g in range(SGU_GROUPS)], axis=0)
        zc = _dot(sw_ref[0], stacked) + sb_ref[0]
        yd_ref[0, rs, :] = (su_ref[0, rs, :] * zc).astype(BF16)


def _local_call(layer, loc, conv_w, ln_g, ln_b, sgu_w, sgu_b):
    col = lambda k: pl.BlockSpec((1, TILE, BRANCH_W), lambda b, j: (b, j, k))
    halo_per_tile = TILE // HALO
    n_halo = TOK // HALO
    vec = pl.BlockSpec((1, 1, BRANCH_W), lambda b, j: (layer, 0, 0))
    return pl.pallas_call(
        _local_kernel,
        grid=(BATCH, N_TILES),
        in_specs=[
            col(0),
            col(1),
            pl.BlockSpec((1, HALO, BRANCH_W), lambda b, j: (b, jnp.maximum(j * halo_per_tile - 1, 0), 1)),
            pl.BlockSpec((1, HALO, BRANCH_W), lambda b, j: (b, jnp.minimum((j + 1) * halo_per_tile, n_halo - 1), 1)),
            col(2),
            col(3),
            pl.BlockSpec((1, CONV_W, BRANCH_W), lambda b, j: (layer, 0, 0)),
            vec,
            vec,
            pl.BlockSpec((1, CHUNK, SGU_GROUPS * CHUNK), lambda b, j: (layer, 0, 0)),
            pl.BlockSpec((1, CHUNK, BRANCH_W), lambda b, j: (layer, 0, 0)),
        ],
        out_specs=[pl.BlockSpec((1, TILE, BRANCH_W), lambda b, j: (b, j, 0))] * 2,
        out_shape=[jax.ShapeDtypeStruct((BATCH, TOK, BRANCH_W), BF16)] * 2,
        compiler_params=_params(32, 2),
        name="local",
    )(loc, loc, loc, loc, loc, loc, conv_w, ln_g, ln_b, sgu_w, sgu_b)


def _merge_kernel(*refs, tile, with_ctx):
    h_ref, ml_ref = refs[0], refs[1]
    mc_ref = refs[2] if with_ctx else None
    g_ref, ya_ref, yb_ref, yc_ref, yd_ref, wg_ref, wb_ref, wo_ref, o_ref = refs[2 + with_ctx:]
    j = pl.program_id(1)
    y_refs = (ya_ref, yb_ref, yc_ref, yd_ref)
    for r0 in range(0, tile, SUB):
        rs = slice(r0, r0 + SUB)
        shift, scale, gate = _sub_mods(j, r0, tile, ml_ref, mc_ref, (0, 1, 2))
        h = h_ref[0, rs, :]
        a = _norm_modulate(h, g_ref[0] * (1.0 + scale), shift).astype(BF16)
        acc = jnp.zeros((SUB, D_MODEL), F32)
        for k in range(N_BRANCH):
            logits = _dot(a, wg_ref[0, :, SLOT_W + k * D_MODEL:SLOT_W + (k + 1) * D_MODEL])
            acc = acc + jax.nn.sigmoid(logits) * _dot(y_refs[k][0, rs, :], wb_ref[0, k])
        o_ref[0, rs, :] = h + gate * _dot(acc.astype(BF16), wo_ref[0])


def _merge_call(layer, h, mods, gains, ya, yb, yc, yd, w_in, w_branch, w_out, *, tile, n_tiles, with_ctx):
    tok = pl.BlockSpec((1, tile, D_MODEL), lambda b, j: (b, j, 0))
    br = pl.BlockSpec((1, tile, BRANCH_W), lambda b, j: (b, j, 0))
    mod_specs = _mod_specs(layer, with_ctx)
    return pl.pallas_call(
        functools.partial(_merge_kernel, tile=tile, with_ctx=with_ctx),
        grid=(BATCH, n_tiles),
        in_specs=[
            tok,
            *mod_specs,
            _gain_spec(layer, 0),
            br, br, br, br,
            _resident((1, D_MODEL, SLOT_W + GATE_W), lambda b, j: (layer, 0, 0)),
            _resident((1, N_BRANCH, BRANCH_W, D_MODEL), lambda b, j: (layer, 0, 0, 0)),
            _resident((1, D_MODEL, D_MODEL), lambda b, j: (layer, 0, 0)),
        ],
        out_specs=tok,
        out_shape=jax.ShapeDtypeStruct((BATCH, tile * n_tiles, D_MODEL), F32),
        compiler_params=_params(56, 2),
        name="merge",
    )(h, *([mods] * len(mod_specs)), gains, ya, yb, yc, yd, w_in, w_branch, w_out)


FF_CHUNK = 1024


def _ffn_kernel(*refs, tile, with_ctx, final):
    h_ref, ml_ref = refs[0], refs[1]
    mc_ref = refs[2] if with_ctx else None
    g_ref, w1_ref, w2_ref = refs[2 + with_ctx:5 + with_ctx]
    fg_ref = refs[5 + with_ctx] if final else None
    o_ref = refs[-1]
    j = pl.program_id(1)
    for r0 in range(0, tile, SUB):
        rs = slice(r0, r0 + SUB)
        shift, scale, gate = _sub_mods(j, r0, tile, ml_ref, mc_ref, (3, 4, 5))
        h = h_ref[0, rs, :]
        a = _norm_modulate(h, g_ref[0] * (1.0 + scale), shift).astype(BF16)
        acc = jnp.zeros((SUB, D_MODEL), F32)
        for c in range(D_FF // FF_CHUNK):
            cs = slice(c * FF_CHUNK, (c + 1) * FF_CHUNK)
            t = jnp.maximum(_dot(a, w1_ref[0, :, cs]), 0.0)
            acc = acc + _dot((t * t).astype(BF16), w2_ref[0, cs, :])
        out = h + gate * acc
        if final:
            out = out * lax.rsqrt(jnp.mean(out * out, axis=-1, keepdims=True) + EPS) * fg_ref[...]
        o_ref[0, rs, :] = out


def _ffn_call(layer, h, mods, gains, w1, w2, final_g=None, *, tile, n_tiles, with_ctx):
    tok = pl.BlockSpec((1, tile, D_MODEL), lambda b, j: (b, j, 0))
    final = final_g is not None
    mod_specs = _mod_specs(layer, with_ctx)
    in_specs = [
        tok,
        *mod_specs,
        _gain_spec(layer, 1),
        _resident((1, D_MODEL, D_FF), lambda b, j: (layer, 0, 0)),
        _resident((1, D_FF, D_MODEL), lambda b, j: (layer, 0, 0)),
    ]
    args = [h, *([mods] * len(mod_specs)), gains, w1, w2]
    if final:
        in_specs.append(pl.BlockSpec((1, D_MODEL), lambda b, j: (0, 0)))
        args.append(final_g)
    return pl.pallas_call(
        functools.partial(_ffn_kernel, tile=tile, with_ctx=with_ctx, final=final),
        grid=(BATCH, n_tiles),
        in_specs=in_specs,
        out_specs=tok,
        out_shape=jax.ShapeDtypeStruct((BATCH, tile * n_tiles, D_MODEL), F32),
        compiler_params=_params(58, 2),
        name="ffn_final" if final else "ffn",
    )(*args)


def kernel(x, c, ctx, c_ctx, ada_w, ada_b, norm_g, final_g, w_in, w_branch, w_out, s5_lam_re, s5_lam_im, s5_log_dt, s5_b_re, s5_b_im, s5_c_re, s5_c_im, s5_d, s5_glu_w, s5_glu_b, na_rpb, conv_w, sgu_ln_g, sgu_ln_b, sgu_w, sgu_b, w_ff1, w_ff2):
    h = jnp.concatenate([x, ctx], axis=1)
    cc = jnp.concatenate([c, c_ctx[None], jnp.zeros((MOD_ROWS - BATCH - 1, D_MODEL), F32)], axis=0)
    b_rows = lambda p: jnp.tile(jnp.swapaxes(p, 3, 4).reshape(DEPTH, 2, BRANCH_W, S5_STATE), (1, 1, 1, LANES // S5_STATE))
    c_rows = lambda p: jnp.tile(jnp.swapaxes(p, 3, 4).reshape(DEPTH, 2, S5_LANES, S5_GROUP), (1, 1, 1, LANES // S5_GROUP))
    bre, bim, cre, cim = b_rows(s5_b_re), b_rows(s5_b_im), c_rows(s5_c_re), c_rows(s5_c_im)
    lam_re = s5_lam_re.reshape(DEPTH, 2, 1, S5_LANES)
    lam_im = s5_lam_im.reshape(DEPTH, 2, 1, S5_LANES)
    log_dt = jnp.repeat(s5_log_dt, S5_STATE, axis=-1).reshape(DEPTH, 2, 1, S5_LANES)
    sgu_bias = jnp.repeat(jnp.swapaxes(sgu_b, 1, 2), SGU_GROUP_W, axis=2)
    w_in_b = w_in.astype(BF16)
    w_branch_b, w_out_b = w_branch.astype(BF16), w_out.astype(BF16)
    w_ff1_b, w_ff2_b = w_ff1.astype(BF16), w_ff2.astype(BF16)
    glu_w_b = s5_glu_w.astype(BF16)
    sgu_w_b = jnp.swapaxes(sgu_w, 1, 2).reshape(DEPTH, CHUNK, SGU_GROUPS * CHUNK).astype(BF16)
    gains = norm_g.reshape(DEPTH * 2, 1, D_MODEL)
    vec3 = lambda p: p.reshape(DEPTH, 1, BRANCH_W)

    mods = _mod_call(cc, ada_w, ada_b).reshape(DEPTH * MOD_ROWS, 1, MOD_W)
    a_bar, bb, cm = _s5_prep_call(lam_re, lam_im, log_dt, bre, bim, cre, cim)
    na_bias = _na_bias_call(na_rpb)

    for i in range(DEPTH):
        last = i == DEPTH - 1
        att, loc = _proj_call(i, h, mods, gains, w_in_b)
        yf, yr = _s5_scan_call(i, att, a_bar, bb, cm)
        ya = _s5_glu_call(i, yf, yr, att, vec3(s5_d), glu_w_b, vec3(s5_glu_b))
        yb = _na_call(i, att, na_bias)
        yc, yd = _local_call(i, loc, conv_w, vec3(sgu_ln_g), vec3(sgu_ln_b), sgu_w_b, sgu_bias)
        tiling = dict(tile=TILE_LATENT, n_tiles=SEQ // TILE_LATENT, with_ctx=False) if last else dict(
            tile=TILE, n_tiles=N_TILES, with_ctx=True)
        h = _merge_call(i, h, mods, gains, ya, yb, yc, yd, w_in_b, w_branch_b, w_out_b, **tiling)
        h = _ffn_call(i, h, mods, gains, w_ff1_b, w_ff2_b, final_g[None] if last else None, **tiling)
    return h
```

```python
import functools
import math

import jax
import jax.numpy as jnp
from jax import lax
from jax.experimental import pallas as pl
from jax.experimental.pallas import tpu as pltpu

F32 = jnp.float32
BF16 = jnp.bfloat16

D_MODEL = 1024
BATCH = 8
SEQ = 2048
DEPTH = 4
GRID_W = 64
GRID_H = SEQ // GRID_W
CTX_LEN = 256
TOK = SEQ + CTX_LEN
N_BRANCH = 4
BRANCH_W = D_MODEL // 4
S5_GROUP = 16
S5_GROUPS = BRANCH_W // S5_GROUP
S5_STATE = 64
S5_LANES = S5_GROUPS * S5_STATE
NA_HEAD_DIM = 64
NA_HEADS = BRANCH_W // NA_HEAD_DIM
NA_WIN_ROWS = 8
NA_WIN_COLS = 16
NA_SCALE = NA_HEAD_DIM ** -0.5
CONV_W = 3
CHUNK = 128
SGU_GROUPS = 4
SGU_GROUP_W = BRANCH_W // SGU_GROUPS
D_FF = 4 * D_MODEL
N_MOD = 6
MOD_W = N_MOD * D_MODEL
EPS = 1e-6
NEG_INF = -1e30
N_SLOTS = 9
SLOT_W = N_SLOTS * BRANCH_W
ATT_W = 4 * BRANCH_W
LOC_W = 5 * BRANCH_W
GATE_W = N_BRANCH * D_MODEL

LANES = 128
MOD_ROWS = 16
MOD_CTX_ROW = BATCH
TILE = 1152
N_TILES = TOK // TILE
TILE_LATENT = 1024
SUB = 256
S5_T = 128
S5_ROWS = S5_T * BATCH
S5_STEPS = TOK // S5_T
S5_LAT_STEPS = SEQ // S5_T
S5_SLABS = BRANCH_W // LANES
NA_QB = 4 * GRID_W
NA_WIN_KROWS = 12
NA_WIN_KEYS = NA_WIN_KROWS * GRID_W
NA_LAT_STEPS = SEQ // NA_QB
NA_STEPS = TOK // NA_QB
MIB = 1024 * 1024


def _params(vmem_mib, n_axes):
    return pltpu.CompilerParams(
        dimension_semantics=("arbitrary",) * n_axes,
        vmem_limit_bytes=vmem_mib * MIB,
    )


def _resident(shape, index_map):
    return pl.BlockSpec(shape, index_map, pipeline_mode=pl.Buffered(1))


def _dot(a, b):
    return jnp.dot(a, b, preferred_element_type=F32)


def _dot_nt(a, b):
    return lax.dot_general(a, b, (((1,), (1,)), ((), ())), preferred_element_type=F32)


def _norm_modulate(h, gain, shift):
    return h * lax.rsqrt(jnp.mean(h * h, axis=-1, keepdims=True) + EPS) * gain + shift


def _sub_tiles(tile):
    ctx_row0 = SEQ % tile or tile
    bounds = list(range(0, ctx_row0, SUB)) + list(range(ctx_row0, tile, SUB)) + [tile]
    return [(lo, hi - lo) for lo, hi in zip(bounds[:-1], bounds[1:])]


def _sub_mods(j, row0, tile, ml_ref, mc_ref, ks):
    vec = lambda ref, k: ref[0, :, k * D_MODEL:(k + 1) * D_MODEL]
    ctx_row0 = SEQ % tile
    if mc_ref is None or row0 < ctx_row0:
        return [vec(ml_ref, k) for k in ks]
    in_ctx = j == TOK // tile - 1
    return [jnp.where(in_ctx, vec(mc_ref, k), vec(ml_ref, k)) for k in ks]


def _mod_specs(layer, with_ctx=True):
    specs = [pl.BlockSpec((1, 1, MOD_W), lambda b, j: (layer * MOD_ROWS + b, 0, 0))]
    if with_ctx:
        specs.append(pl.BlockSpec((1, 1, MOD_W), lambda b, j: (layer * MOD_ROWS + MOD_CTX_ROW, 0, 0)))
    return specs


def _gain_spec(layer, which):
    return pl.BlockSpec((1, 1, D_MODEL), lambda b, j: (2 * layer + which, 0, 0))


MOD_NT = 1536


def _mod_kernel(cc_ref, w_ref, b_ref, o_ref):
    s = jax.nn.silu(cc_ref[...])
    w = w_ref[0]
    s_hi = s.astype(BF16)
    s_lo = (s - s_hi.astype(F32)).astype(BF16)
    w_hi = w.astype(BF16)
    w_lo = (w - w_hi.astype(F32)).astype(BF16)
    o_ref[0] = _dot(s_hi, w_hi) + _dot(s_lo, w_hi) + _dot(s_hi, w_lo) + b_ref[0]


def _mod_call(cc, ada_w, ada_b):
    return pl.pallas_call(
        _mod_kernel,
        grid=(DEPTH, MOD_W // MOD_NT),
        in_specs=[
            pl.BlockSpec((MOD_ROWS, D_MODEL), lambda i, j: (0, 0)),
            pl.BlockSpec((1, D_MODEL, MOD_NT), lambda i, j: (i, 0, j)),
            pl.BlockSpec((1, 1, MOD_NT), lambda i, j: (i, 0, j)),
        ],
        out_specs=pl.BlockSpec((1, MOD_ROWS, MOD_NT), lambda i, j: (i, 0, j)),
        out_shape=jax.ShapeDtypeStruct((DEPTH, MOD_ROWS, MOD_W), F32),
        compiler_params=_params(40, 2),
        name="mod",
    )(cc, ada_w, ada_b.reshape(DEPTH, 1, MOD_W))


def _proj_kernel(h_ref, ml_ref, mc_ref, g_ref, w_ref, att_ref, loc_ref):
    j = pl.program_id(1)
    for r0, n in _sub_tiles(TILE):
        rs = slice(r0, r0 + n)
        shift, scale = _sub_mods(j, r0, TILE, ml_ref, mc_ref, (0, 1))
        a = _norm_modulate(h_ref[0, rs, :], g_ref[0] * (1.0 + scale), shift).astype(BF16)
        att_ref[0, rs, :] = _dot(a, w_ref[0, :, 0:ATT_W]).astype(BF16)
        p = _dot(a, w_ref[0, :, ATT_W:SLOT_W])
        loc_ref[0, rs, 0:BRANCH_W] = p[:, 0:BRANCH_W]
        loc_ref[0, rs, BRANCH_W:2 * BRANCH_W] = p[:, BRANCH_W:2 * BRANCH_W] * p[:, 2 * BRANCH_W:3 * BRANCH_W]
        loc_ref[0, rs, 2 * BRANCH_W:4 * BRANCH_W] = p[:, 3 * BRANCH_W:5 * BRANCH_W]


def _proj_call(layer, h, mods, gains, w_in):
    return pl.pallas_call(
        _proj_kernel,
        grid=(BATCH, N_TILES),
        in_specs=[
            pl.BlockSpec((1, TILE, D_MODEL), lambda b, j: (b, j, 0)),
            *_mod_specs(layer),
            _gain_spec(layer, 0),
            _resident((1, D_MODEL, SLOT_W), lambda b, j: (layer, 0, 0)),
        ],
        out_specs=[
            pl.BlockSpec((1, TILE, ATT_W), lambda b, j: (b, j, 0)),
            pl.BlockSpec((1, TILE, 4 * BRANCH_W), lambda b, j: (b, j, 0)),
        ],
        out_shape=[
            jax.ShapeDtypeStruct((BATCH, TOK, ATT_W), BF16),
            jax.ShapeDtypeStruct((BATCH, TOK, 4 * BRANCH_W), F32),
        ],
        compiler_params=_params(48, 2),
        name="proj",
    )(h, mods, mods, gains, w_in)


def _s5_prep_kernel(lre_ref, lim_ref, ldt_ref, bre_ref, bim_ref, cre_ref, cim_ref, a_ref, bb_ref, cm_ref):
    lam_re = lre_ref[0, 0]
    lam_im = lim_ref[0, 0]
    dt = jnp.exp(ldt_ref[0, 0])
    zr, zi = lam_re * dt, lam_im * dt
    mag = jnp.exp(zr)
    ar, ai = mag * jnp.cos(zi), mag * jnp.sin(zi)
    nr, ni = ar - 1.0, ai
    den = lam_re * lam_re + lam_im * lam_im
    cr = (nr * lam_re + ni * lam_im) / den
    ci = (ni * lam_re - nr * lam_im) / den
    b_row_g = lax.shift_right_logical(lax.broadcasted_iota(jnp.int32, (BRANCH_W, 1), 0), int(math.log2(S5_GROUP)))
    b_col_g = lax.shift_right_logical(lax.broadcasted_iota(jnp.int32, (1, S5_LANES), 1), int(math.log2(S5_STATE)))
    b_own = b_row_g == b_col_g
    expand_b = lambda ref: jnp.where(b_own, jnp.concatenate([ref[0, 0]] * (S5_LANES // LANES), axis=1), 0.0)
    bre, bim = expand_b(bre_ref), expand_b(bim_ref)
    bb_ref[0, 0, :, 0:S5_LANES] = (cr * bre - ci * bim).astype(BF16)
    bb_ref[0, 0, :, S5_LANES:2 * S5_LANES] = (cr * bim + ci * bre).astype(BF16)
    a_ref[0, 0, :, 0:S5_LANES] = jnp.broadcast_to(ar, (BATCH, S5_LANES))
    a_ref[0, 0, :, S5_LANES:2 * S5_LANES] = jnp.broadcast_to(ai, (BATCH, S5_LANES))
    c_row_g = lax.shift_right_logical(lax.broadcasted_iota(jnp.int32, (S5_LANES, 1), 0), int(math.log2(S5_STATE)))
    c_col_g = lax.shift_right_logical(lax.broadcasted_iota(jnp.int32, (1, BRANCH_W), 1), int(math.log2(S5_GROUP)))
    c_own = c_row_g == c_col_g
    expand_c = lambda ref: jnp.where(c_own, jnp.concatenate([ref[0, 0]] * (BRANCH_W // LANES), axis=1), 0.0)
    cm_ref[0, 0, 0:S5_LANES, :] = expand_c(cre_ref).astype(BF16)
    cm_ref[0, 0, S5_LANES:2 * S5_LANES, :] = (-expand_c(cim_ref)).astype(BF16)


def _s5_prep_call(lam_re, lam_im, log_dt, bre, bim, cre, cim):
    vec = pl.BlockSpec((1, 1, 1, S5_LANES), lambda i, d: (i, d, 0, 0))
    bmat = pl.BlockSpec((1, 1, BRANCH_W, LANES), lambda i, d: (i, d, 0, 0))
    cmat = pl.BlockSpec((1, 1, S5_LANES, LANES), lambda i, d: (i, d, 0, 0))
    return pl.pallas_call(
        _s5_prep_kernel,
        grid=(DEPTH, 2),
        in_specs=[vec, vec, vec, bmat, bmat, cmat, cmat],
        out_specs=[
            pl.BlockSpec((1, 1, BATCH, 2 * S5_LANES), lambda i, d: (i, d, 0, 0)),
            pl.BlockSpec((1, 1, BRANCH_W, 2 * S5_LANES), lambda i, d: (i, d, 0, 0)),
            pl.BlockSpec((1, 1, 2 * S5_LANES, BRANCH_W), lambda i, d: (i, d, 0, 0)),
        ],
        out_shape=[
            jax.ShapeDtypeStruct((DEPTH, 2, BATCH, 2 * S5_LANES), F32),
            jax.ShapeDtypeStruct((DEPTH, 2, BRANCH_W, 2 * S5_LANES), BF16),
            jax.ShapeDtypeStruct((DEPTH, 2, 2 * S5_LANES, BRANCH_W), BF16),
        ],
        compiler_params=_params(32, 2),
        name="s5_prep",
    )(lam_re, lam_im, log_dt, bre, bim, cre, cim)


def _s5_fwd_block(i):
    return jnp.where(i < S5_STEPS - S5_LAT_STEPS, S5_LAT_STEPS + i, i - (S5_STEPS - S5_LAT_STEPS))


def _s5_rev_block(i):
    return S5_STEPS - 1 - i


def _to_time_major(u_ref, slab_ref):
    for b in range(BATCH):
        ub = u_ref[b].astype(F32)
        for s in range(S5_SLABS):
            slab_ref[s, pl.ds(b, S5_T, stride=BATCH), :] = ub[:, s * LANES:(s + 1) * LANES]


def _s5_scan_kernel(uf_ref, ur_ref, a_ref, bb_ref, cm_ref, yf_ref, yr_ref, xf_ref, xr_ref, st_ref, usf_ref, usr_ref):
    i = pl.program_id(0)

    @pl.when(i == 0)
    def _():
        st_ref[...] = jnp.zeros_like(st_ref)

    mm_rows = 256
    dirs = ((uf_ref, yf_ref, xf_ref, usf_ref), (ur_ref, yr_ref, xr_ref, usr_ref))
    for d, (u_ref, y_ref, x_ref, us_ref) in enumerate(dirs):
        _to_time_major(u_ref, us_ref)
        for c in range(S5_ROWS // mm_rows):
            rs = slice(c * mm_rows, (c + 1) * mm_rows)
            u = jnp.concatenate([us_ref[s, rs, :] for s in range(S5_SLABS)], axis=1).astype(BF16)
            x_ref[rs, :] = _dot(u, bb_ref[0, d])
    for d, (u_ref, y_ref, x_ref, us_ref) in enumerate(dirs):
        ar = a_ref[0, d, :, 0:S5_LANES]
        ai = a_ref[0, d, :, S5_LANES:2 * S5_LANES]
        hr, hi = st_ref[d, :, 0:S5_LANES], st_ref[d, :, S5_LANES:2 * S5_LANES]
        for t in range(S5_T):
            r = (t if d == 0 else S5_T - 1 - t) * BATCH
            xr = x_ref[r:r + BATCH, 0:S5_LANES]
            xi = x_ref[r:r + BATCH, S5_LANES:2 * S5_LANES]
            hr, hi = ar * hr - ai * hi + xr, ar * hi + ai * hr + xi
            x_ref[r:r + BATCH, 0:S5_LANES] = hr
            x_ref[r:r + BATCH, S5_LANES:2 * S5_LANES] = hi
        st_ref[d, :, 0:S5_LANES] = hr
        st_ref[d, :, S5_LANES:2 * S5_LANES] = hi
        for c in range(S5_ROWS // mm_rows):
            rs = slice(c * mm_rows, (c + 1) * mm_rows)
            y_ref[rs, :] = _dot(x_ref[rs, :].astype(BF16), cm_ref[0, d])


def _s5_scan_call(layer, att, a_bar, bb, cm):
    u_blk = lambda f: pl.BlockSpec((BATCH, S5_T, BRANCH_W), lambda i: (0, f(i), 0))
    y_blk = lambda f: pl.BlockSpec((S5_ROWS, BRANCH_W), lambda i: (f(i), 0))
    return pl.pallas_call(
        _s5_scan_kernel,
        grid=(S5_STEPS,),
        in_specs=[
            u_blk(_s5_fwd_block),
            u_blk(_s5_rev_block),
            pl.BlockSpec((1, 2, BATCH, 2 * S5_LANES), lambda i: (layer, 0, 0, 0)),
            pl.BlockSpec((1, 2, BRANCH_W, 2 * S5_LANES), lambda i: (layer, 0, 0, 0)),
            pl.BlockSpec((1, 2, 2 * S5_LANES, BRANCH_W), lambda i: (layer, 0, 0, 0)),
        ],
        out_specs=[y_blk(_s5_fwd_block), y_blk(_s5_rev_block)],
        out_shape=[jax.ShapeDtypeStruct((TOK * BATCH, BRANCH_W), F32)] * 2,
        scratch_shapes=[
            pltpu.VMEM((S5_ROWS, 2 * S5_LANES), F32),
            pltpu.VMEM((S5_ROWS, 2 * S5_LANES), F32),
            pltpu.VMEM((2, BATCH, 2 * S5_LANES), F32),
            pltpu.VMEM((S5_SLABS, S5_ROWS, LANES), F32),
            pltpu.VMEM((S5_SLABS, S5_ROWS, LANES), F32),
        ],
        compiler_params=_params(48, 1),
        name="s5_scan",
    )(att, att, a_bar, bb, cm)


def _s5_glu_kernel(yf_ref, yr_ref, u_ref, d_ref, w_ref, b_ref, o_ref, ys_ref):
    y = yf_ref[...] + yr_ref[...]
    for s in range(S5_SLABS):
        ys_ref[s] = y[:, s * LANES:(s + 1) * LANES]
    for b in range(BATCH):
        yb = jnp.concatenate([ys_ref[s, pl.ds(b, S5_T, stride=BATCH), :] for s in range(S5_SLABS)], axis=1)
        g = jax.nn.gelu(yb + u_ref[b].astype(F32) * d_ref[0])
        o_ref[b] = (g * jax.nn.sigmoid(_dot(g.astype(BF16), w_ref[0]) + b_ref[0])).astype(BF16)


def _s5_glu_call(layer, yf, yr, att, d_skip, w, b):
    rows = pl.BlockSpec((S5_ROWS, BRANCH_W), lambda i: (i, 0))
    vec = pl.BlockSpec((1, 1, BRANCH_W), lambda i: (layer, 0, 0))
    tok = pl.BlockSpec((BATCH, S5_T, BRANCH_W), lambda i: (0, i, 0))
    return pl.pallas_call(
        _s5_glu_kernel,
        grid=(S5_STEPS,),
        in_specs=[rows, rows, tok, vec, pl.BlockSpec((1, BRANCH_W, BRANCH_W), lambda i: (layer, 0, 0)), vec],
        out_specs=tok,
        out_shape=jax.ShapeDtypeStruct((BATCH, TOK, BRANCH_W), BF16),
        scratch_shapes=[pltpu.VMEM((S5_SLABS, S5_ROWS, LANES), F32)],
        compiler_params=_params(32, 1),
        name="s5_glu",
    )(yf, yr, att, d_skip, w, b)


NA_CASE_FIRST, NA_CASE_INNER, NA_CASE_LAST = 0, 1, 2
NA_CASES = 3
N_RPB = (2 * NA_WIN_ROWS - 1) * (2 * NA_WIN_COLS - 1)


def _na_window_start(row0):
    return min(max(row0 - NA_WIN_ROWS // 2, 0), GRID_H - NA_WIN_KROWS)


def _na_bias_kernel(rpb_ref, o_ref):
    layer, h = pl.program_id(0), pl.program_id(1)
    base = (layer * NA_HEADS + h) * N_RPB
    qq = lax.broadcasted_iota(jnp.int32, (GRID_W, 2 * GRID_W), 0)
    lane = lax.broadcasted_iota(jnp.int32, (GRID_W, 2 * GRID_W), 1)
    kk = jnp.bitwise_and(lane, GRID_W - 1)
    second = lane >= GRID_W
    dcm = jnp.clip(kk - qq + NA_WIN_COLS - 1, 0, 2 * NA_WIN_COLS - 2)
    cs = jnp.clip(qq - NA_WIN_COLS // 2, 0, GRID_W - NA_WIN_COLS)
    col_ok = (kk >= cs) & (kk < cs + NA_WIN_COLS)
    neg = jnp.full((GRID_W, 2 * GRID_W), NEG_INF, F32)
    n_dr, n_dc = 2 * NA_WIN_ROWS - 1, 2 * NA_WIN_COLS - 1
    table = []
    for dr in range(n_dr):
        t = neg
        for dc in range(n_dc):
            t = jnp.where(dcm == dc, rpb_ref[base + dr * n_dc + dc], t)
        table.append(jnp.where(col_ok, t, NEG_INF))
    for case, row0 in ((NA_CASE_FIRST, 0), (NA_CASE_INNER, 4), (NA_CASE_LAST, GRID_H - 4)):
        start = _na_window_start(row0)
        for i in range(4):
            r = row0 + i
            row_start = min(max(r - NA_WIN_ROWS // 2, 0), GRID_H - NA_WIN_ROWS)
            for m in range(NA_WIN_KROWS // 2):
                halves = []
                for jj in (2 * m, 2 * m + 1):
                    kr = start + jj
                    valid = row_start <= kr < row_start + NA_WIN_ROWS
                    halves.append(table[kr - r + NA_WIN_ROWS - 1] if valid else neg)
                o_ref[0, case, 0, i * GRID_W:(i + 1) * GRID_W, m * 2 * GRID_W:(m + 1) * 2 * GRID_W] = (
                    jnp.where(second, halves[1], halves[0]))


def _na_bias_call(rpb):
    return pl.pallas_call(
        _na_bias_kernel,
        grid=(DEPTH, NA_HEADS),
        in_specs=[pl.BlockSpec(memory_space=pltpu.SMEM)],
        out_specs=pl.BlockSpec((1, NA_CASES, 1, NA_QB, NA_WIN_KEYS), lambda i, h: (i, 0, h, 0, 0)),
        out_shape=jax.ShapeDtypeStruct((DEPTH, NA_CASES, NA_HEADS, NA_QB, NA_WIN_KEYS), F32),
        compiler_params=_params(32, 2),
        name="na_bias",
    )(rpb.reshape(-1))


def _na_case(j):
    return jnp.where(j == 0, NA_CASE_FIRST, jnp.where(j >= NA_LAT_STEPS - 1, NA_CASE_LAST, NA_CASE_INNER))


def _na_attend(q, blocks, o_ref):
    head = lax.shift_right_logical(lax.broadcasted_iota(jnp.int32, (1, BRANCH_W), 1), int(math.log2(NA_HEAD_DIM)))
    acc = jnp.zeros((NA_QB, BRANCH_W), F32)
    for h in range(NA_HEADS):
        in_head = head == h
        qh = jnp.where(in_head, q, jnp.zeros_like(q))
        scores = [_dot_nt(qh, k) if bias is None else _dot_nt(qh, k) + bias[h] for k, _, bias in blocks]
        m = functools.reduce(jnp.maximum, [jnp.max(s, axis=-1, keepdims=True) for s in scores])
        probs = [jnp.exp(s - m) for s in scores]
        norm = sum(jnp.sum(p, axis=-1, keepdims=True) for p in probs)
        o = sum(_dot(p.astype(BF16), v) for p, (_, v, _) in zip(probs, blocks))
        acc = jnp.where(in_head, o * (1.0 / norm), acc)
    o_ref[0] = acc.astype(BF16)


def _na_kernel(q_ref, k_ref, v_ref, bias_ref, o_ref):
    j = pl.program_id(1)
    q = q_ref[0] * NA_SCALE
    ctx_block = (k_ref[0, SEQ:TOK, :], v_ref[0, SEQ:TOK, :], None)

    @pl.when(j < NA_LAT_STEPS)
    def _():
        start = jnp.clip(j * 4 - NA_WIN_ROWS // 2, 0, GRID_H - NA_WIN_KROWS)
        k0 = pl.multiple_of(start * GRID_W, GRID_W)
        window = (k_ref[0, pl.ds(k0, NA_WIN_KEYS), :], v_ref[0, pl.ds(k0, NA_WIN_KEYS), :], bias_ref[0, 0])
        _na_attend(q, [window, ctx_block], o_ref)

    @pl.when(j == NA_LAT_STEPS)
    def _():
        _na_attend(q, [ctx_block], o_ref)


def _na_call(layer, att, bias, n_steps):
    return pl.pallas_call(
        _na_kernel,
        grid=(BATCH, n_steps),
        in_specs=[
            pl.BlockSpec((1, NA_QB, BRANCH_W), lambda b, j: (b, j, 1)),
            pl.BlockSpec((1, TOK, BRANCH_W), lambda b, j: (b, 0, 2)),
            pl.BlockSpec((1, TOK, BRANCH_W), lambda b, j: (b, 0, 3)),
            pl.BlockSpec((1, 1, NA_HEADS, NA_QB, NA_WIN_KEYS), lambda b, j: (layer, _na_case(j), 0, 0, 0)),
        ],
        out_specs=pl.BlockSpec((1, NA_QB, BRANCH_W), lambda b, j: (b, j, 0)),
        out_shape=jax.ShapeDtypeStruct((BATCH, n_steps * NA_QB, BRANCH_W), BF16),
        compiler_params=_params(40, 2),
        name="na",
    )(att, att, att, bias)


HALO = 8


def _local_kernel(cb_ref, z_ref, zp_ref, zn_ref, su_ref, sv_ref, cw_ref, lg_ref, lb_ref, sw_ref, sb_ref,
                  yc_ref, yd_ref):
    j = pl.program_id(1)
    local = lax.broadcasted_iota(jnp.int32, (TILE, 1), 0)
    rows = j * TILE + local
    z = z_ref[0]
    prev = jnp.where(local == 0, zp_ref[0, HALO - 1:HALO, :], pltpu.roll(z, 1, 0))
    nxt = jnp.where(local == TILE - 1, zn_ref[0, 0:1, :], pltpu.roll(z, TILE - 1, 0))
    prev = jnp.where((rows == 0) | (rows == SEQ), 0.0, prev)
    nxt = jnp.where((rows == SEQ - 1) | (rows == TOK - 1), 0.0, nxt)
    y = cw_ref[0, 0:1, :] * prev + cw_ref[0, 1:2, :] * z + cw_ref[0, 2:3, :] * nxt
    yc_ref[0] = (cb_ref[0] * y).astype(BF16)

    v = sv_ref[0]
    mu = jnp.mean(v, axis=-1, keepdims=True)
    var = jnp.mean(jnp.square(v - mu), axis=-1, keepdims=True)
    vn = ((v - mu) * lax.rsqrt(var + EPS) * lg_ref[0] + lb_ref[0]).astype(BF16)
    group = lax.shift_right_logical(lax.broadcasted_iota(jnp.int32, (1, BRANCH_W), 1), int(math.log2(SGU_GROUP_W)))
    zero = jnp.zeros((CHUNK, BRANCH_W), BF16)
    for c in range(TILE // CHUNK):
        rs = slice(c * CHUNK, (c + 1) * CHUNK)
        vc = vn[rs, :]
        stacked = jnp.concatenate([jnp.where(group == g, vc, zero) for g in range(SGU_GROUPS)], axis=0)
        zc = _dot(sw_ref[0], stacked) + sb_ref[0]
        yd_ref[0, rs, :] = (su_ref[0, rs, :] * zc).astype(BF16)


def _local_call(layer, loc, conv_w, ln_g, ln_b, sgu_w, sgu_b):
    col = lambda k: pl.BlockSpec((1, TILE, BRANCH_W), lambda b, j: (b, j, k))
    halo_per_tile = TILE // HALO
    n_halo = TOK // HALO
    vec = pl.BlockSpec((1, 1, BRANCH_W), lambda b, j: (layer, 0, 0))
    return pl.pallas_call(
        _local_kernel,
        grid=(BATCH, N_TILES),
        in_specs=[
            col(0),
            col(1),
            pl.BlockSpec((1, HALO, BRANCH_W), lambda b, j: (b, jnp.maximum(j * halo_per_tile - 1, 0), 1)),
            pl.BlockSpec((1, HALO, BRANCH_W), lambda b, j: (b, jnp.minimum((j + 1) * halo_per_tile, n_halo - 1), 1)),
            col(2),
            col(3),
            pl.BlockSpec((1, CONV_W, BRANCH_W), lambda b, j: (layer, 0, 0)),
            vec,
            vec,
            pl.BlockSpec((1, CHUNK, SGU_GROUPS * CHUNK), lambda b, j: (layer, 0, 0)),
            pl.BlockSpec((1, CHUNK, BRANCH_W), lambda b, j: (layer, 0, 0)),
        ],
        out_specs=[pl.BlockSpec((1, TILE, BRANCH_W), lambda b, j: (b, j, 0))] * 2,
        out_shape=[jax.ShapeDtypeStruct((BATCH, TOK, BRANCH_W), BF16)] * 2,
        compiler_params=_params(32, 2),
        name="local",
    )(loc, loc, loc, loc, loc, loc, conv_w, ln_g, ln_b, sgu_w, sgu_b)


def _merge_kernel(*refs, tile, with_ctx):
    h_ref, ml_ref = refs[0], refs[1]
    mc_ref = refs[2] if with_ctx else None
    g_ref, ya_ref, yb_ref, yc_ref, yd_ref, wg_ref, wb_ref, wo_ref, o_ref = refs[2 + with_ctx:]
    j = pl.program_id(1)
    y_refs = (ya_ref, yb_ref, yc_ref, yd_ref)
    for r0, n in _sub_tiles(tile):
        rs = slice(r0, r0 + n)
        shift, scale, gate = _sub_mods(j, r0, tile, ml_ref, mc_ref, (0, 1, 2))
        h = h_ref[0, rs, :]
        a = _norm_modulate(h, g_ref[0] * (1.0 + scale), shift).astype(BF16)
        acc = jnp.zeros((n, D_MODEL), F32)
        for k in range(N_BRANCH):
            logits = _dot(a, wg_ref[0, :, SLOT_W + k * D_MODEL:SLOT_W + (k + 1) * D_MODEL])
            acc = acc + jax.nn.sigmoid(logits) * _dot(y_refs[k][0, rs, :], wb_ref[0, k])
        o_ref[0, rs, :] = h + gate * _dot(acc.astype(BF16), wo_ref[0])


def _merge_call(layer, h, mods, gains, ya, yb, yc, yd, w_in, w_branch, w_out, *, tile, n_tiles, with_ctx):
    tok = pl.BlockSpec((1, tile, D_MODEL), lambda b, j: (b, j, 0))
    br = pl.BlockSpec((1, tile, BRANCH_W), lambda b, j: (b, j, 0))
    mod_specs = _mod_specs(layer, with_ctx)
    return pl.pallas_call(
        functools.partial(_merge_kernel, tile=tile, with_ctx=with_ctx),
        grid=(BATCH, n_tiles),
        in_specs=[
            tok,
            *mod_specs,
            _gain_spec(layer, 0),
            br, br, br, br,
            _resident((1, D_MODEL, SLOT_W + GATE_W), lambda b, j: (layer, 0, 0)),
            _resident((1, N_BRANCH, BRANCH_W, D_MODEL), lambda b, j: (layer, 0, 0, 0)),
            _resident((1, D_MODEL, D_MODEL), lambda b, j: (layer, 0, 0)),
        ],
        out_specs=tok,
        out_shape=jax.ShapeDtypeStruct((BATCH, tile * n_tiles, D_MODEL), F32),
        compiler_params=_params(56, 2),
        name="merge",
    )(h, *([mods] * len(mod_specs)), gains, ya, yb, yc, yd, w_in, w_branch, w_out)


FF_CHUNK = 1024


def _ffn_kernel(*refs, tile, with_ctx, final):
    h_ref, ml_ref = refs[0], refs[1]
    mc_ref = refs[2] if with_ctx else None
    g_ref, w1_ref, w2_ref = refs[2 + with_ctx:5 + with_ctx]
    fg_ref = refs[5 + with_ctx] if final else None
    o_ref = refs[-1]
    j = pl.program_id(1)
    for r0, n in _sub_tiles(tile):
        rs = slice(r0, r0 + n)
        shift, scale, gate = _sub_mods(j, r0, tile, ml_ref, mc_ref, (3, 4, 5))
        h = h_ref[0, rs, :]
        a = _norm_modulate(h, g_ref[0] * (1.0 + scale), shift).astype(BF16)
        acc = jnp.zeros((n, D_MODEL), F32)
        for c in range(D_FF // FF_CHUNK):
            cs = slice(c * FF_CHUNK, (c + 1) * FF_CHUNK)
            t = jnp.maximum(_dot(a, w1_ref[0, :, cs]), 0.0)
            acc = acc + _dot((t * t).astype(BF16), w2_ref[0, cs, :])
        out = h + gate * acc
        if final:
            out = out * lax.rsqrt(jnp.mean(out * out, axis=-1, keepdims=True) + EPS) * fg_ref[...]
        o_ref[0, rs, :] = out


def _ffn_call(layer, h, mods, gains, w1, w2, final_g=None, *, tile, n_tiles, with_ctx):
    tok = pl.BlockSpec((1, tile, D_MODEL), lambda b, j: (b, j, 0))
    final = final_g is not None
    mod_specs = _mod_specs(layer, with_ctx)
    in_specs = [
        tok,
        *mod_specs,
        _gain_spec(layer, 1),
        _resident((1, D_MODEL, D_FF), lambda b, j: (layer, 0, 0)),
        _resident((1, D_FF, D_MODEL), lambda b, j: (layer, 0, 0)),
    ]
    args = [h, *([mods] * len(mod_specs)), gains, w1, w2]
    if final:
        in_specs.append(pl.BlockSpec((1, D_MODEL), lambda b, j: (0, 0)))
        args.append(final_g)
    return pl.pallas_call(
        functools.partial(_ffn_kernel, tile=tile, with_ctx=with_ctx, final=final),
        grid=(BATCH, n_tiles),
        in_specs=in_specs,
        out_specs=tok,
        out_shape=jax.ShapeDtypeStruct((BATCH, tile * n_tiles, D_MODEL), F32),
        compiler_params=_params(58, 2),
        name="ffn_final" if final else "ffn",
    )(*args)


def kernel(x, c, ctx, c_ctx, ada_w, ada_b, norm_g, final_g, w_in, w_branch, w_out, s5_lam_re, s5_lam_im, s5_log_dt, s5_b_re, s5_b_im, s5_c_re, s5_c_im, s5_d, s5_glu_w, s5_glu_b, na_rpb, conv_w, sgu_ln_g, sgu_ln_b, sgu_w, sgu_b, w_ff1, w_ff2):
    h = jnp.concatenate([x, ctx], axis=1)
    cc = jnp.concatenate([c, c_ctx[None], jnp.zeros((MOD_ROWS - BATCH - 1, D_MODEL), F32)], axis=0)
    b_rows = lambda p: jnp.tile(jnp.swapaxes(p, 3, 4).reshape(DEPTH, 2, BRANCH_W, S5_STATE), (1, 1, 1, LANES // S5_STATE))
    c_rows = lambda p: jnp.tile(jnp.swapaxes(p, 3, 4).reshape(DEPTH, 2, S5_LANES, S5_GROUP), (1, 1, 1, LANES // S5_GROUP))
    bre, bim, cre, cim = b_rows(s5_b_re), b_rows(s5_b_im), c_rows(s5_c_re), c_rows(s5_c_im)
    lam_re = s5_lam_re.reshape(DEPTH, 2, 1, S5_LANES)
    lam_im = s5_lam_im.reshape(DEPTH, 2, 1, S5_LANES)
    log_dt = jnp.repeat(s5_log_dt, S5_STATE, axis=-1).reshape(DEPTH, 2, 1, S5_LANES)
    sgu_bias = jnp.repeat(jnp.swapaxes(sgu_b, 1, 2), SGU_GROUP_W, axis=2)
    w_in_b = w_in.astype(BF16)
    w_branch_b, w_out_b = w_branch.astype(BF16), w_out.astype(BF16)
    w_ff1_b, w_ff2_b = w_ff1.astype(BF16), w_ff2.astype(BF16)
    glu_w_b = s5_glu_w.astype(BF16)
    sgu_w_b = jnp.swapaxes(sgu_w, 1, 2).reshape(DEPTH, CHUNK, SGU_GROUPS * CHUNK).astype(BF16)
    gains = norm_g.reshape(DEPTH * 2, 1, D_MODEL)
    vec3 = lambda p: p.reshape(DEPTH, 1, BRANCH_W)

    mods = _mod_call(cc, ada_w, ada_b).reshape(DEPTH * MOD_ROWS, 1, MOD_W)
    a_bar, bb, cm = _s5_prep_call(lam_re, lam_im, log_dt, bre, bim, cre, cim)
    na_bias = _na_bias_call(na_rpb)

    for i in range(DEPTH):
        last = i == DEPTH - 1
        att, loc = _proj_call(i, h, mods, gains, w_in_b)
        yf, yr = _s5_scan_call(i, att, a_bar, bb, cm)
        ya = _s5_glu_call(i, yf, yr, att, vec3(s5_d), glu_w_b, vec3(s5_glu_b))
        yb = _na_call(i, att, na_bias, NA_LAT_STEPS if last else NA_STEPS)
        yc, yd = _local_call(i, loc, conv_w, vec3(sgu_ln_g), vec3(sgu_ln_b), sgu_w_b, sgu_bias)
        tiling = dict(tile=TILE_LATENT, n_tiles=SEQ // TILE_LATENT, with_ctx=False) if last else dict(
            tile=TILE, n_tiles=N_TILES, with_ctx=True)
        h = _merge_call(i, h, mods, gains, ya, yb, yc, yd, w_in_b, w_branch_b, w_out_b, **tiling)
        h = _ffn_call(i, h, mods, gains, w_ff1_b, w_ff2_b, final_g[None] if last else None, **tiling)
    return h
```

```python
import functools
import math

import jax
import jax.numpy as jnp
from jax import lax
from jax.experimental import pallas as pl
from jax.experimental.pallas import tpu as pltpu

F32 = jnp.float32
BF16 = jnp.bfloat16

D_MODEL = 1024
BATCH = 8
SEQ = 2048
DEPTH = 4
GRID_W = 64
GRID_H = SEQ // GRID_W
CTX_LEN = 256
TOK = SEQ + CTX_LEN
N_BRANCH = 4
BRANCH_W = D_MODEL // 4
S5_GROUP = 16
S5_GROUPS = BRANCH_W // S5_GROUP
S5_STATE = 64
S5_LANES = S5_GROUPS * S5_STATE
NA_HEAD_DIM = 64
NA_HEADS = BRANCH_W // NA_HEAD_DIM
NA_WIN_ROWS = 8
NA_WIN_COLS = 16
NA_SCALE = NA_HEAD_DIM ** -0.5
CONV_W = 3
CHUNK = 128
SGU_GROUPS = 4
SGU_GROUP_W = BRANCH_W // SGU_GROUPS
D_FF = 4 * D_MODEL
N_MOD = 6
MOD_W = N_MOD * D_MODEL
EPS = 1e-6
NEG_INF = -1e30
N_SLOTS = 9
SLOT_W = N_SLOTS * BRANCH_W
ATT_W = 4 * BRANCH_W
LOC_W = 5 * BRANCH_W
GATE_W = N_BRANCH * D_MODEL

LANES = 128
MOD_ROWS = 16
MOD_CTX_ROW = BATCH
TILE = 1152
N_TILES = TOK // TILE
TILE_LATENT = 1024
SUB = 256
S5_T = 128
S5_ROWS = S5_T * BATCH
S5_STEPS = TOK // S5_T
S5_LAT_STEPS = SEQ // S5_T
S5_SLABS = BRANCH_W // LANES
NA_QB = 4 * GRID_W
NA_WIN_KROWS = 12
NA_WIN_KEYS = NA_WIN_KROWS * GRID_W
NA_LAT_STEPS = SEQ // NA_QB
NA_GROUP = 4
MIB = 1024 * 1024


def _params(vmem_mib, n_axes):
    return pltpu.CompilerParams(
        dimension_semantics=("arbitrary",) * n_axes,
        vmem_limit_bytes=vmem_mib * MIB,
    )


def _resident(shape, index_map):
    return pl.BlockSpec(shape, index_map, pipeline_mode=pl.Buffered(1))


def _dot(a, b):
    return jnp.dot(a, b, preferred_element_type=F32)


def _dot_nt(a, b):
    return lax.dot_general(a, b, (((1,), (1,)), ((), ())), preferred_element_type=F32)


def _norm_modulate(h, gain, shift):
    return h * lax.rsqrt(jnp.mean(h * h, axis=-1, keepdims=True) + EPS) * gain + shift


def _sub_tiles(tile):
    ctx_row0 = SEQ % tile or tile
    bounds = list(range(0, ctx_row0, SUB)) + list(range(ctx_row0, tile, SUB)) + [tile]
    return [(lo, hi - lo) for lo, hi in zip(bounds[:-1], bounds[1:])]


def _sub_mods(j, row0, tile, ml_ref, mc_ref, ks):
    vec = lambda ref, k: ref[0, :, k * D_MODEL:(k + 1) * D_MODEL]
    ctx_row0 = SEQ % tile
    if mc_ref is None or row0 < ctx_row0:
        return [vec(ml_ref, k) for k in ks]
    in_ctx = j == TOK // tile - 1
    return [jnp.where(in_ctx, vec(mc_ref, k), vec(ml_ref, k)) for k in ks]


def _mod_specs(layer, with_ctx=True):
    specs = [pl.BlockSpec((1, 1, MOD_W), lambda b, j: (layer * MOD_ROWS + b, 0, 0))]
    if with_ctx:
        specs.append(pl.BlockSpec((1, 1, MOD_W), lambda b, j: (layer * MOD_ROWS + MOD_CTX_ROW, 0, 0)))
    return specs


def _gain_spec(layer, which):
    return pl.BlockSpec((1, 1, D_MODEL), lambda b, j: (2 * layer + which, 0, 0))


MOD_NT = 1536


def _mod_kernel(cc_ref, w_ref, b_ref, o_ref):
    s = jax.nn.silu(cc_ref[...])
    w = w_ref[0]
    s_hi = s.astype(BF16)
    s_lo = (s - s_hi.astype(F32)).astype(BF16)
    w_hi = w.astype(BF16)
    w_lo = (w - w_hi.astype(F32)).astype(BF16)
    o_ref[0] = _dot(s_hi, w_hi) + _dot(s_lo, w_hi) + _dot(s_hi, w_lo) + b_ref[0]


def _mod_call(cc, ada_w, ada_b):
    return pl.pallas_call(
        _mod_kernel,
        grid=(DEPTH, MOD_W // MOD_NT),
        in_specs=[
            pl.BlockSpec((MOD_ROWS, D_MODEL), lambda i, j: (0, 0)),
            pl.BlockSpec((1, D_MODEL, MOD_NT), lambda i, j: (i, 0, j)),
            pl.BlockSpec((1, 1, MOD_NT), lambda i, j: (i, 0, j)),
        ],
        out_specs=pl.BlockSpec((1, MOD_ROWS, MOD_NT), lambda i, j: (i, 0, j)),
        out_shape=jax.ShapeDtypeStruct((DEPTH, MOD_ROWS, MOD_W), F32),
        compiler_params=_params(40, 2),
        name="mod",
    )(cc, ada_w, ada_b.reshape(DEPTH, 1, MOD_W))


def _proj_kernel(h_ref, ml_ref, mc_ref, g_ref, w_ref, att_ref, loc_ref):
    j = pl.program_id(1)
    for r0, n in _sub_tiles(TILE):
        rs = slice(r0, r0 + n)
        shift, scale = _sub_mods(j, r0, TILE, ml_ref, mc_ref, (0, 1))
        a = _norm_modulate(h_ref[0, rs, :], g_ref[0] * (1.0 + scale), shift).astype(BF16)
        att_ref[0, rs, :] = _dot(a, w_ref[0, :, 0:ATT_W]).astype(BF16)
        p = _dot(a, w_ref[0, :, ATT_W:SLOT_W])
        loc_ref[0, rs, 0:BRANCH_W] = p[:, 0:BRANCH_W]
        loc_ref[0, rs, BRANCH_W:2 * BRANCH_W] = p[:, BRANCH_W:2 * BRANCH_W] * p[:, 2 * BRANCH_W:3 * BRANCH_W]
        loc_ref[0, rs, 2 * BRANCH_W:4 * BRANCH_W] = p[:, 3 * BRANCH_W:5 * BRANCH_W]


def _proj_call(layer, h, mods, gains, w_in):
    return pl.pallas_call(
        _proj_kernel,
        grid=(BATCH, N_TILES),
        in_specs=[
            pl.BlockSpec((1, TILE, D_MODEL), lambda b, j: (b, j, 0)),
            *_mod_specs(layer),
            _gain_spec(layer, 0),
            _resident((1, D_MODEL, SLOT_W), lambda b, j: (layer, 0, 0)),
        ],
        out_specs=[
            pl.BlockSpec((1, TILE, ATT_W), lambda b, j: (b, j, 0)),
            pl.BlockSpec((1, TILE, 4 * BRANCH_W), lambda b, j: (b, j, 0)),
        ],
        out_shape=[
            jax.ShapeDtypeStruct((BATCH, TOK, ATT_W), BF16),
            jax.ShapeDtypeStruct((BATCH, TOK, 4 * BRANCH_W), F32),
        ],
        compiler_params=_params(48, 2),
        name="proj",
    )(h, mods, mods, gains, w_in)


def _s5_prep_kernel(lre_ref, lim_ref, ldt_ref, bre_ref, bim_ref, cre_ref, cim_ref, a_ref, bb_ref, cm_ref):
    lam_re = lre_ref[0, 0]
    lam_im = lim_ref[0, 0]
    dt = jnp.exp(ldt_ref[0, 0])
    zr, zi = lam_re * dt, lam_im * dt
    mag = jnp.exp(zr)
    ar, ai = mag * jnp.cos(zi), mag * jnp.sin(zi)
    nr, ni = ar - 1.0, ai
    den = lam_re * lam_re + lam_im * lam_im
    cr = (nr * lam_re + ni * lam_im) / den
    ci = (ni * lam_re - nr * lam_im) / den
    b_row_g = lax.shift_right_logical(lax.broadcasted_iota(jnp.int32, (BRANCH_W, 1), 0), int(math.log2(S5_GROUP)))
    b_col_g = lax.shift_right_logical(lax.broadcasted_iota(jnp.int32, (1, S5_LANES), 1), int(math.log2(S5_STATE)))
    b_own = b_row_g == b_col_g
    expand_b = lambda ref: jnp.where(b_own, jnp.concatenate([ref[0, 0]] * (S5_LANES // LANES), axis=1), 0.0)
    bre, bim = expand_b(bre_ref), expand_b(bim_ref)
    bb_ref[0, 0, :, 0:S5_LANES] = (cr * bre - ci * bim).astype(BF16)
    bb_ref[0, 0, :, S5_LANES:2 * S5_LANES] = (cr * bim + ci * bre).astype(BF16)
    a_ref[0, 0, :, 0:S5_LANES] = jnp.broadcast_to(ar, (BATCH, S5_LANES))
    a_ref[0, 0, :, S5_LANES:2 * S5_LANES] = jnp.broadcast_to(ai, (BATCH, S5_LANES))
    c_row_g = lax.shift_right_logical(lax.broadcasted_iota(jnp.int32, (S5_LANES, 1), 0), int(math.log2(S5_STATE)))
    c_col_g = lax.shift_right_logical(lax.broadcasted_iota(jnp.int32, (1, BRANCH_W), 1), int(math.log2(S5_GROUP)))
    c_own = c_row_g == c_col_g
    expand_c = lambda ref: jnp.where(c_own, jnp.concatenate([ref[0, 0]] * (BRANCH_W // LANES), axis=1), 0.0)
    cm_ref[0, 0, 0:S5_LANES, :] = expand_c(cre_ref).astype(BF16)
    cm_ref[0, 0, S5_LANES:2 * S5_LANES, :] = (-expand_c(cim_ref)).astype(BF16)


def _s5_prep_call(lam_re, lam_im, log_dt, bre, bim, cre, cim):
    vec = pl.BlockSpec((1, 1, 1, S5_LANES), lambda i, d: (i, d, 0, 0))
    bmat = pl.BlockSpec((1, 1, BRANCH_W, LANES), lambda i, d: (i, d, 0, 0))
    cmat = pl.BlockSpec((1, 1, S5_LANES, LANES), lambda i, d: (i, d, 0, 0))
    return pl.pallas_call(
        _s5_prep_kernel,
        grid=(DEPTH, 2),
        in_specs=[vec, vec, vec, bmat, bmat, cmat, cmat],
        out_specs=[
            pl.BlockSpec((1, 1, BATCH, 2 * S5_LANES), lambda i, d: (i, d, 0, 0)),
            pl.BlockSpec((1, 1, BRANCH_W, 2 * S5_LANES), lambda i, d: (i, d, 0, 0)),
            pl.BlockSpec((1, 1, 2 * S5_LANES, BRANCH_W), lambda i, d: (i, d, 0, 0)),
        ],
        out_shape=[
            jax.ShapeDtypeStruct((DEPTH, 2, BATCH, 2 * S5_LANES), F32),
            jax.ShapeDtypeStruct((DEPTH, 2, BRANCH_W, 2 * S5_LANES), BF16),
            jax.ShapeDtypeStruct((DEPTH, 2, 2 * S5_LANES, BRANCH_W), BF16),
        ],
        compiler_params=_params(32, 2),
        name="s5_prep",
    )(lam_re, lam_im, log_dt, bre, bim, cre, cim)


def _s5_fwd_block(i):
    return jnp.where(i < S5_STEPS - S5_LAT_STEPS, S5_LAT_STEPS + i, i - (S5_STEPS - S5_LAT_STEPS))


def _s5_rev_block(i):
    return S5_STEPS - 1 - i


def _to_time_major(u_ref, slab_ref):
    for b in range(BATCH):
        ub = u_ref[b].astype(F32)
        for s in range(S5_SLABS):
            slab_ref[s, pl.ds(b, S5_T, stride=BATCH), :] = ub[:, s * LANES:(s + 1) * LANES]


def _s5_scan_kernel(uf_ref, ur_ref, a_ref, bb_ref, cm_ref, yf_ref, yr_ref, xf_ref, xr_ref, st_ref, usf_ref, usr_ref):
    i = pl.program_id(0)

    @pl.when(i == 0)
    def _():
        st_ref[...] = jnp.zeros_like(st_ref)

    mm_rows = 256
    dirs = ((uf_ref, yf_ref, xf_ref, usf_ref), (ur_ref, yr_ref, xr_ref, usr_ref))
    for d, (u_ref, y_ref, x_ref, us_ref) in enumerate(dirs):
        _to_time_major(u_ref, us_ref)
        for c in range(S5_ROWS // mm_rows):
            rs = slice(c * mm_rows, (c + 1) * mm_rows)
            u = jnp.concatenate([us_ref[s, rs, :] for s in range(S5_SLABS)], axis=1).astype(BF16)
            x_ref[rs, :] = _dot(u, bb_ref[0, d])
    for d, (u_ref, y_ref, x_ref, us_ref) in enumerate(dirs):
        ar = a_ref[0, d, :, 0:S5_LANES]
        ai = a_ref[0, d, :, S5_LANES:2 * S5_LANES]
        hr, hi = st_ref[d, :, 0:S5_LANES], st_ref[d, :, S5_LANES:2 * S5_LANES]
        for t in range(S5_T):
            r = (t if d == 0 else S5_T - 1 - t) * BATCH
            xr = x_ref[r:r + BATCH, 0:S5_LANES]
            xi = x_ref[r:r + BATCH, S5_LANES:2 * S5_LANES]
            hr, hi = ar * hr - ai * hi + xr, ar * hi + ai * hr + xi
            x_ref[r:r + BATCH, 0:S5_LANES] = hr
            x_ref[r:r + BATCH, S5_LANES:2 * S5_LANES] = hi
        st_ref[d, :, 0:S5_LANES] = hr
        st_ref[d, :, S5_LANES:2 * S5_LANES] = hi
        for c in range(S5_ROWS // mm_rows):
            rs = slice(c * mm_rows, (c + 1) * mm_rows)
            y_ref[rs, :] = _dot(x_ref[rs, :].astype(BF16), cm_ref[0, d])


def _s5_scan_call(layer, att, a_bar, bb, cm):
    u_blk = lambda f: pl.BlockSpec((BATCH, S5_T, BRANCH_W), lambda i: (0, f(i), 0))
    y_blk = lambda f: pl.BlockSpec((S5_ROWS, BRANCH_W), lambda i: (f(i), 0))
    return pl.pallas_call(
        _s5_scan_kernel,
        grid=(S5_STEPS,),
        in_specs=[
            u_blk(_s5_fwd_block),
            u_blk(_s5_rev_block),
            pl.BlockSpec((1, 2, BATCH, 2 * S5_LANES), lambda i: (layer, 0, 0, 0)),
            pl.BlockSpec((1, 2, BRANCH_W, 2 * S5_LANES), lambda i: (layer, 0, 0, 0)),
            pl.BlockSpec((1, 2, 2 * S5_LANES, BRANCH_W), lambda i: (layer, 0, 0, 0)),
        ],
        out_specs=[y_blk(_s5_fwd_block), y_blk(_s5_rev_block)],
        out_shape=[jax.ShapeDtypeStruct((TOK * BATCH, BRANCH_W), F32)] * 2,
        scratch_shapes=[
            pltpu.VMEM((S5_ROWS, 2 * S5_LANES), F32),
            pltpu.VMEM((S5_ROWS, 2 * S5_LANES), F32),
            pltpu.VMEM((2, BATCH, 2 * S5_LANES), F32),
            pltpu.VMEM((S5_SLABS, S5_ROWS, LANES), F32),
            pltpu.VMEM((S5_SLABS, S5_ROWS, LANES), F32),
        ],
        compiler_params=_params(48, 1),
        name="s5_scan",
    )(att, att, a_bar, bb, cm)


def _s5_glu_kernel(yf_ref, yr_ref, u_ref, d_ref, w_ref, b_ref, o_ref, ys_ref):
    y = yf_ref[...] + yr_ref[...]
    for s in range(S5_SLABS):
        ys_ref[s] = y[:, s * LANES:(s + 1) * LANES]
    for b in range(BATCH):
        yb = jnp.concatenate([ys_ref[s, pl.ds(b, S5_T, stride=BATCH), :] for s in range(S5_SLABS)], axis=1)
        g = jax.nn.gelu(yb + u_ref[b].astype(F32) * d_ref[0])
        o_ref[b] = (g * jax.nn.sigmoid(_dot(g.astype(BF16), w_ref[0]) + b_ref[0])).astype(BF16)


def _s5_glu_call(layer, yf, yr, att, d_skip, w, b):
    rows = pl.BlockSpec((S5_ROWS, BRANCH_W), lambda i: (i, 0))
    vec = pl.BlockSpec((1, 1, BRANCH_W), lambda i: (layer, 0, 0))
    tok = pl.BlockSpec((BATCH, S5_T, BRANCH_W), lambda i: (0, i, 0))
    return pl.pallas_call(
        _s5_glu_kernel,
        grid=(S5_STEPS,),
        in_specs=[rows, rows, tok, vec, pl.BlockSpec((1, BRANCH_W, BRANCH_W), lambda i: (layer, 0, 0)), vec],
        out_specs=tok,
        out_shape=jax.ShapeDtypeStruct((BATCH, TOK, BRANCH_W), BF16),
        scratch_shapes=[pltpu.VMEM((S5_SLABS, S5_ROWS, LANES), F32)],
        compiler_params=_params(32, 1),
        name="s5_glu",
    )(yf, yr, att, d_skip, w, b)


NA_CASE_FIRST, NA_CASE_INNER, NA_CASE_LAST = 0, 1, 2
NA_CASES = 3
N_RPB = (2 * NA_WIN_ROWS - 1) * (2 * NA_WIN_COLS - 1)


def _na_window_start(row0):
    return min(max(row0 - NA_WIN_ROWS // 2, 0), GRID_H - NA_WIN_KROWS)


def _na_bias_kernel(rpb_ref, o_ref):
    layer, h = pl.program_id(0), pl.program_id(1)
    base = (layer * NA_HEADS + h) * N_RPB
    qq = lax.broadcasted_iota(jnp.int32, (GRID_W, 2 * GRID_W), 0)
    lane = lax.broadcasted_iota(jnp.int32, (GRID_W, 2 * GRID_W), 1)
    kk = jnp.bitwise_and(lane, GRID_W - 1)
    second = lane >= GRID_W
    dcm = jnp.clip(kk - qq + NA_WIN_COLS - 1, 0, 2 * NA_WIN_COLS - 2)
    cs = jnp.clip(qq - NA_WIN_COLS // 2, 0, GRID_W - NA_WIN_COLS)
    col_ok = (kk >= cs) & (kk < cs + NA_WIN_COLS)
    neg = jnp.full((GRID_W, 2 * GRID_W), NEG_INF, F32)
    n_dr, n_dc = 2 * NA_WIN_ROWS - 1, 2 * NA_WIN_COLS - 1
    table = []
    for dr in range(n_dr):
        t = neg
        for dc in range(n_dc):
            t = jnp.where(dcm == dc, rpb_ref[base + dr * n_dc + dc], t)
        table.append(jnp.where(col_ok, t, NEG_INF))
    for case, row0 in ((NA_CASE_FIRST, 0), (NA_CASE_INNER, 4), (NA_CASE_LAST, GRID_H - 4)):
        start = _na_window_start(row0)
        for i in range(4):
            r = row0 + i
            row_start = min(max(r - NA_WIN_ROWS // 2, 0), GRID_H - NA_WIN_ROWS)
            for m in range(NA_WIN_KROWS // 2):
                halves = []
                for jj in (2 * m, 2 * m + 1):
                    kr = start + jj
                    valid = row_start <= kr < row_start + NA_WIN_ROWS
                    halves.append(table[kr - r + NA_WIN_ROWS - 1] if valid else neg)
                o_ref[0, case, 0, i * GRID_W:(i + 1) * GRID_W, m * 2 * GRID_W:(m + 1) * 2 * GRID_W] = (
                    jnp.where(second, halves[1], halves[0]))


def _na_bias_call(rpb):
    return pl.pallas_call(
        _na_bias_kernel,
        grid=(DEPTH, NA_HEADS),
        in_specs=[pl.BlockSpec(memory_space=pltpu.SMEM)],
        out_specs=pl.BlockSpec((1, NA_CASES, 1, NA_QB, NA_WIN_KEYS), lambda i, h: (i, 0, h, 0, 0)),
        out_shape=jax.ShapeDtypeStruct((DEPTH, NA_CASES, NA_HEADS, NA_QB, NA_WIN_KEYS), F32),
        compiler_params=_params(32, 2),
        name="na_bias",
    )(rpb.reshape(-1))


def _na_attend(q, blocks):
    head = lax.shift_right_logical(lax.broadcasted_iota(jnp.int32, (1, BRANCH_W), 1), int(math.log2(NA_HEAD_DIM)))
    acc = jnp.zeros((NA_QB, BRANCH_W), F32)
    for h in range(NA_HEADS):
        in_head = head == h
        qh = jnp.where(in_head, q, jnp.zeros_like(q))
        scores = [_dot_nt(qh, k) if bias is None else _dot_nt(qh, k) + bias(h) for k, _, bias in blocks]
        m = functools.reduce(jnp.maximum, [jnp.max(s, axis=-1, keepdims=True) for s in scores])
        probs = [jnp.exp(s - m) for s in scores]
        norm = sum(jnp.sum(p, axis=-1, keepdims=True) for p in probs)
        o = sum(_dot(p.astype(BF16), v) for p, (_, v, _) in zip(probs, blocks))
        acc = jnp.where(in_head, o * (1.0 / norm), acc)
    return acc.astype(BF16)


def _na_kernel(q_ref, k_ref, v_ref, bias_ref, o_ref):
    j = pl.program_id(1)
    ctx_block = (k_ref[0, SEQ:TOK, :], v_ref[0, SEQ:TOK, :], None)
    for i in range(NA_GROUP):
        blk = j * NA_GROUP + i
        start = jnp.clip(blk * 4 - NA_WIN_ROWS // 2, 0, GRID_H - NA_WIN_KROWS)
        k0 = pl.multiple_of(start * GRID_W, GRID_W)
        case = jnp.where(blk == 0, NA_CASE_FIRST, jnp.where(blk == NA_LAT_STEPS - 1, NA_CASE_LAST, NA_CASE_INNER))
        bias = lambda h, case=case: bias_ref[0, case, h]
        window = (k_ref[0, pl.ds(k0, NA_WIN_KEYS), :], v_ref[0, pl.ds(k0, NA_WIN_KEYS), :], bias)
        rs = slice(i * NA_QB, (i + 1) * NA_QB)
        o_ref[0, rs, :] = _na_attend(q_ref[0, rs, :] * NA_SCALE, [window, ctx_block])


def _na_call(layer, att, bias):
    rows = NA_GROUP * NA_QB
    return pl.pallas_call(
        _na_kernel,
        grid=(BATCH, SEQ // rows),
        in_specs=[
            pl.BlockSpec((1, rows, BRANCH_W), lambda b, j: (b, j, 1)),
            pl.BlockSpec((1, TOK, BRANCH_W), lambda b, j: (b, 0, 2)),
            pl.BlockSpec((1, TOK, BRANCH_W), lambda b, j: (b, 0, 3)),
            _resident((1, NA_CASES, NA_HEADS, NA_QB, NA_WIN_KEYS), lambda b, j: (layer, 0, 0, 0, 0)),
        ],
        out_specs=pl.BlockSpec((1, rows, BRANCH_W), lambda b, j: (b, j, 0)),
        out_shape=jax.ShapeDtypeStruct((BATCH, TOK, BRANCH_W), BF16),
        compiler_params=_params(48, 2),
        name="na",
    )(att, att, att, bias)


def _na_ctx_kernel(q_ref, k_ref, v_ref, y_hbm_ref, o_ref):
    del y_hbm_ref
    o_ref[0] = _na_attend(q_ref[0] * NA_SCALE, [(k_ref[0], v_ref[0], None)])


def _na_ctx_call(att, yb):
    ctx_blk = lambda col: pl.BlockSpec((1, CTX_LEN, BRANCH_W), lambda b: (b, SEQ // CTX_LEN, col))
    return pl.pallas_call(
        _na_ctx_kernel,
        grid=(BATCH,),
        in_specs=[ctx_blk(1), ctx_blk(2), ctx_blk(3), pl.BlockSpec(memory_space=pl.ANY)],
        out_specs=ctx_blk(0),
        out_shape=jax.ShapeDtypeStruct((BATCH, TOK, BRANCH_W), BF16),
        input_output_aliases={3: 0},
        compiler_params=_params(32, 1),
        name="na_ctx",
    )(att, att, att, yb)


HALO = 8


def _local_kernel(cb_ref, z_ref, zp_ref, zn_ref, su_ref, sv_ref, cw_ref, lg_ref, lb_ref, sw_ref, sb_ref,
                  yc_ref, yd_ref):
    j = pl.program_id(1)
    local = lax.broadcasted_iota(jnp.int32, (TILE, 1), 0)
    rows = j * TILE + local
    z = z_ref[0]
    prev = jnp.where(local == 0, zp_ref[0, HALO - 1:HALO, :], pltpu.roll(z, 1, 0))
    nxt = jnp.where(local == TILE - 1, zn_ref[0, 0:1, :], pltpu.roll(z, TILE - 1, 0))
    prev = jnp.where((rows == 0) | (rows == SEQ), 0.0, prev)
    nxt = jnp.where((rows == SEQ - 1) | (rows == TOK - 1), 0.0, nxt)
    y = cw_ref[0, 0:1, :] * prev + cw_ref[0, 1:2, :] * z + cw_ref[0, 2:3, :] * nxt
    yc_ref[0] = (cb_ref[0] * y).astype(BF16)

    v = sv_ref[0]
    mu = jnp.mean(v, axis=-1, keepdims=True)
    var = jnp.mean(jnp.square(v - mu), axis=-1, keepdims=True)
    vn = ((v - mu) * lax.rsqrt(var + EPS) * lg_ref[0] + lb_ref[0]).astype(BF16)
    group = lax.shift_right_logical(lax.broadcasted_iota(jnp.int32, (1, BRANCH_W), 1), int(math.log2(SGU_GROUP_W)))
    zero = jnp.zeros((CHUNK, BRANCH_W), BF16)
    for c in range(TILE // CHUNK):
        rs = slice(c * CHUNK, (c + 1) * CHUNK)
        vc = vn[rs, :]
        stacked = jnp.concatenate([jnp.where(group == g, vc, zero) for g in range(SGU_GROUPS)], axis=0)
        zc = _dot(sw_ref[0], stacked) + sb_ref[0]
        yd_ref[0, rs, :] = (su_ref[0, rs, :] * zc).astype(BF16)


def _local_call(layer, loc, conv_w, ln_g, ln_b, sgu_w, sgu_b):
    col = lambda k: pl.BlockSpec((1, TILE, BRANCH_W), lambda b, j: (b, j, k))
    halo_per_tile = TILE // HALO
    n_halo = TOK // HALO
    vec = pl.BlockSpec((1, 1, BRANCH_W), lambda b, j: (layer, 0, 0))
    return pl.pallas_call(
        _local_kernel,
        grid=(BATCH, N_TILES),
        in_specs=[
            col(0),
            col(1),
            pl.BlockSpec((1, HALO, BRANCH_W), lambda b, j: (b, jnp.maximum(j * halo_per_tile - 1, 0), 1)),
            pl.BlockSpec((1, HALO, BRANCH_W), lambda b, j: (b, jnp.minimum((j + 1) * halo_per_tile, n_halo - 1), 1)),
            col(2),
            col(3),
            pl.BlockSpec((1, CONV_W, BRANCH_W), lambda b, j: (layer, 0, 0)),
            vec,
            vec,
            pl.BlockSpec((1, CHUNK, SGU_GROUPS * CHUNK), lambda b, j: (layer, 0, 0)),
            pl.BlockSpec((1, CHUNK, BRANCH_W), lambda b, j: (layer, 0, 0)),
        ],
        out_specs=[pl.BlockSpec((1, TILE, BRANCH_W), lambda b, j: (b, j, 0))] * 2,
        out_shape=[jax.ShapeDtypeStruct((BATCH, TOK, BRANCH_W), BF16)] * 2,
        compiler_params=_params(32, 2),
        name="local",
    )(loc, loc, loc, loc, loc, loc, conv_w, ln_g, ln_b, sgu_w, sgu_b)


def _merge_kernel(*refs, tile, with_ctx):
    h_ref, ml_ref = refs[0], refs[1]
    mc_ref = refs[2] if with_ctx else None
    g_ref, ya_ref, yb_ref, yc_ref, yd_ref, wg_ref, wb_ref, wo_ref, o_ref = refs[2 + with_ctx:]
    j = pl.program_id(1)
    y_refs = (ya_ref, yb_ref, yc_ref, yd_ref)
    for r0, n in _sub_tiles(tile):
        rs = slice(r0, r0 + n)
        shift, scale, gate = _sub_mods(j, r0, tile, ml_ref, mc_ref, (0, 1, 2))
        h = h_ref[0, rs, :]
        a = _norm_modulate(h, g_ref[0] * (1.0 + scale), shift).astype(BF16)
        acc = jnp.zeros((n, D_MODEL), F32)
        for k in range(N_BRANCH):
            logits = _dot(a, wg_ref[0, :, SLOT_W + k * D_MODEL:SLOT_W + (k + 1) * D_MODEL])
            acc = acc + jax.nn.sigmoid(logits) * _dot(y_refs[k][0, rs, :], wb_ref[0, k])
        o_ref[0, rs, :] = h + gate * _dot(acc.astype(BF16), wo_ref[0])


def _merge_call(layer, h, mods, gains, ya, yb, yc, yd, w_in, w_branch, w_out, *, tile, n_tiles, with_ctx):
    tok = pl.BlockSpec((1, tile, D_MODEL), lambda b, j: (b, j, 0))
    br = pl.BlockSpec((1, tile, BRANCH_W), lambda b, j: (b, j, 0))
    mod_specs = _mod_specs(layer, with_ctx)
    return pl.pallas_call(
        functools.partial(_merge_kernel, tile=tile, with_ctx=with_ctx),
        grid=(BATCH, n_tiles),
        in_specs=[
            tok,
            *mod_specs,
            _gain_spec(layer, 0),
            br, br, br, br,
            _resident((1, D_MODEL, SLOT_W + GATE_W), lambda b, j: (layer, 0, 0)),
            _resident((1, N_BRANCH, BRANCH_W, D_MODEL), lambda b, j: (layer, 0, 0, 0)),
            _resident((1, D_MODEL, D_MODEL), lambda b, j: (layer, 0, 0)),
        ],
        out_specs=tok,
        out_shape=jax.ShapeDtypeStruct((BATCH, tile * n_tiles, D_MODEL), F32),
        compiler_params=_params(56, 2),
        name="merge",
    )(h, *([mods] * len(mod_specs)), gains, ya, yb, yc, yd, w_in, w_branch, w_out)


FF_CHUNK = 1024


def _ffn_kernel(*refs, tile, with_ctx, final):
    h_ref, ml_ref = refs[0], refs[1]
    mc_ref = refs[2] if with_ctx else None
    g_ref, w1_ref, w2_ref = refs[2 + with_ctx:5 + with_ctx]
    fg_ref = refs[5 + with_ctx] if final else None
    o_ref = refs[-1]
    j = pl.program_id(1)
    for r0, n in _sub_tiles(tile):
        rs = slice(r0, r0 + n)
        shift, scale, gate = _sub_mods(j, r0, tile, ml_ref, mc_ref, (3, 4, 5))
        h = h_ref[0, rs, :]
        a = _norm_modulate(h, g_ref[0] * (1.0 + scale), shift).astype(BF16)
        acc = jnp.zeros((n, D_MODEL), F32)
        for c in range(D_FF // FF_CHUNK):
            cs = slice(c * FF_CHUNK, (c + 1) * FF_CHUNK)
            t = jnp.maximum(_dot(a, w1_ref[0, :, cs]), 0.0)
            acc = acc + _dot((t * t).astype(BF16), w2_ref[0, cs, :])
        out = h + gate * acc
        if final:
            out = out * lax.rsqrt(jnp.mean(out * out, axis=-1, keepdims=True) + EPS) * fg_ref[...]
        o_ref[0, rs, :] = out


def _ffn_call(layer, h, mods, gains, w1, w2, final_g=None, *, tile, n_tiles, with_ctx):
    tok = pl.BlockSpec((1, tile, D_MODEL), lambda b, j: (b, j, 0))
    final = final_g is not None
    mod_specs = _mod_specs(layer, with_ctx)
    in_specs = [
        tok,
        *mod_specs,
        _gain_spec(layer, 1),
        _resident((1, D_MODEL, D_FF), lambda b, j: (layer, 0, 0)),
        _resident((1, D_FF, D_MODEL), lambda b, j: (layer, 0, 0)),
    ]
    args = [h, *([mods] * len(mod_specs)), gains, w1, w2]
    if final:
        in_specs.append(pl.BlockSpec((1, D_MODEL), lambda b, j: (0, 0)))
        args.append(final_g)
    return pl.pallas_call(
        functools.partial(_ffn_kernel, tile=tile, with_ctx=with_ctx, final=final),
        grid=(BATCH, n_tiles),
        in_specs=in_specs,
        out_specs=tok,
        out_shape=jax.ShapeDtypeStruct((BATCH, tile * n_tiles, D_MODEL), F32),
        compiler_params=_params(58, 2),
        name="ffn_final" if final else "ffn",
    )(*args)


def kernel(x, c, ctx, c_ctx, ada_w, ada_b, norm_g, final_g, w_in, w_branch, w_out, s5_lam_re, s5_lam_im, s5_log_dt, s5_b_re, s5_b_im, s5_c_re, s5_c_im, s5_d, s5_glu_w, s5_glu_b, na_rpb, conv_w, sgu_ln_g, sgu_ln_b, sgu_w, sgu_b, w_ff1, w_ff2):
    h = jnp.concatenate([x, ctx], axis=1)
    cc = jnp.concatenate([c, c_ctx[None], jnp.zeros((MOD_ROWS - BATCH - 1, D_MODEL), F32)], axis=0)
    b_rows = lambda p: jnp.tile(jnp.swapaxes(p, 3, 4).reshape(DEPTH, 2, BRANCH_W, S5_STATE), (1, 1, 1, LANES // S5_STATE))
    c_rows = lambda p: jnp.tile(jnp.swapaxes(p, 3, 4).reshape(DEPTH, 2, S5_LANES, S5_GROUP), (1, 1, 1, LANES // S5_GROUP))
    bre, bim, cre, cim = b_rows(s5_b_re), b_rows(s5_b_im), c_rows(s5_c_re), c_rows(s5_c_im)
    lam_re = s5_lam_re.reshape(DEPTH, 2, 1, S5_LANES)
    lam_im = s5_lam_im.reshape(DEPTH, 2, 1, S5_LANES)
    log_dt = jnp.repeat(s5_log_dt, S5_STATE, axis=-1).reshape(DEPTH, 2, 1, S5_LANES)
    sgu_bias = jnp.repeat(jnp.swapaxes(sgu_b, 1, 2), SGU_GROUP_W, axis=2)
    w_in_b = w_in.astype(BF16)
    w_branch_b, w_out_b = w_branch.astype(BF16), w_out.astype(BF16)
    w_ff1_b, w_ff2_b = w_ff1.astype(BF16), w_ff2.astype(BF16)
    glu_w_b = s5_glu_w.astype(BF16)
    sgu_w_b = jnp.swapaxes(sgu_w, 1, 2).reshape(DEPTH, CHUNK, SGU_GROUPS * CHUNK).astype(BF16)
    gains = norm_g.reshape(DEPTH * 2, 1, D_MODEL)
    vec3 = lambda p: p.reshape(DEPTH, 1, BRANCH_W)

    mods = _mod_call(cc, ada_w, ada_b).reshape(DEPTH * MOD_ROWS, 1, MOD_W)
    a_bar, bb, cm = _s5_prep_call(lam_re, lam_im, log_dt, bre, bim, cre, cim)
    na_bias = _na_bias_call(na_rpb)

    for i in range(DEPTH):
        last = i == DEPTH - 1
        att, loc = _proj_call(i, h, mods, gains, w_in_b)
        yf, yr = _s5_scan_call(i, att, a_bar, bb, cm)
        ya = _s5_glu_call(i, yf, yr, att, vec3(s5_d), glu_w_b, vec3(s5_glu_b))
        yb = _na_call(i, att, na_bias)
        if not last:
            yb = _na_ctx_call(att, yb)
        yc, yd = _local_call(i, loc, conv_w, vec3(sgu_ln_g), vec3(sgu_ln_b), sgu_w_b, sgu_bias)
        tiling = dict(tile=TILE_LATENT, n_tiles=SEQ // TILE_LATENT, with_ctx=False) if last else dict(
            tile=TILE, n_tiles=N_TILES, with_ctx=True)
        h = _merge_call(i, h, mods, gains, ya, yb, yc, yd, w_in_b, w_branch_b, w_out_b, **tiling)
        h = _ffn_call(i, h, mods, gains, w_ff1_b, w_ff2_b, final_g[None] if last else None, **tiling)
    return h
```

```python
import functools
import math

import jax
import jax.numpy as jnp
from jax import lax
from jax.experimental import pallas as pl
from jax.experimental.pallas import tpu as pltpu

F32 = jnp.float32
BF16 = jnp.bfloat16

D_MODEL = 1024
BATCH = 8
SEQ = 2048
DEPTH = 4
GRID_W = 64
GRID_H = SEQ // GRID_W
CTX_LEN = 256
TOK = SEQ + CTX_LEN
N_BRANCH = 4
BRANCH_W = D_MODEL // 4
S5_GROUP = 16
S5_GROUPS = BRANCH_W // S5_GROUP
S5_STATE = 64
S5_LANES = S5_GROUPS * S5_STATE
NA_HEAD_DIM = 64
NA_HEADS = BRANCH_W // NA_HEAD_DIM
NA_WIN_ROWS = 8
NA_WIN_COLS = 16
NA_SCALE = NA_HEAD_DIM ** -0.5
CONV_W = 3
CHUNK = 128
SGU_GROUPS = 4
SGU_GROUP_W = BRANCH_W // SGU_GROUPS
D_FF = 4 * D_MODEL
N_MOD = 6
MOD_W = N_MOD * D_MODEL
EPS = 1e-6
NEG_INF = -1e30
N_SLOTS = 9
SLOT_W = N_SLOTS * BRANCH_W
ATT_W = 4 * BRANCH_W
LOC_W = 5 * BRANCH_W
GATE_W = N_BRANCH * D_MODEL

LANES = 128
MOD_ROWS = 16
MOD_CTX_ROW = BATCH
TILE = 1152
N_TILES = TOK // TILE
TILE_LATENT = 1024
SUB = 256
S5_T = 128
S5_ROWS = S5_T * BATCH
S5_STEPS = TOK // S5_T
S5_LAT_STEPS = SEQ // S5_T
S5_SLABS = BRANCH_W // LANES
NA_QB = 4 * GRID_W
NA_WIN_KROWS = 12
NA_WIN_KEYS = NA_WIN_KROWS * GRID_W
NA_LAT_STEPS = SEQ // NA_QB
MIB = 1024 * 1024
VMEM_LIMIT_MIB = 56


def _params(n_axes):
    return pltpu.CompilerParams(
        dimension_semantics=("arbitrary",) * n_axes,
        vmem_limit_bytes=VMEM_LIMIT_MIB * MIB,
    )


def _resident(shape, index_map):
    return pl.BlockSpec(shape, index_map, pipeline_mode=pl.Buffered(1))


def _dot(a, b):
    return jnp.dot(a, b, preferred_element_type=F32)


def _dot_nt(a, b):
    return lax.dot_general(a, b, (((1,), (1,)), ((), ())), preferred_element_type=F32)


def _norm_modulate(h, gain, shift):
    return h * lax.rsqrt(jnp.mean(h * h, axis=-1, keepdims=True) + EPS) * gain + shift


def _sub_tiles(tile):
    ctx_row0 = SEQ % tile or tile
    bounds = list(range(0, ctx_row0, SUB)) + list(range(ctx_row0, tile, SUB)) + [tile]
    return [(lo, hi - lo) for lo, hi in zip(bounds[:-1], bounds[1:])]


def _sub_mods(j, row0, tile, ml_ref, mc_ref, ks):
    vec = lambda ref, k: ref[0, :, k * D_MODEL:(k + 1) * D_MODEL]
    ctx_row0 = SEQ % tile
    if mc_ref is None or row0 < ctx_row0:
        return [vec(ml_ref, k) for k in ks]
    in_ctx = j == TOK // tile - 1
    return [jnp.where(in_ctx, vec(mc_ref, k), vec(ml_ref, k)) for k in ks]


def _mod_specs(layer, with_ctx=True):
    specs = [pl.BlockSpec((1, 1, MOD_W), lambda b, j: (layer * MOD_ROWS + b, 0, 0))]
    if with_ctx:
        specs.append(pl.BlockSpec((1, 1, MOD_W), lambda b, j: (layer * MOD_ROWS + MOD_CTX_ROW, 0, 0)))
    return specs


def _gain_spec(layer, which):
    return pl.BlockSpec((1, 1, D_MODEL), lambda b, j: (2 * layer + which, 0, 0))


MOD_NT = 1536


def _mod_kernel(cc_ref, w_ref, b_ref, o_ref):
    s = jax.nn.silu(cc_ref[...])
    w = w_ref[0]
    s_hi = s.astype(BF16)
    s_lo = (s - s_hi.astype(F32)).astype(BF16)
    w_hi = w.astype(BF16)
    w_lo = (w - w_hi.astype(F32)).astype(BF16)
    o_ref[0] = _dot(s_hi, w_hi) + _dot(s_lo, w_hi) + _dot(s_hi, w_lo) + b_ref[0]


def _mod_call(cc, ada_w, ada_b):
    return pl.pallas_call(
        _mod_kernel,
        grid=(DEPTH, MOD_W // MOD_NT),
        in_specs=[
            pl.BlockSpec((MOD_ROWS, D_MODEL), lambda i, j: (0, 0)),
            pl.BlockSpec((1, D_MODEL, MOD_NT), lambda i, j: (i, 0, j)),
            pl.BlockSpec((1, 1, MOD_NT), lambda i, j: (i, 0, j)),
        ],
        out_specs=pl.BlockSpec((1, MOD_ROWS, MOD_NT), lambda i, j: (i, 0, j)),
        out_shape=jax.ShapeDtypeStruct((DEPTH, MOD_ROWS, MOD_W), F32),
        compiler_params=_params(2),
        name="mod",
    )(cc, ada_w, ada_b.reshape(DEPTH, 1, MOD_W))


def _proj_kernel(h_ref, ml_ref, mc_ref, g_ref, w_ref, att_ref, loc_ref):
    j = pl.program_id(1)
    for r0, n in _sub_tiles(TILE):
        rs = slice(r0, r0 + n)
        shift, scale = _sub_mods(j, r0, TILE, ml_ref, mc_ref, (0, 1))
        a = _norm_modulate(h_ref[0, rs, :], g_ref[0] * (1.0 + scale), shift).astype(BF16)
        att_ref[0, rs, :] = _dot(a, w_ref[0, :, 0:ATT_W]).astype(BF16)
        p = _dot(a, w_ref[0, :, ATT_W:SLOT_W])
        loc_ref[0, rs, 0:BRANCH_W] = p[:, 0:BRANCH_W]
        loc_ref[0, rs, BRANCH_W:2 * BRANCH_W] = p[:, BRANCH_W:2 * BRANCH_W] * p[:, 2 * BRANCH_W:3 * BRANCH_W]
        loc_ref[0, rs, 2 * BRANCH_W:4 * BRANCH_W] = p[:, 3 * BRANCH_W:5 * BRANCH_W]


def _proj_call(layer, h, mods, gains, w_in):
    return pl.pallas_call(
        _proj_kernel,
        grid=(BATCH, N_TILES),
        in_specs=[
            pl.BlockSpec((1, TILE, D_MODEL), lambda b, j: (b, j, 0)),
            *_mod_specs(layer),
            _gain_spec(layer, 0),
            _resident((1, D_MODEL, SLOT_W), lambda b, j: (layer, 0, 0)),
        ],
        out_specs=[
            pl.BlockSpec((1, TILE, ATT_W), lambda b, j: (b, j, 0)),
            pl.BlockSpec((1, TILE, 4 * BRANCH_W), lambda b, j: (b, j, 0)),
        ],
        out_shape=[
            jax.ShapeDtypeStruct((BATCH, TOK, ATT_W), BF16),
            jax.ShapeDtypeStruct((BATCH, TOK, 4 * BRANCH_W), F32),
        ],
        compiler_params=_params(2),
        name="proj",
    )(h, mods, mods, gains, w_in)


def _s5_prep_kernel(lre_ref, lim_ref, ldt_ref, bre_ref, bim_ref, cre_ref, cim_ref, a_ref, bb_ref, cm_ref):
    lam_re = lre_ref[0, 0]
    lam_im = lim_ref[0, 0]
    dt = jnp.exp(ldt_ref[0, 0])
    zr, zi = lam_re * dt, lam_im * dt
    mag = jnp.exp(zr)
    ar, ai = mag * jnp.cos(zi), mag * jnp.sin(zi)
    nr, ni = ar - 1.0, ai
    den = lam_re * lam_re + lam_im * lam_im
    cr = (nr * lam_re + ni * lam_im) / den
    ci = (ni * lam_re - nr * lam_im) / den
    b_row_g = lax.shift_right_logical(lax.broadcasted_iota(jnp.int32, (BRANCH_W, 1), 0), int(math.log2(S5_GROUP)))
    b_col_g = lax.shift_right_logical(lax.broadcasted_iota(jnp.int32, (1, S5_LANES), 1), int(math.log2(S5_STATE)))
    b_own = b_row_g == b_col_g
    expand_b = lambda ref: jnp.where(b_own, jnp.concatenate([ref[0, 0]] * (S5_LANES // LANES), axis=1), 0.0)
    bre, bim = expand_b(bre_ref), expand_b(bim_ref)
    bb_ref[0, 0, :, 0:S5_LANES] = (cr * bre - ci * bim).astype(BF16)
    bb_ref[0, 0, :, S5_LANES:2 * S5_LANES] = (cr * bim + ci * bre).astype(BF16)
    a_ref[0, 0, :, 0:S5_LANES] = jnp.broadcast_to(ar, (BATCH, S5_LANES))
    a_ref[0, 0, :, S5_LANES:2 * S5_LANES] = jnp.broadcast_to(ai, (BATCH, S5_LANES))
    c_row_g = lax.shift_right_logical(lax.broadcasted_iota(jnp.int32, (S5_LANES, 1), 0), int(math.log2(S5_STATE)))
    c_col_g = lax.shift_right_logical(lax.broadcasted_iota(jnp.int32, (1, BRANCH_W), 1), int(math.log2(S5_GROUP)))
    c_own = c_row_g == c_col_g
    expand_c = lambda ref: jnp.where(c_own, jnp.concatenate([ref[0, 0]] * (BRANCH_W // LANES), axis=1), 0.0)
    cm_ref[0, 0, 0:S5_LANES, :] = expand_c(cre_ref).astype(BF16)
    cm_ref[0, 0, S5_LANES:2 * S5_LANES, :] = (-expand_c(cim_ref)).astype(BF16)


def _s5_prep_call(lam_re, lam_im, log_dt, bre, bim, cre, cim):
    vec = pl.BlockSpec((1, 1, 1, S5_LANES), lambda i, d: (i, d, 0, 0))
    bmat = pl.BlockSpec((1, 1, BRANCH_W, LANES), lambda i, d: (i, d, 0, 0))
    cmat = pl.BlockSpec((1, 1, S5_LANES, LANES), lambda i, d: (i, d, 0, 0))
    return pl.pallas_call(
        _s5_prep_kernel,
        grid=(DEPTH, 2),
        in_specs=[vec, vec, vec, bmat, bmat, cmat, cmat],
        out_specs=[
            pl.BlockSpec((1, 1, BATCH, 2 * S5_LANES), lambda i, d: (i, d, 0, 0)),
            pl.BlockSpec((1, 1, BRANCH_W, 2 * S5_LANES), lambda i, d: (i, d, 0, 0)),
            pl.BlockSpec((1, 1, 2 * S5_LANES, BRANCH_W), lambda i, d: (i, d, 0, 0)),
        ],
        out_shape=[
            jax.ShapeDtypeStruct((DEPTH, 2, BATCH, 2 * S5_LANES), F32),
            jax.ShapeDtypeStruct((DEPTH, 2, BRANCH_W, 2 * S5_LANES), BF16),
            jax.ShapeDtypeStruct((DEPTH, 2, 2 * S5_LANES, BRANCH_W), BF16),
        ],
        compiler_params=_params(2),
        name="s5_prep",
    )(lam_re, lam_im, log_dt, bre, bim, cre, cim)


def _s5_fwd_block(i):
    return jnp.where(i < S5_STEPS - S5_LAT_STEPS, S5_LAT_STEPS + i, i - (S5_STEPS - S5_LAT_STEPS))


def _s5_rev_block(i):
    return S5_STEPS - 1 - i


def _to_time_major(u_ref, slab_ref):
    for b in range(BATCH):
        ub = u_ref[b].astype(F32)
        for s in range(S5_SLABS):
            slab_ref[s, pl.ds(b, S5_T, stride=BATCH), :] = ub[:, s * LANES:(s + 1) * LANES]


def _s5_scan_kernel(uf_ref, ur_ref, a_ref, bb_ref, cm_ref, yf_ref, yr_ref, xf_ref, xr_ref, st_ref, usf_ref, usr_ref):
    i = pl.program_id(0)

    @pl.when(i == 0)
    def _():
        st_ref[...] = jnp.zeros_like(st_ref)

    mm_rows = 256
    dirs = ((uf_ref, yf_ref, xf_ref, usf_ref), (ur_ref, yr_ref, xr_ref, usr_ref))
    for d, (u_ref, y_ref, x_ref, us_ref) in enumerate(dirs):
        _to_time_major(u_ref, us_ref)
        for c in range(S5_ROWS // mm_rows):
            rs = slice(c * mm_rows, (c + 1) * mm_rows)
            u = jnp.concatenate([us_ref[s, rs, :] for s in range(S5_SLABS)], axis=1).astype(BF16)
            x_ref[rs, :] = _dot(u, bb_ref[0, d])
    for d, (u_ref, y_ref, x_ref, us_ref) in enumerate(dirs):
        ar = a_ref[0, d, :, 0:S5_LANES]
        ai = a_ref[0, d, :, S5_LANES:2 * S5_LANES]
        hr, hi = st_ref[d, :, 0:S5_LANES], st_ref[d, :, S5_LANES:2 * S5_LANES]
        for t in range(S5_T):
            r = (t if d == 0 else S5_T - 1 - t) * BATCH
            xr = x_ref[r:r + BATCH, 0:S5_LANES]
            xi = x_ref[r:r + BATCH, S5_LANES:2 * S5_LANES]
            hr, hi = ar * hr - ai * hi + xr, ar * hi + ai * hr + xi
            x_ref[r:r + BATCH, 0:S5_LANES] = hr
            x_ref[r:r + BATCH, S5_LANES:2 * S5_LANES] = hi
        st_ref[d, :, 0:S5_LANES] = hr
        st_ref[d, :, S5_LANES:2 * S5_LANES] = hi
        for c in range(S5_ROWS // mm_rows):
            rs = slice(c * mm_rows, (c + 1) * mm_rows)
            y_ref[rs, :] = _dot(x_ref[rs, :].astype(BF16), cm_ref[0, d])


def _s5_scan_call(layer, att, a_bar, bb, cm):
    u_blk = lambda f: pl.BlockSpec((BATCH, S5_T, BRANCH_W), lambda i: (0, f(i), 0))
    y_blk = lambda f: pl.BlockSpec((S5_ROWS, BRANCH_W), lambda i: (f(i), 0))
    return pl.pallas_call(
        _s5_scan_kernel,
        grid=(S5_STEPS,),
        in_specs=[
            u_blk(_s5_fwd_block),
            u_blk(_s5_rev_block),
            pl.BlockSpec((1, 2, BATCH, 2 * S5_LANES), lambda i: (layer, 0, 0, 0)),
            pl.BlockSpec((1, 2, BRANCH_W, 2 * S5_LANES), lambda i: (layer, 0, 0, 0)),
            pl.BlockSpec((1, 2, 2 * S5_LANES, BRANCH_W), lambda i: (layer, 0, 0, 0)),
        ],
        out_specs=[y_blk(_s5_fwd_block), y_blk(_s5_rev_block)],
        out_shape=[jax.ShapeDtypeStruct((TOK * BATCH, BRANCH_W), F32)] * 2,
        scratch_shapes=[
            pltpu.VMEM((S5_ROWS, 2 * S5_LANES), F32),
            pltpu.VMEM((S5_ROWS, 2 * S5_LANES), F32),
            pltpu.VMEM((2, BATCH, 2 * S5_LANES), F32),
            pltpu.VMEM((S5_SLABS, S5_ROWS, LANES), F32),
            pltpu.VMEM((S5_SLABS, S5_ROWS, LANES), F32),
        ],
        compiler_params=_params(1),
        name="s5_scan",
    )(att, att, a_bar, bb, cm)


def _s5_glu_kernel(yf_ref, yr_ref, u_ref, d_ref, w_ref, b_ref, o_ref, ys_ref):
    y = yf_ref[...] + yr_ref[...]
    for s in range(S5_SLABS):
        ys_ref[s] = y[:, s * LANES:(s + 1) * LANES]
    for b in range(BATCH):
        yb = jnp.concatenate([ys_ref[s, pl.ds(b, S5_T, stride=BATCH), :] for s in range(S5_SLABS)], axis=1)
        g = jax.nn.gelu(yb + u_ref[b].astype(F32) * d_ref[0])
        o_ref[b] = (g * jax.nn.sigmoid(_dot(g.astype(BF16), w_ref[0]) + b_ref[0])).astype(BF16)


def _s5_glu_call(layer, yf, yr, att, d_skip, w, b):
    rows = pl.BlockSpec((S5_ROWS, BRANCH_W), lambda i: (i, 0))
    vec = pl.BlockSpec((1, 1, BRANCH_W), lambda i: (layer, 0, 0))
    tok = pl.BlockSpec((BATCH, S5_T, BRANCH_W), lambda i: (0, i, 0))
    return pl.pallas_call(
        _s5_glu_kernel,
        grid=(S5_STEPS,),
        in_specs=[rows, rows, tok, vec, pl.BlockSpec((1, BRANCH_W, BRANCH_W), lambda i: (layer, 0, 0)), vec],
        out_specs=tok,
        out_shape=jax.ShapeDtypeStruct((BATCH, TOK, BRANCH_W), BF16),
        scratch_shapes=[pltpu.VMEM((S5_SLABS, S5_ROWS, LANES), F32)],
        compiler_params=_params(1),
        name="s5_glu",
    )(yf, yr, att, d_skip, w, b)


NA_CASE_FIRST, NA_CASE_INNER, NA_CASE_LAST, NA_CASE_CTX = 0, 1, 2, 3
NA_CASES = 4
N_RPB = (2 * NA_WIN_ROWS - 1) * (2 * NA_WIN_COLS - 1)


def _na_window_start(row0):
    return min(max(row0 - NA_WIN_ROWS // 2, 0), GRID_H - NA_WIN_KROWS)


def _na_bias_kernel(rpb_ref, o_ref):
    layer, h = pl.program_id(0), pl.program_id(1)
    base = (layer * NA_HEADS + h) * N_RPB
    qq = lax.broadcasted_iota(jnp.int32, (GRID_W, 2 * GRID_W), 0)
    lane = lax.broadcasted_iota(jnp.int32, (GRID_W, 2 * GRID_W), 1)
    kk = jnp.bitwise_and(lane, GRID_W - 1)
    second = lane >= GRID_W
    dcm = jnp.clip(kk - qq + NA_WIN_COLS - 1, 0, 2 * NA_WIN_COLS - 2)
    cs = jnp.clip(qq - NA_WIN_COLS // 2, 0, GRID_W - NA_WIN_COLS)
    col_ok = (kk >= cs) & (kk < cs + NA_WIN_COLS)
    neg = jnp.full((GRID_W, 2 * GRID_W), NEG_INF, F32)
    n_dr, n_dc = 2 * NA_WIN_ROWS - 1, 2 * NA_WIN_COLS - 1
    table = []
    for dr in range(n_dr):
        t = neg
        for dc in range(n_dc):
            t = jnp.where(dcm == dc, rpb_ref[base + dr * n_dc + dc], t)
        table.append(jnp.where(col_ok, t, NEG_INF))
    for case, row0 in ((NA_CASE_FIRST, 0), (NA_CASE_INNER, 4), (NA_CASE_LAST, GRID_H - 4)):
        start = _na_window_start(row0)
        for i in range(4):
            r = row0 + i
            row_start = min(max(r - NA_WIN_ROWS // 2, 0), GRID_H - NA_WIN_ROWS)
            for m in range(NA_WIN_KROWS // 2):
                halves = []
                for jj in (2 * m, 2 * m + 1):
                    kr = start + jj
                    valid = row_start <= kr < row_start + NA_WIN_ROWS
                    halves.append(table[kr - r + NA_WIN_ROWS - 1] if valid else neg)
                o_ref[0, case, 0, i * GRID_W:(i + 1) * GRID_W, m * 2 * GRID_W:(m + 1) * 2 * GRID_W] = (
                    jnp.where(second, halves[1], halves[0]))
    o_ref[0, NA_CASE_CTX, 0] = jnp.full((NA_QB, NA_WIN_KEYS), NEG_INF, F32)


def _na_bias_call(rpb):
    return pl.pallas_call(
        _na_bias_kernel,
        grid=(DEPTH, NA_HEADS),
        in_specs=[pl.BlockSpec(memory_space=pltpu.SMEM)],
        out_specs=pl.BlockSpec((1, NA_CASES, 1, NA_QB, NA_WIN_KEYS), lambda i, h: (i, 0, h, 0, 0)),
        out_shape=jax.ShapeDtypeStruct((DEPTH, NA_CASES, NA_HEADS, NA_QB, NA_WIN_KEYS), F32),
        compiler_params=_params(2),
        name="na_bias",
    )(rpb.reshape(-1))


def _na_attend(q, blocks):
    head = lax.shift_right_logical(lax.broadcasted_iota(jnp.int32, (1, BRANCH_W), 1), int(math.log2(NA_HEAD_DIM)))
    acc = jnp.zeros((NA_QB, BRANCH_W), F32)
    for h in range(NA_HEADS):
        in_head = head == h
        qh = jnp.where(in_head, q, jnp.zeros_like(q))
        scores = [_dot_nt(qh, k) if bias is None else _dot_nt(qh, k) + bias(h) for k, _, bias in blocks]
        m = functools.reduce(jnp.maximum, [jnp.max(s, axis=-1, keepdims=True) for s in scores])
        probs = [jnp.exp(s - m) for s in scores]
        norm = sum(jnp.sum(p, axis=-1, keepdims=True) for p in probs)
        o = sum(_dot(p.astype(BF16), v) for p, (_, v, _) in zip(probs, blocks))
        acc = jnp.where(in_head, o * (1.0 / norm), acc)
    return acc.astype(BF16)


def _na_kernel(q_ref, k_ref, v_ref, bias_ref, o_ref, *, group):
    j = pl.program_id(1)
    ctx_block = (k_ref[0, SEQ:TOK, :], v_ref[0, SEQ:TOK, :], None)
    for i in range(group):
        blk = j * group + i
        start = jnp.clip(blk * 4 - NA_WIN_ROWS // 2, 0, GRID_H - NA_WIN_KROWS)
        k0 = pl.multiple_of(start * GRID_W, GRID_W)
        case = jnp.where(blk == 0, NA_CASE_FIRST, jnp.where(
            blk < NA_LAT_STEPS - 1, NA_CASE_INNER, jnp.where(blk == NA_LAT_STEPS - 1, NA_CASE_LAST, NA_CASE_CTX)))
        bias = lambda h, case=case: bias_ref[0, case, h]
        window = (k_ref[0, pl.ds(k0, NA_WIN_KEYS), :], v_ref[0, pl.ds(k0, NA_WIN_KEYS), :], bias)
        rs = slice(i * NA_QB, (i + 1) * NA_QB)
        o_ref[0, rs, :] = _na_attend(q_ref[0, rs, :] * NA_SCALE, [window, ctx_block])


def _na_call(layer, att, bias, *, group, n_rows):
    rows = group * NA_QB
    return pl.pallas_call(
        functools.partial(_na_kernel, group=group),
        grid=(BATCH, n_rows // rows),
        in_specs=[
            pl.BlockSpec((1, rows, BRANCH_W), lambda b, j: (b, j, 1)),
            pl.BlockSpec((1, TOK, BRANCH_W), lambda b, j: (b, 0, 2)),
            pl.BlockSpec((1, TOK, BRANCH_W), lambda b, j: (b, 0, 3)),
            _resident((1, NA_CASES, NA_HEADS, NA_QB, NA_WIN_KEYS), lambda b, j: (layer, 0, 0, 0, 0)),
        ],
        out_specs=pl.BlockSpec((1, rows, BRANCH_W), lambda b, j: (b, j, 0)),
        out_shape=jax.ShapeDtypeStruct((BATCH, n_rows, BRANCH_W), BF16),
        compiler_params=_params(2),
        name="na",
    )(att, att, att, bias)


HALO = 8


def _local_kernel(cb_ref, z_ref, zp_ref, zn_ref, su_ref, sv_ref, cw_ref, lg_ref, lb_ref, sw_ref, sb_ref,
                  yc_ref, yd_ref):
    j = pl.program_id(1)
    local = lax.broadcasted_iota(jnp.int32, (TILE, 1), 0)
    rows = j * TILE + local
    z = z_ref[0]
    prev = jnp.where(local == 0, zp_ref[0, HALO - 1:HALO, :], pltpu.roll(z, 1, 0))
    nxt = jnp.where(local == TILE - 1, zn_ref[0, 0:1, :], pltpu.roll(z, TILE - 1, 0))
    prev = jnp.where((rows == 0) | (rows == SEQ), 0.0, prev)
    nxt = jnp.where((rows == SEQ - 1) | (rows == TOK - 1), 0.0, nxt)
    y = cw_ref[0, 0:1, :] * prev + cw_ref[0, 1:2, :] * z + cw_ref[0, 2:3, :] * nxt
    yc_ref[0] = (cb_ref[0] * y).astype(BF16)

    v = sv_ref[0]
    mu = jnp.mean(v, axis=-1, keepdims=True)
    var = jnp.mean(jnp.square(v - mu), axis=-1, keepdims=True)
    vn = ((v - mu) * lax.rsqrt(var + EPS) * lg_ref[0] + lb_ref[0]).astype(BF16)
    group = lax.shift_right_logical(lax.broadcasted_iota(jnp.int32, (1, BRANCH_W), 1), int(math.log2(SGU_GROUP_W)))
    zero = jnp.zeros((CHUNK, BRANCH_W), BF16)
    for c in range(TILE // CHUNK):
        rs = slice(c * CHUNK, (c + 1) * CHUNK)
        vc = vn[rs, :]
        stacked = jnp.concatenate([jnp.where(group == g, vc, zero) for g in range(SGU_GROUPS)], axis=0)
        zc = _dot(sw_ref[0], stacked) + sb_ref[0]
        yd_ref[0, rs, :] = (su_ref[0, rs, :] * zc).astype(BF16)


def _local_call(layer, loc, conv_w, ln_g, ln_b, sgu_w, sgu_b):
    col = lambda k: pl.BlockSpec((1, TILE, BRANCH_W), lambda b, j: (b, j, k))
    halo_per_tile = TILE // HALO
    n_halo = TOK // HALO
    vec = pl.BlockSpec((1, 1, BRANCH_W), lambda b, j: (layer, 0, 0))
    return pl.pallas_call(
        _local_kernel,
        grid=(BATCH, N_TILES),
        in_specs=[
            col(0),
            col(1),
            pl.BlockSpec((1, HALO, BRANCH_W), lambda b, j: (b, jnp.maximum(j * halo_per_tile - 1, 0), 1)),
            pl.BlockSpec((1, HALO, BRANCH_W), lambda b, j: (b, jnp.minimum((j + 1) * halo_per_tile, n_halo - 1), 1)),
            col(2),
            col(3),
            pl.BlockSpec((1, CONV_W, BRANCH_W), lambda b, j: (layer, 0, 0)),
            vec,
            vec,
            pl.BlockSpec((1, CHUNK, SGU_GROUPS * CHUNK), lambda b, j: (layer, 0, 0)),
            pl.BlockSpec((1, CHUNK, BRANCH_W), lambda b, j: (layer, 0, 0)),
        ],
        out_specs=[pl.BlockSpec((1, TILE, BRANCH_W), lambda b, j: (b, j, 0))] * 2,
        out_shape=[jax.ShapeDtypeStruct((BATCH, TOK, BRANCH_W), BF16)] * 2,
        compiler_params=_params(2),
        name="local",
    )(loc, loc, loc, loc, loc, loc, conv_w, ln_g, ln_b, sgu_w, sgu_b)


def _merge_kernel(*refs, tile, with_ctx):
    h_ref, ml_ref = refs[0], refs[1]
    mc_ref = refs[2] if with_ctx else None
    g_ref, ya_ref, yb_ref, yc_ref, yd_ref, wg_ref, wb_ref, wo_ref, o_ref = refs[2 + with_ctx:]
    j = pl.program_id(1)
    y_refs = (ya_ref, yb_ref, yc_ref, yd_ref)
    for r0, n in _sub_tiles(tile):
        rs = slice(r0, r0 + n)
        shift, scale, gate = _sub_mods(j, r0, tile, ml_ref, mc_ref, (0, 1, 2))
        h = h_ref[0, rs, :]
        a = _norm_modulate(h, g_ref[0] * (1.0 + scale), shift).astype(BF16)
        acc = jnp.zeros((n, D_MODEL), F32)
        for k in range(N_BRANCH):
            logits = _dot(a, wg_ref[0, :, SLOT_W + k * D_MODEL:SLOT_W + (k + 1) * D_MODEL])
            acc = acc + jax.nn.sigmoid(logits) * _dot(y_refs[k][0, rs, :], wb_ref[0, k])
        o_ref[0, rs, :] = h + gate * _dot(acc.astype(BF16), wo_ref[0])


def _merge_call(layer, h, mods, gains, ya, yb, yc, yd, w_in, w_branch, w_out, *, tile, n_tiles, with_ctx):
    tok = pl.BlockSpec((1, tile, D_MODEL), lambda b, j: (b, j, 0))
    br = pl.BlockSpec((1, tile, BRANCH_W), lambda b, j: (b, j, 0))
    mod_specs = _mod_specs(layer, with_ctx)
    return pl.pallas_call(
        functools.partial(_merge_kernel, tile=tile, with_ctx=with_ctx),
        grid=(BATCH, n_tiles),
        in_specs=[
            tok,
            *mod_specs,
            _gain_spec(layer, 0),
            br, br, br, br,
            _resident((1, D_MODEL, SLOT_W + GATE_W), lambda b, j: (layer, 0, 0)),
            _resident((1, N_BRANCH, BRANCH_W, D_MODEL), lambda b, j: (layer, 0, 0, 0)),
            _resident((1, D_MODEL, D_MODEL), lambda b, j: (layer, 0, 0)),
        ],
        out_specs=tok,
        out_shape=jax.ShapeDtypeStruct((BATCH, tile * n_tiles, D_MODEL), F32),
        compiler_params=_params(2),
        name="merge",
    )(h, *([mods] * len(mod_specs)), gains, ya, yb, yc, yd, w_in, w_branch, w_out)


FF_CHUNK = 1024


def _ffn_kernel(*refs, tile, with_ctx, final):
    h_ref, ml_ref = refs[0], refs[1]
    mc_ref = refs[2] if with_ctx else None
    g_ref, w1_ref, w2_ref = refs[2 + with_ctx:5 + with_ctx]
    fg_ref = refs[5 + with_ctx] if final else None
    o_ref = refs[-1]
    j = pl.program_id(1)
    for r0, n in _sub_tiles(tile):
        rs = slice(r0, r0 + n)
        shift, scale, gate = _sub_mods(j, r0, tile, ml_ref, mc_ref, (3, 4, 5))
        h = h_ref[0, rs, :]
        a = _norm_modulate(h, g_ref[0] * (1.0 + scale), shift).astype(BF16)
        acc = jnp.zeros((n, D_MODEL), F32)
        for c in range(D_FF // FF_CHUNK):
            cs = slice(c * FF_CHUNK, (c + 1) * FF_CHUNK)
            t = jnp.maximum(_dot(a, w1_ref[0, :, cs]), 0.0)
            acc = acc + _dot((t * t).astype(BF16), w2_ref[0, cs, :])
        out = h + gate * acc
        if final:
            out = out * lax.rsqrt(jnp.mean(out * out, axis=-1, keepdims=True) + EPS) * fg_ref[...]
        o_ref[0, rs, :] = out


def _ffn_call(layer, h, mods, gains, w1, w2, final_g=None, *, tile, n_tiles, with_ctx):
    tok = pl.BlockSpec((1, tile, D_MODEL), lambda b, j: (b, j, 0))
    final = final_g is not None
    mod_specs = _mod_specs(layer, with_ctx)
    in_specs = [
        tok,
        *mod_specs,
        _gain_spec(layer, 1),
        _resident((1, D_MODEL, D_FF), lambda b, j: (layer, 0, 0)),
        _resident((1, D_FF, D_MODEL), lambda b, j: (layer, 0, 0)),
    ]
    args = [h, *([mods] * len(mod_specs)), gains, w1, w2]
    if final:
        in_specs.append(pl.BlockSpec((1, D_MODEL), lambda b, j: (0, 0)))
        args.append(final_g)
    return pl.pallas_call(
        functools.partial(_ffn_kernel, tile=tile, with_ctx=with_ctx, final=final),
        grid=(BATCH, n_tiles),
        in_specs=in_specs,
        out_specs=tok,
        out_shape=jax.ShapeDtypeStruct((BATCH, tile * n_tiles, D_MODEL), F32),
        compiler_params=_params(2),
        name="ffn_final" if final else "ffn",
    )(*args)


def kernel(x, c, ctx, c_ctx, ada_w, ada_b, norm_g, final_g, w_in, w_branch, w_out, s5_lam_re, s5_lam_im, s5_log_dt, s5_b_re, s5_b_im, s5_c_re, s5_c_im, s5_d, s5_glu_w, s5_glu_b, na_rpb, conv_w, sgu_ln_g, sgu_ln_b, sgu_w, sgu_b, w_ff1, w_ff2):
    h = jnp.concatenate([x, ctx], axis=1)
    cc = jnp.concatenate([c, c_ctx[None], jnp.zeros((MOD_ROWS - BATCH - 1, D_MODEL), F32)], axis=0)
    b_rows = lambda p: jnp.tile(jnp.swapaxes(p, 3, 4).reshape(DEPTH, 2, BRANCH_W, S5_STATE), (1, 1, 1, LANES // S5_STATE))
    c_rows = lambda p: jnp.tile(jnp.swapaxes(p, 3, 4).reshape(DEPTH, 2, S5_LANES, S5_GROUP), (1, 1, 1, LANES // S5_GROUP))
    bre, bim, cre, cim = b_rows(s5_b_re), b_rows(s5_b_im), c_rows(s5_c_re), c_rows(s5_c_im)
    lam_re = s5_lam_re.reshape(DEPTH, 2, 1, S5_LANES)
    lam_im = s5_lam_im.reshape(DEPTH, 2, 1, S5_LANES)
    log_dt = jnp.repeat(s5_log_dt, S5_STATE, axis=-1).reshape(DEPTH, 2, 1, S5_LANES)
    sgu_bias = jnp.repeat(jnp.swapaxes(sgu_b, 1, 2), SGU_GROUP_W, axis=2)
    w_in_b = w_in.astype(BF16)
    w_branch_b, w_out_b = w_branch.astype(BF16), w_out.astype(BF16)
    w_ff1_b, w_ff2_b = w_ff1.astype(BF16), w_ff2.astype(BF16)
    glu_w_b = s5_glu_w.astype(BF16)
    sgu_w_b = jnp.swapaxes(sgu_w, 1, 2).reshape(DEPTH, CHUNK, SGU_GROUPS * CHUNK).astype(BF16)
    gains = norm_g.reshape(DEPTH * 2, 1, D_MODEL)
    vec3 = lambda p: p.reshape(DEPTH, 1, BRANCH_W)

    mods = _mod_call(cc, ada_w, ada_b).reshape(DEPTH * MOD_ROWS, 1, MOD_W)
    a_bar, bb, cm = _s5_prep_call(lam_re, lam_im, log_dt, bre, bim, cre, cim)
    na_bias = _na_bias_call(na_rpb)

    for i in range(DEPTH):
        last = i == DEPTH - 1
        att, loc = _proj_call(i, h, mods, gains, w_in_b)
        yf, yr = _s5_scan_call(i, att, a_bar, bb, cm)
        ya = _s5_glu_call(i, yf, yr, att, vec3(s5_d), glu_w_b, vec3(s5_glu_b))
        yb = _na_call(i, att, na_bias, **(dict(group=4, n_rows=SEQ) if last else dict(group=3, n_rows=TOK)))
        yc, yd = _local_call(i, loc, conv_w, vec3(sgu_ln_g), vec3(sgu_ln_b), sgu_w_b, sgu_bias)
        tiling = dict(tile=TILE_LATENT, n_tiles=SEQ // TILE_LATENT, with_ctx=False) if last else dict(
            tile=TILE, n_tiles=N_TILES, with_ctx=True)
        h = _merge_call(i, h, mods, gains, ya, yb, yc, yd, w_in_b, w_branch_b, w_out_b, **tiling)
        h = _ffn_call(i, h, mods, gains, w_ff1_b, w_ff2_b, final_g[None] if last else None, **tiling)
    return h
```

```python
import functools
import math

import jax
import jax.numpy as jnp
from jax import lax
from jax.experimental import pallas as pl
from jax.experimental.pallas import tpu as pltpu

F32 = jnp.float32
BF16 = jnp.bfloat16

D_MODEL = 1024
BATCH = 8
SEQ = 2048
DEPTH = 4
GRID_W = 64
GRID_H = SEQ // GRID_W
CTX_LEN = 256
TOK = SEQ + CTX_LEN
N_BRANCH = 4
BRANCH_W = D_MODEL // 4
S5_GROUP = 16
S5_GROUPS = BRANCH_W // S5_GROUP
S5_STATE = 64
S5_LANES = S5_GROUPS * S5_STATE
NA_HEAD_DIM = 64
NA_HEADS = BRANCH_W // NA_HEAD_DIM
NA_WIN_ROWS = 8
NA_WIN_COLS = 16
NA_SCALE = NA_HEAD_DIM ** -0.5
CONV_W = 3
CHUNK = 128
SGU_GROUPS = 4
SGU_GROUP_W = BRANCH_W // SGU_GROUPS
D_FF = 4 * D_MODEL
N_MOD = 6
MOD_W = N_MOD * D_MODEL
EPS = 1e-6
NEG_INF = -1e30
N_SLOTS = 9
SLOT_W = N_SLOTS * BRANCH_W
ATT_W = 4 * BRANCH_W
LOC_W = 5 * BRANCH_W
GATE_W = N_BRANCH * D_MODEL

LANES = 128
MOD_ROWS = 16
MOD_CTX_ROW = BATCH
TILE = 1152
N_TILES = TOK // TILE
TILE_LATENT = 1024
SUB = 256
S5_T = 128
S5_ROWS = S5_T * BATCH
S5_STEPS = TOK // S5_T
S5_LAT_STEPS = SEQ // S5_T
S5_SLABS = BRANCH_W // LANES
NA_QB = 4 * GRID_W
NA_WIN_KROWS = 12
NA_WIN_KEYS = NA_WIN_KROWS * GRID_W
NA_LAT_STEPS = SEQ // NA_QB
MIB = 1024 * 1024
VMEM_LIMIT_MIB = 56


def _params(n_axes):
    return pltpu.CompilerParams(
        dimension_semantics=("arbitrary",) * n_axes,
        vmem_limit_bytes=VMEM_LIMIT_MIB * MIB,
    )


def _resident(shape, index_map):
    return pl.BlockSpec(shape, index_map, pipeline_mode=pl.Buffered(1))


def _dot(a, b):
    return jnp.dot(a, b, preferred_element_type=F32)


def _dot_nt(a, b):
    return lax.dot_general(a, b, (((1,), (1,)), ((), ())), preferred_element_type=F32)


def _norm_modulate(h, gain, shift):
    return h * lax.rsqrt(jnp.mean(h * h, axis=-1, keepdims=True) + EPS) * gain + shift


def _sub_tiles(tile):
    ctx_row0 = SEQ % tile or tile
    bounds = list(range(0, ctx_row0, SUB)) + list(range(ctx_row0, tile, SUB)) + [tile]
    return [(lo, hi - lo) for lo, hi in zip(bounds[:-1], bounds[1:])]


def _sub_mods(j, row0, tile, ml_ref, mc_ref, ks):
    vec = lambda ref, k: ref[0, :, k * D_MODEL:(k + 1) * D_MODEL]
    ctx_row0 = SEQ % tile
    if mc_ref is None or row0 < ctx_row0:
        return [vec(ml_ref, k) for k in ks]
    in_ctx = j == TOK // tile - 1
    return [jnp.where(in_ctx, vec(mc_ref, k), vec(ml_ref, k)) for k in ks]


def _mod_specs(layer, with_ctx=True):
    specs = [pl.BlockSpec((1, 1, MOD_W), lambda b, j: (layer * MOD_ROWS + b, 0, 0))]
    if with_ctx:
        specs.append(pl.BlockSpec((1, 1, MOD_W), lambda b, j: (layer * MOD_ROWS + MOD_CTX_ROW, 0, 0)))
    return specs


def _gain_spec(layer, which):
    return pl.BlockSpec((1, 1, D_MODEL), lambda b, j: (2 * layer + which, 0, 0))


MOD_NT = 1536


def _mod_kernel(cc_ref, w_ref, b_ref, o_ref):
    s = jax.nn.silu(cc_ref[...])
    w = w_ref[0]
    s_hi = s.astype(BF16)
    s_lo = (s - s_hi.astype(F32)).astype(BF16)
    w_hi = w.astype(BF16)
    w_lo = (w - w_hi.astype(F32)).astype(BF16)
    o_ref[0] = _dot(s_hi, w_hi) + _dot(s_lo, w_hi) + _dot(s_hi, w_lo) + b_ref[0]


def _mod_call(cc, ada_w, ada_b):
    return pl.pallas_call(
        _mod_kernel,
        grid=(DEPTH, MOD_W // MOD_NT),
        in_specs=[
            pl.BlockSpec((MOD_ROWS, D_MODEL), lambda i, j: (0, 0)),
            pl.BlockSpec((1, D_MODEL, MOD_NT), lambda i, j: (i, 0, j)),
            pl.BlockSpec((1, 1, MOD_NT), lambda i, j: (i, 0, j)),
        ],
        out_specs=pl.BlockSpec((1, MOD_ROWS, MOD_NT), lambda i, j: (i, 0, j)),
        out_shape=jax.ShapeDtypeStruct((DEPTH, MOD_ROWS, MOD_W), F32),
        compiler_params=_params(2),
        name="mod",
    )(cc, ada_w, ada_b.reshape(DEPTH, 1, MOD_W))


def _proj_kernel(h_ref, ml_ref, mc_ref, g_ref, w_ref, att_ref, loc_ref):
    j = pl.program_id(1)
    for r0, n in _sub_tiles(TILE):
        rs = slice(r0, r0 + n)
        shift, scale = _sub_mods(j, r0, TILE, ml_ref, mc_ref, (0, 1))
        a = _norm_modulate(h_ref[0, rs, :], g_ref[0] * (1.0 + scale), shift).astype(BF16)
        att_ref[0, rs, :] = _dot(a, w_ref[0, :, 0:ATT_W]).astype(BF16)
        p = _dot(a, w_ref[0, :, ATT_W:SLOT_W])
        loc_ref[0, rs, 0:BRANCH_W] = p[:, 0:BRANCH_W]
        loc_ref[0, rs, BRANCH_W:2 * BRANCH_W] = p[:, BRANCH_W:2 * BRANCH_W] * p[:, 2 * BRANCH_W:3 * BRANCH_W]
        loc_ref[0, rs, 2 * BRANCH_W:4 * BRANCH_W] = p[:, 3 * BRANCH_W:5 * BRANCH_W]


def _proj_call(layer, h, mods, gains, w_in):
    return pl.pallas_call(
        _proj_kernel,
        grid=(BATCH, N_TILES),
        in_specs=[
            pl.BlockSpec((1, TILE, D_MODEL), lambda b, j: (b, j, 0)),
            *_mod_specs(layer),
            _gain_spec(layer, 0),
            _resident((1, D_MODEL, SLOT_W), lambda b, j: (layer, 0, 0)),
        ],
        out_specs=[
            pl.BlockSpec((1, TILE, ATT_W), lambda b, j: (b, j, 0)),
            pl.BlockSpec((1, TILE, 4 * BRANCH_W), lambda b, j: (b, j, 0)),
        ],
        out_shape=[
            jax.ShapeDtypeStruct((BATCH, TOK, ATT_W), BF16),
            jax.ShapeDtypeStruct((BATCH, TOK, 4 * BRANCH_W), F32),
        ],
        compiler_params=_params(2),
        name="proj",
    )(h, mods, mods, gains, w_in)


def _s5_prep_kernel(lre_ref, lim_ref, ldt_ref, bre_ref, bim_ref, cre_ref, cim_ref, a_ref, bb_ref, cm_ref):
    lam_re = lre_ref[0, 0]
    lam_im = lim_ref[0, 0]
    dt = jnp.exp(ldt_ref[0, 0])
    zr, zi = lam_re * dt, lam_im * dt
    mag = jnp.exp(zr)
    ar, ai = mag * jnp.cos(zi), mag * jnp.sin(zi)
    nr, ni = ar - 1.0, ai
    den = lam_re * lam_re + lam_im * lam_im
    cr = (nr * lam_re + ni * lam_im) / den
    ci = (ni * lam_re - nr * lam_im) / den
    b_row_g = lax.shift_right_logical(lax.broadcasted_iota(jnp.int32, (BRANCH_W, 1), 0), int(math.log2(S5_GROUP)))
    b_col_g = lax.shift_right_logical(lax.broadcasted_iota(jnp.int32, (1, S5_LANES), 1), int(math.log2(S5_STATE)))
    b_own = b_row_g == b_col_g
    expand_b = lambda ref: jnp.where(b_own, jnp.concatenate([ref[0, 0]] * (S5_LANES // LANES), axis=1), 0.0)
    bre, bim = expand_b(bre_ref), expand_b(bim_ref)
    bb_ref[0, 0, :, 0:S5_LANES] = (cr * bre - ci * bim).astype(BF16)
    bb_ref[0, 0, :, S5_LANES:2 * S5_LANES] = (cr * bim + ci * bre).astype(BF16)
    a_ref[0, 0, :, 0:S5_LANES] = jnp.broadcast_to(ar, (BATCH, S5_LANES))
    a_ref[0, 0, :, S5_LANES:2 * S5_LANES] = jnp.broadcast_to(ai, (BATCH, S5_LANES))
    c_row_g = lax.shift_right_logical(lax.broadcasted_iota(jnp.int32, (S5_LANES, 1), 0), int(math.log2(S5_STATE)))
    c_col_g = lax.shift_right_logical(lax.broadcasted_iota(jnp.int32, (1, BRANCH_W), 1), int(math.log2(S5_GROUP)))
    c_own = c_row_g == c_col_g
    expand_c = lambda ref: jnp.where(c_own, jnp.concatenate([ref[0, 0]] * (BRANCH_W // LANES), axis=1), 0.0)
    cm_ref[0, 0, 0:S5_LANES, :] = expand_c(cre_ref).astype(BF16)
    cm_ref[0, 0, S5_LANES:2 * S5_LANES, :] = (-expand_c(cim_ref)).astype(BF16)


def _s5_prep_call(lam_re, lam_im, log_dt, bre, bim, cre, cim):
    vec = pl.BlockSpec((1, 1, 1, S5_LANES), lambda i, d: (i, d, 0, 0))
    bmat = pl.BlockSpec((1, 1, BRANCH_W, LANES), lambda i, d: (i, d, 0, 0))
    cmat = pl.BlockSpec((1, 1, S5_LANES, LANES), lambda i, d: (i, d, 0, 0))
    return pl.pallas_call(
        _s5_prep_kernel,
        grid=(DEPTH, 2),
        in_specs=[vec, vec, vec, bmat, bmat, cmat, cmat],
        out_specs=[
            pl.BlockSpec((1, 1, BATCH, 2 * S5_LANES), lambda i, d: (i, d, 0, 0)),
            pl.BlockSpec((1, 1, BRANCH_W, 2 * S5_LANES), lambda i, d: (i, d, 0, 0)),
            pl.BlockSpec((1, 1, 2 * S5_LANES, BRANCH_W), lambda i, d: (i, d, 0, 0)),
        ],
        out_shape=[
            jax.ShapeDtypeStruct((DEPTH, 2, BATCH, 2 * S5_LANES), F32),
            jax.ShapeDtypeStruct((DEPTH, 2, BRANCH_W, 2 * S5_LANES), BF16),
            jax.ShapeDtypeStruct((DEPTH, 2, 2 * S5_LANES, BRANCH_W), BF16),
        ],
        compiler_params=_params(2),
        name="s5_prep",
    )(lam_re, lam_im, log_dt, bre, bim, cre, cim)


def _s5_fwd_block(i):
    return jnp.where(i < S5_STEPS - S5_LAT_STEPS, S5_LAT_STEPS + i, i - (S5_STEPS - S5_LAT_STEPS))


def _s5_rev_block(i):
    return S5_STEPS - 1 - i


def _to_time_major(u_ref, slab_ref):
    for b in range(BATCH):
        ub = u_ref[b].astype(F32)
        for s in range(S5_SLABS):
            slab_ref[s, pl.ds(b, S5_T, stride=BATCH), :] = ub[:, s * LANES:(s + 1) * LANES]


def _s5_scan_kernel(uf_ref, ur_ref, a_ref, bb_ref, cm_ref, yf_ref, yr_ref, xf_ref, xr_ref, st_ref, usf_ref, usr_ref):
    i = pl.program_id(0)

    @pl.when(i == 0)
    def _():
        st_ref[...] = jnp.zeros_like(st_ref)

    mm_rows = 256
    dirs = ((uf_ref, yf_ref, xf_ref, usf_ref), (ur_ref, yr_ref, xr_ref, usr_ref))
    for d, (u_ref, y_ref, x_ref, us_ref) in enumerate(dirs):
        _to_time_major(u_ref, us_ref)
        for c in range(S5_ROWS // mm_rows):
            rs = slice(c * mm_rows, (c + 1) * mm_rows)
            u = jnp.concatenate([us_ref[s, rs, :] for s in range(S5_SLABS)], axis=1).astype(BF16)
            x_ref[rs, :] = _dot(u, bb_ref[0, d])
    for d, (u_ref, y_ref, x_ref, us_ref) in enumerate(dirs):
        ar = a_ref[0, d, :, 0:S5_LANES]
        ai = a_ref[0, d, :, S5_LANES:2 * S5_LANES]
        hr, hi = st_ref[d, :, 0:S5_LANES], st_ref[d, :, S5_LANES:2 * S5_LANES]
        for t in range(S5_T):
            r = (t if d == 0 else S5_T - 1 - t) * BATCH
            xr = x_ref[r:r + BATCH, 0:S5_LANES]
            xi = x_ref[r:r + BATCH, S5_LANES:2 * S5_LANES]
            hr, hi = ar * hr - ai * hi + xr, ar * hi + ai * hr + xi
            x_ref[r:r + BATCH, 0:S5_LANES] = hr
            x_ref[r:r + BATCH, S5_LANES:2 * S5_LANES] = hi
        st_ref[d, :, 0:S5_LANES] = hr
        st_ref[d, :, S5_LANES:2 * S5_LANES] = hi
        for c in range(S5_ROWS // mm_rows):
            rs = slice(c * mm_rows, (c + 1) * mm_rows)
            y_ref[rs, :] = _dot(x_ref[rs, :].astype(BF16), cm_ref[0, d])


def _s5_scan_call(layer, att, a_bar, bb, cm):
    u_blk = lambda f: pl.BlockSpec((BATCH, S5_T, BRANCH_W), lambda i: (0, f(i), 0))
    y_blk = lambda f: pl.BlockSpec((S5_ROWS, BRANCH_W), lambda i: (f(i), 0))
    return pl.pallas_call(
        _s5_scan_kernel,
        grid=(S5_STEPS,),
        in_specs=[
            u_blk(_s5_fwd_block),
            u_blk(_s5_rev_block),
            pl.BlockSpec((1, 2, BATCH, 2 * S5_LANES), lambda i: (layer, 0, 0, 0)),
            pl.BlockSpec((1, 2, BRANCH_W, 2 * S5_LANES), lambda i: (layer, 0, 0, 0)),
            pl.BlockSpec((1, 2, 2 * S5_LANES, BRANCH_W), lambda i: (layer, 0, 0, 0)),
        ],
        out_specs=[y_blk(_s5_fwd_block), y_blk(_s5_rev_block)],
        out_shape=[jax.ShapeDtypeStruct((TOK * BATCH, BRANCH_W), F32)] * 2,
        scratch_shapes=[
            pltpu.VMEM((S5_ROWS, 2 * S5_LANES), F32),
            pltpu.VMEM((S5_ROWS, 2 * S5_LANES), F32),
            pltpu.VMEM((2, BATCH, 2 * S5_LANES), F32),
            pltpu.VMEM((S5_SLABS, S5_ROWS, LANES), F32),
            pltpu.VMEM((S5_SLABS, S5_ROWS, LANES), F32),
        ],
        compiler_params=_params(1),
        name="s5_scan",
    )(att, att, a_bar, bb, cm)


def _s5_glu_kernel(yf_ref, yr_ref, u_ref, d_ref, w_ref, b_ref, o_ref, ys_ref):
    y = yf_ref[...] + yr_ref[...]
    for s in range(S5_SLABS):
        ys_ref[s] = y[:, s * LANES:(s + 1) * LANES]
    for b in range(BATCH):
        yb = jnp.concatenate([ys_ref[s, pl.ds(b, S5_T, stride=BATCH), :] for s in range(S5_SLABS)], axis=1)
        g = jax.nn.gelu(yb + u_ref[b].astype(F32) * d_ref[0])
        o_ref[b] = (g * jax.nn.sigmoid(_dot(g.astype(BF16), w_ref[0]) + b_ref[0])).astype(BF16)


def _s5_glu_call(layer, yf, yr, att, d_skip, w, b):
    rows = pl.BlockSpec((S5_ROWS, BRANCH_W), lambda i: (i, 0))
    vec = pl.BlockSpec((1, 1, BRANCH_W), lambda i: (layer, 0, 0))
    tok = pl.BlockSpec((BATCH, S5_T, BRANCH_W), lambda i: (0, i, 0))
    return pl.pallas_call(
        _s5_glu_kernel,
        grid=(S5_STEPS,),
        in_specs=[rows, rows, tok, vec, pl.BlockSpec((1, BRANCH_W, BRANCH_W), lambda i: (layer, 0, 0)), vec],
        out_specs=tok,
        out_shape=jax.ShapeDtypeStruct((BATCH, TOK, BRANCH_W), BF16),
        scratch_shapes=[pltpu.VMEM((S5_SLABS, S5_ROWS, LANES), F32)],
        compiler_params=_params(1),
        name="s5_glu",
    )(yf, yr, att, d_skip, w, b)


NA_CASE_FIRST, NA_CASE_INNER, NA_CASE_LAST, NA_CASE_CTX = 0, 1, 2, 3
NA_CASES = 4
N_RPB = (2 * NA_WIN_ROWS - 1) * (2 * NA_WIN_COLS - 1)


def _na_window_start(row0):
    return min(max(row0 - NA_WIN_ROWS // 2, 0), GRID_H - NA_WIN_KROWS)


def _na_bias_kernel(rpb_ref, o_ref):
    layer, h = pl.program_id(0), pl.program_id(1)
    base = (layer * NA_HEADS + h) * N_RPB
    qq = lax.broadcasted_iota(jnp.int32, (GRID_W, 2 * GRID_W), 0)
    lane = lax.broadcasted_iota(jnp.int32, (GRID_W, 2 * GRID_W), 1)
    kk = jnp.bitwise_and(lane, GRID_W - 1)
    second = lane >= GRID_W
    dcm = jnp.clip(kk - qq + NA_WIN_COLS - 1, 0, 2 * NA_WIN_COLS - 2)
    cs = jnp.clip(qq - NA_WIN_COLS // 2, 0, GRID_W - NA_WIN_COLS)
    col_ok = (kk >= cs) & (kk < cs + NA_WIN_COLS)
    neg = jnp.full((GRID_W, 2 * GRID_W), NEG_INF, F32)
    n_dr, n_dc = 2 * NA_WIN_ROWS - 1, 2 * NA_WIN_COLS - 1
    table = []
    for dr in range(n_dr):
        t = neg
        for dc in range(n_dc):
            t = jnp.where(dcm == dc, rpb_ref[base + dr * n_dc + dc], t)
        table.append(jnp.where(col_ok, t, NEG_INF))
    for case, row0 in ((NA_CASE_FIRST, 0), (NA_CASE_INNER, 4), (NA_CASE_LAST, GRID_H - 4)):
        start = _na_window_start(row0)
        for i in range(4):
            r = row0 + i
            row_start = min(max(r - NA_WIN_ROWS // 2, 0), GRID_H - NA_WIN_ROWS)
            for m in range(NA_WIN_KROWS // 2):
                halves = []
                for jj in (2 * m, 2 * m + 1):
                    kr = start + jj
                    valid = row_start <= kr < row_start + NA_WIN_ROWS
                    halves.append(table[kr - r + NA_WIN_ROWS - 1] if valid else neg)
                o_ref[0, case, 0, i * GRID_W:(i + 1) * GRID_W, m * 2 * GRID_W:(m + 1) * 2 * GRID_W] = (
                    jnp.where(second, halves[1], halves[0]))
    o_ref[0, NA_CASE_CTX, 0] = jnp.full((NA_QB, NA_WIN_KEYS), NEG_INF, F32)


def _na_bias_call(rpb):
    return pl.pallas_call(
        _na_bias_kernel,
        grid=(DEPTH, NA_HEADS),
        in_specs=[pl.BlockSpec(memory_space=pltpu.SMEM)],
        out_specs=pl.BlockSpec((1, NA_CASES, 1, NA_QB, NA_WIN_KEYS), lambda i, h: (i, 0, h, 0, 0)),
        out_shape=jax.ShapeDtypeStruct((DEPTH, NA_CASES, NA_HEADS, NA_QB, NA_WIN_KEYS), F32),
        compiler_params=_params(2),
        name="na_bias",
    )(rpb.reshape(-1))


def _na_attend(q, blocks):
    head = lax.shift_right_logical(lax.broadcasted_iota(jnp.int32, (1, BRANCH_W), 1), int(math.log2(NA_HEAD_DIM)))
    acc = jnp.zeros((NA_QB, BRANCH_W), F32)
    for h in range(NA_HEADS):
        in_head = head == h
        qh = jnp.where(in_head, q, jnp.zeros_like(q))
        scores = [_dot_nt(qh, k) if bias is None else _dot_nt(qh, k) + bias(h) for k, _, bias in blocks]
        m = functools.reduce(jnp.maximum, [jnp.max(s, axis=-1, keepdims=True) for s in scores])
        probs = [jnp.exp(s - m) for s in scores]
        norm = sum(jnp.sum(p, axis=-1, keepdims=True) for p in probs)
        o = sum(_dot(p.astype(BF16), v) for p, (_, v, _) in zip(probs, blocks))
        acc = jnp.where(in_head, o * (1.0 / norm), acc)
    return acc.astype(BF16)


def _na_kernel(q_ref, k_ref, v_ref, bias_ref, o_ref, *, group):
    j = pl.program_id(1)
    ctx_block = (k_ref[0, SEQ:TOK, :], v_ref[0, SEQ:TOK, :], None)
    for i in range(group):
        blk = j * group + i
        start = jnp.clip(blk * 4 - NA_WIN_ROWS // 2, 0, GRID_H - NA_WIN_KROWS)
        k0 = pl.multiple_of(start * GRID_W, GRID_W)
        case = jnp.where(blk == 0, NA_CASE_FIRST, jnp.where(
            blk < NA_LAT_STEPS - 1, NA_CASE_INNER, jnp.where(blk == NA_LAT_STEPS - 1, NA_CASE_LAST, NA_CASE_CTX)))
        bias = lambda h, case=case: bias_ref[0, case, h]
        window = (k_ref[0, pl.ds(k0, NA_WIN_KEYS), :], v_ref[0, pl.ds(k0, NA_WIN_KEYS), :], bias)
        rs = slice(i * NA_QB, (i + 1) * NA_QB)
        o_ref[0, rs, :] = _na_attend(q_ref[0, rs, :] * NA_SCALE, [window, ctx_block])


def _na_call(layer, att, bias, *, group, n_rows):
    rows = group * NA_QB
    return pl.pallas_call(
        functools.partial(_na_kernel, group=group),
        grid=(BATCH, n_rows // rows),
        in_specs=[
            pl.BlockSpec((1, rows, BRANCH_W), lambda b, j: (b, j, 1)),
            pl.BlockSpec((1, TOK, BRANCH_W), lambda b, j: (b, 0, 2)),
            pl.BlockSpec((1, TOK, BRANCH_W), lambda b, j: (b, 0, 3)),
            _resident((1, NA_CASES, NA_HEADS, NA_QB, NA_WIN_KEYS), lambda b, j: (layer, 0, 0, 0, 0)),
        ],
        out_specs=pl.BlockSpec((1, rows, BRANCH_W), lambda b, j: (b, j, 0)),
        out_shape=jax.ShapeDtypeStruct((BATCH, n_rows, BRANCH_W), BF16),
        compiler_params=_params(2),
        name="na",
    )(att, att, att, bias)


HALO = 8


def _short_conv(j, tile, cb_ref, z_ref, zp_ref, zn_ref, cw_ref):
    local = lax.broadcasted_iota(jnp.int32, (tile, 1), 0)
    rows = j * tile + local
    z = z_ref[0]
    prev = jnp.where(local == 0, zp_ref[0, HALO - 1:HALO, :], pltpu.roll(z, 1, 0))
    nxt = jnp.where(local == tile - 1, zn_ref[0, 0:1, :], pltpu.roll(z, tile - 1, 0))
    prev = jnp.where((rows == 0) | (rows == SEQ), 0.0, prev)
    nxt = jnp.where((rows == SEQ - 1) | (rows == TOK - 1), 0.0, nxt)
    y = cw_ref[0, 0:1, :] * prev + cw_ref[0, 1:2, :] * z + cw_ref[0, 2:3, :] * nxt
    return (cb_ref[0] * y).astype(BF16)


def _spatial_gating(u, v, lg_ref, lb_ref, sw_ref, sb_ref):
    mu = jnp.mean(v, axis=-1, keepdims=True)
    var = jnp.mean(jnp.square(v - mu), axis=-1, keepdims=True)
    vn = ((v - mu) * lax.rsqrt(var + EPS) * lg_ref[0] + lb_ref[0]).astype(BF16)
    group = lax.shift_right_logical(lax.broadcasted_iota(jnp.int32, (1, BRANCH_W), 1), int(math.log2(SGU_GROUP_W)))
    zero = jnp.zeros((CHUNK, BRANCH_W), BF16)
    out = []
    for c in range(v.shape[0] // CHUNK):
        rs = slice(c * CHUNK, (c + 1) * CHUNK)
        stacked = jnp.concatenate([jnp.where(group == g, vn[rs, :], zero) for g in range(SGU_GROUPS)], axis=0)
        out.append((u[rs, :] * (_dot(sw_ref[0], stacked) + sb_ref[0])).astype(BF16))
    return jnp.concatenate(out, axis=0)


def _merge_kernel(*refs, tile, with_ctx):
    h_ref, ml_ref = refs[0], refs[1]
    mc_ref = refs[2] if with_ctx else None
    (g_ref, ya_ref, yb_ref, cb_ref, z_ref, zp_ref, zn_ref, su_ref, sv_ref, cw_ref, lg_ref, lb_ref, sw_ref, sb_ref,
     wg_ref, wb_ref, wo_ref, o_ref) = refs[2 + with_ctx:]
    j = pl.program_id(1)
    yc = _short_conv(j, tile, cb_ref, z_ref, zp_ref, zn_ref, cw_ref)
    for r0, n in _sub_tiles(tile):
        rs = slice(r0, r0 + n)
        shift, scale, gate = _sub_mods(j, r0, tile, ml_ref, mc_ref, (0, 1, 2))
        h = h_ref[0, rs, :]
        a = _norm_modulate(h, g_ref[0] * (1.0 + scale), shift).astype(BF16)
        yd = _spatial_gating(su_ref[0, rs, :], sv_ref[0, rs, :], lg_ref, lb_ref, sw_ref, sb_ref)
        ys = (ya_ref[0, rs, :], yb_ref[0, rs, :], yc[rs, :], yd)
        acc = jnp.zeros((n, D_MODEL), F32)
        for k in range(N_BRANCH):
            logits = _dot(a, wg_ref[0, :, SLOT_W + k * D_MODEL:SLOT_W + (k + 1) * D_MODEL])
            acc = acc + jax.nn.sigmoid(logits) * _dot(ys[k], wb_ref[0, k])
        o_ref[0, rs, :] = h + gate * _dot(acc.astype(BF16), wo_ref[0])


def _merge_call(layer, h, mods, gains, ya, yb, loc, conv_w, ln_g, ln_b, sgu_w, sgu_b, w_in, w_branch, w_out,
                *, tile, n_tiles, with_ctx):
    tok = pl.BlockSpec((1, tile, D_MODEL), lambda b, j: (b, j, 0))
    br = pl.BlockSpec((1, tile, BRANCH_W), lambda b, j: (b, j, 0))
    col = lambda k: pl.BlockSpec((1, tile, BRANCH_W), lambda b, j: (b, j, k))
    halo_per_tile = tile // HALO
    n_halo = TOK // HALO
    vec = pl.BlockSpec((1, 1, BRANCH_W), lambda b, j: (layer, 0, 0))
    mod_specs = _mod_specs(layer, with_ctx)
    return pl.pallas_call(
        functools.partial(_merge_kernel, tile=tile, with_ctx=with_ctx),
        grid=(BATCH, n_tiles),
        in_specs=[
            tok,
            *mod_specs,
            _gain_spec(layer, 0),
            br, br,
            col(0),
            col(1),
            pl.BlockSpec((1, HALO, BRANCH_W), lambda b, j: (b, jnp.maximum(j * halo_per_tile - 1, 0), 1)),
            pl.BlockSpec((1, HALO, BRANCH_W), lambda b, j: (b, jnp.minimum((j + 1) * halo_per_tile, n_halo - 1), 1)),
            col(2),
            col(3),
            pl.BlockSpec((1, CONV_W, BRANCH_W), lambda b, j: (layer, 0, 0)),
            vec,
            vec,
            pl.BlockSpec((1, CHUNK, SGU_GROUPS * CHUNK), lambda b, j: (layer, 0, 0)),
            pl.BlockSpec((1, CHUNK, BRANCH_W), lambda b, j: (layer, 0, 0)),
            _resident((1, D_MODEL, SLOT_W + GATE_W), lambda b, j: (layer, 0, 0)),
            _resident((1, N_BRANCH, BRANCH_W, D_MODEL), lambda b, j: (layer, 0, 0, 0)),
            _resident((1, D_MODEL, D_MODEL), lambda b, j: (layer, 0, 0)),
        ],
        out_specs=tok,
        out_shape=jax.ShapeDtypeStruct((BATCH, tile * n_tiles, D_MODEL), F32),
        compiler_params=_params(2),
        name="merge",
    )(h, *([mods] * len(mod_specs)), gains, ya, yb, loc, loc, loc, loc, loc, loc,
      conv_w, ln_g, ln_b, sgu_w, sgu_b, w_in, w_branch, w_out)


FF_CHUNK = 1024


def _ffn_kernel(*refs, tile, with_ctx, final):
    h_ref, ml_ref = refs[0], refs[1]
    mc_ref = refs[2] if with_ctx else None
    g_ref, w1_ref, w2_ref = refs[2 + with_ctx:5 + with_ctx]
    fg_ref = refs[5 + with_ctx] if final else None
    o_ref = refs[-1]
    j = pl.program_id(1)
    for r0, n in _sub_tiles(tile):
        rs = slice(r0, r0 + n)
        shift, scale, gate = _sub_mods(j, r0, tile, ml_ref, mc_ref, (3, 4, 5))
        h = h_ref[0, rs, :]
        a = _norm_modulate(h, g_ref[0] * (1.0 + scale), shift).astype(BF16)
        acc = jnp.zeros((n, D_MODEL), F32)
        for c in range(D_FF // FF_CHUNK):
            cs = slice(c * FF_CHUNK, (c + 1) * FF_CHUNK)
            t = jnp.maximum(_dot(a, w1_ref[0, :, cs]), 0.0)
            acc = acc + _dot((t * t).astype(BF16), w2_ref[0, cs, :])
        out = h + gate * acc
        if final:
            out = out * lax.rsqrt(jnp.mean(out * out, axis=-1, keepdims=True) + EPS) * fg_ref[...]
        o_ref[0, rs, :] = out


def _ffn_call(layer, h, mods, gains, w1, w2, final_g=None, *, tile, n_tiles, with_ctx):
    tok = pl.BlockSpec((1, tile, D_MODEL), lambda b, j: (b, j, 0))
    final = final_g is not None
    mod_specs = _mod_specs(layer, with_ctx)
    in_specs = [
        tok,
        *mod_specs,
        _gain_spec(layer, 1),
        _resident((1, D_MODEL, D_FF), lambda b, j: (layer, 0, 0)),
        _resident((1, D_FF, D_MODEL), lambda b, j: (layer, 0, 0)),
    ]
    args = [h, *([mods] * len(mod_specs)), gains, w1, w2]
    if final:
        in_specs.append(pl.BlockSpec((1, D_MODEL), lambda b, j: (0, 0)))
        args.append(final_g)
    return pl.pallas_call(
        functools.partial(_ffn_kernel, tile=tile, with_ctx=with_ctx, final=final),
        grid=(BATCH, n_tiles),
        in_specs=in_specs,
        out_specs=tok,
        out_shape=jax.ShapeDtypeStruct((BATCH, tile * n_tiles, D_MODEL), F32),
        compiler_params=_params(2),
        name="ffn_final" if final else "ffn",
    )(*args)


def kernel(x, c, ctx, c_ctx, ada_w, ada_b, norm_g, final_g, w_in, w_branch, w_out, s5_lam_re, s5_lam_im, s5_log_dt, s5_b_re, s5_b_im, s5_c_re, s5_c_im, s5_d, s5_glu_w, s5_glu_b, na_rpb, conv_w, sgu_ln_g, sgu_ln_b, sgu_w, sgu_b, w_ff1, w_ff2):
    h = jnp.concatenate([x, ctx], axis=1)
    cc = jnp.concatenate([c, c_ctx[None], jnp.zeros((MOD_ROWS - BATCH - 1, D_MODEL), F32)], axis=0)
    b_rows = lambda p: jnp.tile(jnp.swapaxes(p, 3, 4).reshape(DEPTH, 2, BRANCH_W, S5_STATE), (1, 1, 1, LANES // S5_STATE))
    c_rows = lambda p: jnp.tile(jnp.swapaxes(p, 3, 4).reshape(DEPTH, 2, S5_LANES, S5_GROUP), (1, 1, 1, LANES // S5_GROUP))
    bre, bim, cre, cim = b_rows(s5_b_re), b_rows(s5_b_im), c_rows(s5_c_re), c_rows(s5_c_im)
    lam_re = s5_lam_re.reshape(DEPTH, 2, 1, S5_LANES)
    lam_im = s5_lam_im.reshape(DEPTH, 2, 1, S5_LANES)
    log_dt = jnp.repeat(s5_log_dt, S5_STATE, axis=-1).reshape(DEPTH, 2, 1, S5_LANES)
    sgu_bias = jnp.repeat(jnp.swapaxes(sgu_b, 1, 2), SGU_GROUP_W, axis=2)
    w_in_b = w_in.astype(BF16)
    w_branch_b, w_out_b = w_branch.astype(BF16), w_out.astype(BF16)
    w_ff1_b, w_ff2_b = w_ff1.astype(BF16), w_ff2.astype(BF16)
    glu_w_b = s5_glu_w.astype(BF16)
    sgu_w_b = jnp.swapaxes(sgu_w, 1, 2).reshape(DEPTH, CHUNK, SGU_GROUPS * CHUNK).astype(BF16)
    gains = norm_g.reshape(DEPTH * 2, 1, D_MODEL)
    vec3 = lambda p: p.reshape(DEPTH, 1, BRANCH_W)

    mods = _mod_call(cc, ada_w, ada_b).reshape(DEPTH * MOD_ROWS, 1, MOD_W)
    a_bar, bb, cm = _s5_prep_call(lam_re, lam_im, log_dt, bre, bim, cre, cim)
    na_bias = _na_bias_call(na_rpb)

    for i in range(DEPTH):
        last = i == DEPTH - 1
        att, loc = _proj_call(i, h, mods, gains, w_in_b)
        yf, yr = _s5_scan_call(i, att, a_bar, bb, cm)
        ya = _s5_glu_call(i, yf, yr, att, vec3(s5_d), glu_w_b, vec3(s5_glu_b))
        yb = _na_call(i, att, na_bias, **(dict(group=4, n_rows=SEQ) if last else dict(group=3, n_rows=TOK)))
        tiling = dict(tile=TILE_LATENT, n_tiles=SEQ // TILE_LATENT, with_ctx=False) if last else dict(
            tile=TILE, n_tiles=N_TILES, with_ctx=True)
        h = _merge_call(i, h, mods, gains, ya, yb, loc, conv_w, vec3(sgu_ln_g), vec3(sgu_ln_b), sgu_w_b, sgu_bias,
                        w_in_b, w_branch_b, w_out_b, **tiling)
        h = _ffn_call(i, h, mods, gains, w_ff1_b, w_ff2_b, final_g[None] if last else None, **tiling)
    return h
```

```python
import functools
import math

import jax
import jax.numpy as jnp
from jax import lax
from jax.experimental import pallas as pl
from jax.experimental.pallas import tpu as pltpu

F32 = jnp.float32
BF16 = jnp.bfloat16

D_MODEL = 1024
BATCH = 8
SEQ = 2048
DEPTH = 4
GRID_W = 64
GRID_H = SEQ // GRID_W
CTX_LEN = 256
TOK = SEQ + CTX_LEN
N_BRANCH = 4
BRANCH_W = D_MODEL // 4
S5_GROUP = 16
S5_GROUPS = BRANCH_W // S5_GROUP
S5_STATE = 64
S5_LANES = S5_GROUPS * S5_STATE
NA_HEAD_DIM = 64
NA_HEADS = BRANCH_W // NA_HEAD_DIM
NA_WIN_ROWS = 8
NA_WIN_COLS = 16
NA_SCALE = NA_HEAD_DIM ** -0.5
CONV_W = 3
CHUNK = 128
SGU_GROUPS = 4
SGU_GROUP_W = BRANCH_W // SGU_GROUPS
D_FF = 4 * D_MODEL
N_MOD = 6
MOD_W = N_MOD * D_MODEL
EPS = 1e-6
NEG_INF = -1e30
N_SLOTS = 9
SLOT_W = N_SLOTS * BRANCH_W
ATT_W = 4 * BRANCH_W
LOC_W = 5 * BRANCH_W
GATE_W = N_BRANCH * D_MODEL

LANES = 128
MOD_ROWS = 16
MOD_CTX_ROW = BATCH
TILE = 768
N_TILES = TOK // TILE
TILE_LATENT = 1024
SUB = 256
S5_T = 128
S5_ROWS = S5_T * BATCH
S5_STEPS = TOK // S5_T
S5_LAT_STEPS = SEQ // S5_T
S5_SLABS = BRANCH_W // LANES
NA_QB = 4 * GRID_W
NA_WIN_KROWS = 12
NA_WIN_KEYS = NA_WIN_KROWS * GRID_W
NA_LAT_STEPS = SEQ // NA_QB
MIB = 1024 * 1024
VMEM_LIMIT_MIB = 56


def _params(n_axes):
    return pltpu.CompilerParams(
        dimension_semantics=("arbitrary",) * n_axes,
        vmem_limit_bytes=VMEM_LIMIT_MIB * MIB,
    )


def _resident(shape, index_map):
    return pl.BlockSpec(shape, index_map, pipeline_mode=pl.Buffered(1))


def _dot(a, b):
    return jnp.dot(a, b, preferred_element_type=F32)


def _dot_nt(a, b):
    return lax.dot_general(a, b, (((1,), (1,)), ((), ())), preferred_element_type=F32)


def _norm_modulate(h, gain, shift):
    return h * lax.rsqrt(jnp.mean(h * h, axis=-1, keepdims=True) + EPS) * gain + shift


def _sub_tiles(tile):
    ctx_row0 = SEQ % tile or tile
    bounds = list(range(0, ctx_row0, SUB)) + list(range(ctx_row0, tile, SUB)) + [tile]
    return [(lo, hi - lo) for lo, hi in zip(bounds[:-1], bounds[1:])]


def _sub_mods(j, row0, tile, ml_ref, mc_ref, ks):
    vec = lambda ref, k: ref[0, :, k * D_MODEL:(k + 1) * D_MODEL]
    ctx_row0 = SEQ % tile
    if mc_ref is None or row0 < ctx_row0:
        return [vec(ml_ref, k) for k in ks]
    in_ctx = j == TOK // tile - 1
    return [jnp.where(in_ctx, vec(mc_ref, k), vec(ml_ref, k)) for k in ks]


def _mod_specs(layer, with_ctx=True):
    specs = [pl.BlockSpec((1, 1, MOD_W), lambda b, j: (layer * MOD_ROWS + b, 0, 0))]
    if with_ctx:
        specs.append(pl.BlockSpec((1, 1, MOD_W), lambda b, j: (layer * MOD_ROWS + MOD_CTX_ROW, 0, 0)))
    return specs


def _gain_spec(layer, which):
    return pl.BlockSpec((1, 1, D_MODEL), lambda b, j: (2 * layer + which, 0, 0))


MOD_NT = 1536


def _mod_kernel(cc_ref, w_ref, b_ref, o_ref):
    s = jax.nn.silu(cc_ref[...])
    w = w_ref[0]
    s_hi = s.astype(BF16)
    s_lo = (s - s_hi.astype(F32)).astype(BF16)
    w_hi = w.astype(BF16)
    w_lo = (w - w_hi.astype(F32)).astype(BF16)
    o_ref[0] = _dot(s_hi, w_hi) + _dot(s_lo, w_hi) + _dot(s_hi, w_lo) + b_ref[0]


def _mod_call(cc, ada_w, ada_b):
    return pl.pallas_call(
        _mod_kernel,
        grid=(DEPTH, MOD_W // MOD_NT),
        in_specs=[
            pl.BlockSpec((MOD_ROWS, D_MODEL), lambda i, j: (0, 0)),
            pl.BlockSpec((1, D_MODEL, MOD_NT), lambda i, j: (i, 0, j)),
            pl.BlockSpec((1, 1, MOD_NT), lambda i, j: (i, 0, j)),
        ],
        out_specs=pl.BlockSpec((1, MOD_ROWS, MOD_NT), lambda i, j: (i, 0, j)),
        out_shape=jax.ShapeDtypeStruct((DEPTH, MOD_ROWS, MOD_W), F32),
        compiler_params=_params(2),
        name="mod",
    )(cc, ada_w, ada_b.reshape(DEPTH, 1, MOD_W))


def _proj_kernel(h_ref, ml_ref, mc_ref, g_ref, w_ref, att_ref, loc_ref):
    j = pl.program_id(1)
    for r0, n in _sub_tiles(TILE):
        rs = slice(r0, r0 + n)
        shift, scale = _sub_mods(j, r0, TILE, ml_ref, mc_ref, (0, 1))
        a = _norm_modulate(h_ref[0, rs, :], g_ref[0] * (1.0 + scale), shift).astype(BF16)
        att_ref[0, rs, :] = _dot(a, w_ref[0, :, 0:ATT_W]).astype(BF16)
        p = _dot(a, w_ref[0, :, ATT_W:SLOT_W])
        loc_ref[0, rs, 0:BRANCH_W] = p[:, 0:BRANCH_W]
        loc_ref[0, rs, BRANCH_W:2 * BRANCH_W] = p[:, BRANCH_W:2 * BRANCH_W] * p[:, 2 * BRANCH_W:3 * BRANCH_W]
        loc_ref[0, rs, 2 * BRANCH_W:4 * BRANCH_W] = p[:, 3 * BRANCH_W:5 * BRANCH_W]


def _proj_call(layer, h, mods, gains, w_in):
    return pl.pallas_call(
        _proj_kernel,
        grid=(BATCH, N_TILES),
        in_specs=[
            pl.BlockSpec((1, TILE, D_MODEL), lambda b, j: (b, j, 0)),
            *_mod_specs(layer),
            _gain_spec(layer, 0),
            _resident((1, D_MODEL, SLOT_W), lambda b, j: (layer, 0, 0)),
        ],
        out_specs=[
            pl.BlockSpec((1, TILE, ATT_W), lambda b, j: (b, j, 0)),
            pl.BlockSpec((1, TILE, 4 * BRANCH_W), lambda b, j: (b, j, 0)),
        ],
        out_shape=[
            jax.ShapeDtypeStruct((BATCH, TOK, ATT_W), BF16),
            jax.ShapeDtypeStruct((BATCH, TOK, 4 * BRANCH_W), F32),
        ],
        compiler_params=_params(2),
        name="proj",
    )(h, mods, mods, gains, w_in)


def _s5_prep_kernel(lre_ref, lim_ref, ldt_ref, bre_ref, bim_ref, cre_ref, cim_ref, a_ref, bb_ref, cm_ref):
    lam_re = lre_ref[0, 0]
    lam_im = lim_ref[0, 0]
    dt = jnp.exp(ldt_ref[0, 0])
    zr, zi = lam_re * dt, lam_im * dt
    mag = jnp.exp(zr)
    ar, ai = mag * jnp.cos(zi), mag * jnp.sin(zi)
    nr, ni = ar - 1.0, ai
    den = lam_re * lam_re + lam_im * lam_im
    cr = (nr * lam_re + ni * lam_im) / den
    ci = (ni * lam_re - nr * lam_im) / den
    b_row_g = lax.shift_right_logical(lax.broadcasted_iota(jnp.int32, (BRANCH_W, 1), 0), int(math.log2(S5_GROUP)))
    b_col_g = lax.shift_right_logical(lax.broadcasted_iota(jnp.int32, (1, S5_LANES), 1), int(math.log2(S5_STATE)))
    b_own = b_row_g == b_col_g
    expand_b = lambda ref: jnp.where(b_own, jnp.concatenate([ref[0, 0]] * (S5_LANES // LANES), axis=1), 0.0)
    bre, bim = expand_b(bre_ref), expand_b(bim_ref)
    bb_ref[0, 0, :, 0:S5_LANES] = (cr * bre - ci * bim).astype(BF16)
    bb_ref[0, 0, :, S5_LANES:2 * S5_LANES] = (cr * bim + ci * bre).astype(BF16)
    a_ref[0, 0, :, 0:S5_LANES] = jnp.broadcast_to(ar, (BATCH, S5_LANES))
    a_ref[0, 0, :, S5_LANES:2 * S5_LANES] = jnp.broadcast_to(ai, (BATCH, S5_LANES))
    c_row_g = lax.shift_right_logical(lax.broadcasted_iota(jnp.int32, (S5_LANES, 1), 0), int(math.log2(S5_STATE)))
    c_col_g = lax.shift_right_logical(lax.broadcasted_iota(jnp.int32, (1, BRANCH_W), 1), int(math.log2(S5_GROUP)))
    c_own = c_row_g == c_col_g
    expand_c = lambda ref: jnp.where(c_own, jnp.concatenate([ref[0, 0]] * (BRANCH_W // LANES), axis=1), 0.0)
    cm_ref[0, 0, 0:S5_LANES, :] = expand_c(cre_ref).astype(BF16)
    cm_ref[0, 0, S5_LANES:2 * S5_LANES, :] = (-expand_c(cim_ref)).astype(BF16)


def _s5_prep_call(lam_re, lam_im, log_dt, bre, bim, cre, cim):
    vec = pl.BlockSpec((1, 1, 1, S5_LANES), lambda i, d: (i, d, 0, 0))
    bmat = pl.BlockSpec((1, 1, BRANCH_W, LANES), lambda i, d: (i, d, 0, 0))
    cmat = pl.BlockSpec((1, 1, S5_LANES, LANES), lambda i, d: (i, d, 0, 0))
    return pl.pallas_call(
        _s5_prep_kernel,
        grid=(DEPTH, 2),
        in_specs=[vec, vec, vec, bmat, bmat, cmat, cmat],
        out_specs=[
            pl.BlockSpec((1, 1, BATCH, 2 * S5_LANES), lambda i, d: (i, d, 0, 0)),
            pl.BlockSpec((1, 1, BRANCH_W, 2 * S5_LANES), lambda i, d: (i, d, 0, 0)),
            pl.BlockSpec((1, 1, 2 * S5_LANES, BRANCH_W), lambda i, d: (i, d, 0, 0)),
        ],
        out_shape=[
            jax.ShapeDtypeStruct((DEPTH, 2, BATCH, 2 * S5_LANES), F32),
            jax.ShapeDtypeStruct((DEPTH, 2, BRANCH_W, 2 * S5_LANES), BF16),
            jax.ShapeDtypeStruct((DEPTH, 2, 2 * S5_LANES, BRANCH_W), BF16),
        ],
        compiler_params=_params(2),
        name="s5_prep",
    )(lam_re, lam_im, log_dt, bre, bim, cre, cim)


def _s5_fwd_block(i):
    return jnp.where(i < S5_STEPS - S5_LAT_STEPS, S5_LAT_STEPS + i, i - (S5_STEPS - S5_LAT_STEPS))


def _s5_rev_block(i):
    return S5_STEPS - 1 - i


def _to_time_major(u_ref, slab_ref):
    for b in range(BATCH):
        ub = u_ref[b].astype(F32)
        for s in range(S5_SLABS):
            slab_ref[s, pl.ds(b, S5_T, stride=BATCH), :] = ub[:, s * LANES:(s + 1) * LANES]


def _s5_scan_kernel(uf_ref, ur_ref, a_ref, bb_ref, cm_ref, yf_ref, yr_ref, xf_ref, xr_ref, st_ref, usf_ref, usr_ref):
    i = pl.program_id(0)

    @pl.when(i == 0)
    def _():
        st_ref[...] = jnp.zeros_like(st_ref)

    mm_rows = 256
    dirs = ((uf_ref, yf_ref, xf_ref, usf_ref), (ur_ref, yr_ref, xr_ref, usr_ref))
    for d, (u_ref, y_ref, x_ref, us_ref) in enumerate(dirs):
        _to_time_major(u_ref, us_ref)
        for c in range(S5_ROWS // mm_rows):
            rs = slice(c * mm_rows, (c + 1) * mm_rows)
            u = jnp.concatenate([us_ref[s, rs, :] for s in range(S5_SLABS)], axis=1).astype(BF16)
            x_ref[rs, :] = _dot(u, bb_ref[0, d])
    for d, (u_ref, y_ref, x_ref, us_ref) in enumerate(dirs):
        ar = a_ref[0, d, :, 0:S5_LANES]
        ai = a_ref[0, d, :, S5_LANES:2 * S5_LANES]
        hr, hi = st_ref[d, :, 0:S5_LANES], st_ref[d, :, S5_LANES:2 * S5_LANES]
        for t in range(S5_T):
            r = (t if d == 0 else S5_T - 1 - t) * BATCH
            xr = x_ref[r:r + BATCH, 0:S5_LANES]
            xi = x_ref[r:r + BATCH, S5_LANES:2 * S5_LANES]
            hr, hi = ar * hr - ai * hi + xr, ar * hi + ai * hr + xi
            x_ref[r:r + BATCH, 0:S5_LANES] = hr
            x_ref[r:r + BATCH, S5_LANES:2 * S5_LANES] = hi
        st_ref[d, :, 0:S5_LANES] = hr
        st_ref[d, :, S5_LANES:2 * S5_LANES] = hi
        for c in range(S5_ROWS // mm_rows):
            rs = slice(c * mm_rows, (c + 1) * mm_rows)
            y = _dot(x_ref[rs, :].astype(BF16), cm_ref[0, d])
            for s in range(S5_SLABS):
                us_ref[s, rs, :] = y[:, s * LANES:(s + 1) * LANES]
        for b in range(BATCH):
            y_ref[b] = jnp.concatenate(
                [us_ref[s, pl.ds(b, S5_T, stride=BATCH), :] for s in range(S5_SLABS)], axis=1)


def _s5_scan_call(layer, att, a_bar, bb, cm):
    u_blk = lambda f: pl.BlockSpec((BATCH, S5_T, BRANCH_W), lambda i: (0, f(i), 0))
    y_blk = u_blk
    return pl.pallas_call(
        _s5_scan_kernel,
        grid=(S5_STEPS,),
        in_specs=[
            u_blk(_s5_fwd_block),
            u_blk(_s5_rev_block),
            pl.BlockSpec((1, 2, BATCH, 2 * S5_LANES), lambda i: (layer, 0, 0, 0)),
            pl.BlockSpec((1, 2, BRANCH_W, 2 * S5_LANES), lambda i: (layer, 0, 0, 0)),
            pl.BlockSpec((1, 2, 2 * S5_LANES, BRANCH_W), lambda i: (layer, 0, 0, 0)),
        ],
        out_specs=[y_blk(_s5_fwd_block), y_blk(_s5_rev_block)],
        out_shape=[jax.ShapeDtypeStruct((BATCH, TOK, BRANCH_W), F32)] * 2,
        scratch_shapes=[
            pltpu.VMEM((S5_ROWS, 2 * S5_LANES), F32),
            pltpu.VMEM((S5_ROWS, 2 * S5_LANES), F32),
            pltpu.VMEM((2, BATCH, 2 * S5_LANES), F32),
            pltpu.VMEM((S5_SLABS, S5_ROWS, LANES), F32),
            pltpu.VMEM((S5_SLABS, S5_ROWS, LANES), F32),
        ],
        compiler_params=_params(1),
        name="s5_scan",
    )(att, att, a_bar, bb, cm)


NA_CASE_FIRST, NA_CASE_INNER, NA_CASE_LAST, NA_CASE_CTX = 0, 1, 2, 3
NA_CASES = 4
N_RPB = (2 * NA_WIN_ROWS - 1) * (2 * NA_WIN_COLS - 1)


def _na_window_start(row0):
    return min(max(row0 - NA_WIN_ROWS // 2, 0), GRID_H - NA_WIN_KROWS)


def _na_bias_kernel(rpb_ref, o_ref):
    layer, h = pl.program_id(0), pl.program_id(1)
    base = (layer * NA_HEADS + h) * N_RPB
    qq = lax.broadcasted_iota(jnp.int32, (GRID_W, 2 * GRID_W), 0)
    lane = lax.broadcasted_iota(jnp.int32, (GRID_W, 2 * GRID_W), 1)
    kk = jnp.bitwise_and(lane, GRID_W - 1)
    second = lane >= GRID_W
    dcm = jnp.clip(kk - qq + NA_WIN_COLS - 1, 0, 2 * NA_WIN_COLS - 2)
    cs = jnp.clip(qq - NA_WIN_COLS // 2, 0, GRID_W - NA_WIN_COLS)
    col_ok = (kk >= cs) & (kk < cs + NA_WIN_COLS)
    neg = jnp.full((GRID_W, 2 * GRID_W), NEG_INF, F32)
    n_dr, n_dc = 2 * NA_WIN_ROWS - 1, 2 * NA_WIN_COLS - 1
    table = []
    for dr in range(n_dr):
        t = neg
        for dc in range(n_dc):
            t = jnp.where(dcm == dc, rpb_ref[base + dr * n_dc + dc], t)
        table.append(jnp.where(col_ok, t, NEG_INF))
    for case, row0 in ((NA_CASE_FIRST, 0), (NA_CASE_INNER, 4), (NA_CASE_LAST, GRID_H - 4)):
        start = _na_window_start(row0)
        for i in range(4):
            r = row0 + i
            row_start = min(max(r - NA_WIN_ROWS // 2, 0), GRID_H - NA_WIN_ROWS)
            for m in range(NA_WIN_KROWS // 2):
                halves = []
                for jj in (2 * m, 2 * m + 1):
                    kr = start + jj
                    valid = row_start <= kr < row_start + NA_WIN_ROWS
                    halves.append(table[kr - r + NA_WIN_ROWS - 1] if valid else neg)
                o_ref[0, case, 0, i * GRID_W:(i + 1) * GRID_W, m * 2 * GRID_W:(m + 1) * 2 * GRID_W] = (
                    jnp.where(second, halves[1], halves[0]))
    o_ref[0, NA_CASE_CTX, 0] = jnp.full((NA_QB, NA_WIN_KEYS), NEG_INF, F32)


def _na_bias_call(rpb):
    return pl.pallas_call(
        _na_bias_kernel,
        grid=(DEPTH, NA_HEADS),
        in_specs=[pl.BlockSpec(memory_space=pltpu.SMEM)],
        out_specs=pl.BlockSpec((1, NA_CASES, 1, NA_QB, NA_WIN_KEYS), lambda i, h: (i, 0, h, 0, 0)),
        out_shape=jax.ShapeDtypeStruct((DEPTH, NA_CASES, NA_HEADS, NA_QB, NA_WIN_KEYS), F32),
        compiler_params=_params(2),
        name="na_bias",
    )(rpb.reshape(-1))


def _na_attend(q, blocks):
    head = lax.shift_right_logical(lax.broadcasted_iota(jnp.int32, (1, BRANCH_W), 1), int(math.log2(NA_HEAD_DIM)))
    acc = jnp.zeros((NA_QB, BRANCH_W), F32)
    for h in range(NA_HEADS):
        in_head = head == h
        qh = jnp.where(in_head, q, jnp.zeros_like(q))
        scores = [_dot_nt(qh, k) if bias is None else _dot_nt(qh, k) + bias(h) for k, _, bias in blocks]
        m = functools.reduce(jnp.maximum, [jnp.max(s, axis=-1, keepdims=True) for s in scores])
        probs = [jnp.exp(s - m) for s in scores]
        norm = sum(jnp.sum(p, axis=-1, keepdims=True) for p in probs)
        o = sum(_dot(p.astype(BF16), v) for p, (_, v, _) in zip(probs, blocks))
        acc = jnp.where(in_head, o * (1.0 / norm), acc)
    return acc.astype(BF16)


def _na_kernel(q_ref, k_ref, v_ref, bias_ref, o_ref, *, group):
    j = pl.program_id(1)
    ctx_block = (k_ref[0, SEQ:TOK, :], v_ref[0, SEQ:TOK, :], None)
    for i in range(group):
        blk = j * group + i
        start = jnp.clip(blk * 4 - NA_WIN_ROWS // 2, 0, GRID_H - NA_WIN_KROWS)
        k0 = pl.multiple_of(start * GRID_W, GRID_W)
        case = jnp.where(blk == 0, NA_CASE_FIRST, jnp.where(
            blk < NA_LAT_STEPS - 1, NA_CASE_INNER, jnp.where(blk == NA_LAT_STEPS - 1, NA_CASE_LAST, NA_CASE_CTX)))
        bias = lambda h, case=case: bias_ref[0, case, h]
        window = (k_ref[0, pl.ds(k0, NA_WIN_KEYS), :], v_ref[0, pl.ds(k0, NA_WIN_KEYS), :], bias)
        rs = slice(i * NA_QB, (i + 1) * NA_QB)
        o_ref[0, rs, :] = _na_attend(q_ref[0, rs, :] * NA_SCALE, [window, ctx_block])


def _na_call(layer, att, bias, *, group, n_rows):
    rows = group * NA_QB
    return pl.pallas_call(
        functools.partial(_na_kernel, group=group),
        grid=(BATCH, n_rows // rows),
        in_specs=[
            pl.BlockSpec((1, rows, BRANCH_W), lambda b, j: (b, j, 1)),
            pl.BlockSpec((1, TOK, BRANCH_W), lambda b, j: (b, 0, 2)),
            pl.BlockSpec((1, TOK, BRANCH_W), lambda b, j: (b, 0, 3)),
            _resident((1, NA_CASES, NA_HEADS, NA_QB, NA_WIN_KEYS), lambda b, j: (layer, 0, 0, 0, 0)),
        ],
        out_specs=pl.BlockSpec((1, rows, BRANCH_W), lambda b, j: (b, j, 0)),
        out_shape=jax.ShapeDtypeStruct((BATCH, n_rows, BRANCH_W), BF16),
        compiler_params=_params(2),
        name="na",
    )(att, att, att, bias)


HALO = 8


def _short_conv(j, tile, cb_ref, z_ref, zp_ref, zn_ref, cw_ref):
    local = lax.broadcasted_iota(jnp.int32, (tile, 1), 0)
    rows = j * tile + local
    z = z_ref[0]
    prev = jnp.where(local == 0, zp_ref[0, HALO - 1:HALO, :], pltpu.roll(z, 1, 0))
    nxt = jnp.where(local == tile - 1, zn_ref[0, 0:1, :], pltpu.roll(z, tile - 1, 0))
    prev = jnp.where((rows == 0) | (rows == SEQ), 0.0, prev)
    nxt = jnp.where((rows == SEQ - 1) | (rows == TOK - 1), 0.0, nxt)
    y = cw_ref[0, 0:1, :] * prev + cw_ref[0, 1:2, :] * z + cw_ref[0, 2:3, :] * nxt
    return (cb_ref[0] * y).astype(BF16)


def _spatial_gating(u, v, lg_ref, lb_ref, sw_ref, sb_ref):
    mu = jnp.mean(v, axis=-1, keepdims=True)
    var = jnp.mean(jnp.square(v - mu), axis=-1, keepdims=True)
    vn = ((v - mu) * lax.rsqrt(var + EPS) * lg_ref[0] + lb_ref[0]).astype(BF16)
    group = lax.shift_right_logical(lax.broadcasted_iota(jnp.int32, (1, BRANCH_W), 1), int(math.log2(SGU_GROUP_W)))
    zero = jnp.zeros((CHUNK, BRANCH_W), BF16)
    out = []
    for c in range(v.shape[0] // CHUNK):
        rs = slice(c * CHUNK, (c + 1) * CHUNK)
        stacked = jnp.concatenate([jnp.where(group == g, vn[rs, :], zero) for g in range(SGU_GROUPS)], axis=0)
        out.append((u[rs, :] * (_dot(sw_ref[0], stacked) + sb_ref[0])).astype(BF16))
    return jnp.concatenate(out, axis=0)


def _s5_glu(y, w_ref, b_ref):
    g = jax.nn.gelu(y)
    return (g * jax.nn.sigmoid(_dot(g.astype(BF16), w_ref[0]) + b_ref[0])).astype(BF16)


def _merge_kernel(*refs, tile, with_ctx):
    h_ref, ml_ref = refs[0], refs[1]
    mc_ref = refs[2] if with_ctx else None
    (g_ref, yf_ref, yr_ref, u_ref, d_ref, gw_ref, gb_ref, yb_ref,
     cb_ref, z_ref, zp_ref, zn_ref, su_ref, sv_ref, cw_ref, lg_ref, lb_ref, sw_ref, sb_ref,
     wg_ref, wb_ref, wo_ref, o_ref) = refs[2 + with_ctx:]
    j = pl.program_id(1)
    yc = _short_conv(j, tile, cb_ref, z_ref, zp_ref, zn_ref, cw_ref)
    for r0, n in _sub_tiles(tile):
        rs = slice(r0, r0 + n)
        shift, scale, gate = _sub_mods(j, r0, tile, ml_ref, mc_ref, (0, 1, 2))
        h = h_ref[0, rs, :]
        a = _norm_modulate(h, g_ref[0] * (1.0 + scale), shift).astype(BF16)
        yd = _spatial_gating(su_ref[0, rs, :], sv_ref[0, rs, :], lg_ref, lb_ref, sw_ref, sb_ref)
        ya = _s5_glu(yf_ref[0, rs, :] + yr_ref[0, rs, :] + u_ref[0, rs, :].astype(F32) * d_ref[0], gw_ref, gb_ref)
        ys = (ya, yb_ref[0, rs, :], yc[rs, :], yd)
        acc = jnp.zeros((n, D_MODEL), F32)
        for k in range(N_BRANCH):
            logits = _dot(a, wg_ref[0, :, SLOT_W + k * D_MODEL:SLOT_W + (k + 1) * D_MODEL])
            acc = acc + jax.nn.sigmoid(logits) * _dot(ys[k], wb_ref[0, k])
        o_ref[0, rs, :] = h + gate * _dot(acc.astype(BF16), wo_ref[0])


def _merge_call(layer, h, mods, gains, yf, yr, att, d_skip, glu_w, glu_b, yb, loc, conv_w, ln_g, ln_b, sgu_w, sgu_b,
                w_in, w_branch, w_out, *, tile, n_tiles, with_ctx):
    tok = pl.BlockSpec((1, tile, D_MODEL), lambda b, j: (b, j, 0))
    br = pl.BlockSpec((1, tile, BRANCH_W), lambda b, j: (b, j, 0))
    col = lambda k: pl.BlockSpec((1, tile, BRANCH_W), lambda b, j: (b, j, k))
    halo_per_tile = tile // HALO
    n_halo = TOK // HALO
    vec = pl.BlockSpec((1, 1, BRANCH_W), lambda b, j: (layer, 0, 0))
    mod_specs = _mod_specs(layer, with_ctx)
    return pl.pallas_call(
        functools.partial(_merge_kernel, tile=tile, with_ctx=with_ctx),
        grid=(BATCH, n_tiles),
        in_specs=[
            tok,
            *mod_specs,
            _gain_spec(layer, 0),
            br, br, br,
            vec,
            pl.BlockSpec((1, BRANCH_W, BRANCH_W), lambda b, j: (layer, 0, 0)),
            vec,
            br,
            col(0),
            col(1),
            pl.BlockSpec((1, HALO, BRANCH_W), lambda b, j: (b, jnp.maximum(j * halo_per_tile - 1, 0), 1)),
            pl.BlockSpec((1, HALO, BRANCH_W), lambda b, j: (b, jnp.minimum((j + 1) * halo_per_tile, n_halo - 1), 1)),
            col(2),
            col(3),
            pl.BlockSpec((1, CONV_W, BRANCH_W), lambda b, j: (layer, 0, 0)),
            vec,
            vec,
            pl.BlockSpec((1, CHUNK, SGU_GROUPS * CHUNK), lambda b, j: (layer, 0, 0)),
            pl.BlockSpec((1, CHUNK, BRANCH_W), lambda b, j: (layer, 0, 0)),
            _resident((1, D_MODEL, SLOT_W + GATE_W), lambda b, j: (layer, 0, 0)),
            _resident((1, N_BRANCH, BRANCH_W, D_MODEL), lambda b, j: (layer, 0, 0, 0)),
            _resident((1, D_MODEL, D_MODEL), lambda b, j: (layer, 0, 0)),
        ],
        out_specs=tok,
        out_shape=jax.ShapeDtypeStruct((BATCH, tile * n_tiles, D_MODEL), F32),
        compiler_params=_params(2),
        name="merge",
    )(h, *([mods] * len(mod_specs)), gains, yf, yr, att, d_skip, glu_w, glu_b, yb, loc, loc, loc, loc, loc, loc,
      conv_w, ln_g, ln_b, sgu_w, sgu_b, w_in, w_branch, w_out)


FF_CHUNK = 1024


def _ffn_kernel(*refs, tile, with_ctx, final):
    h_ref, ml_ref = refs[0], refs[1]
    mc_ref = refs[2] if with_ctx else None
    g_ref, w1_ref, w2_ref = refs[2 + with_ctx:5 + with_ctx]
    fg_ref = refs[5 + with_ctx] if final else None
    o_ref = refs[-1]
    j = pl.program_id(1)
    for r0, n in _sub_tiles(tile):
        rs = slice(r0, r0 + n)
        shift, scale, gate = _sub_mods(j, r0, tile, ml_ref, mc_ref, (3, 4, 5))
        h = h_ref[0, rs, :]
        a = _norm_modulate(h, g_ref[0] * (1.0 + scale), shift).astype(BF16)
        acc = jnp.zeros((n, D_MODEL), F32)
        for c in range(D_FF // FF_CHUNK):
            cs = slice(c * FF_CHUNK, (c + 1) * FF_CHUNK)
            t = jnp.maximum(_dot(a, w1_ref[0, :, cs]), 0.0)
            acc = acc + _dot((t * t).astype(BF16), w2_ref[0, cs, :])
        out = h + gate * acc
        if final:
            out = out * lax.rsqrt(jnp.mean(out * out, axis=-1, keepdims=True) + EPS) * fg_ref[...]
        o_ref[0, rs, :] = out


def _ffn_call(layer, h, mods, gains, w1, w2, final_g=None, *, tile, n_tiles, with_ctx):
    tok = pl.BlockSpec((1, tile, D_MODEL), lambda b, j: (b, j, 0))
    final = final_g is not None
    mod_specs = _mod_specs(layer, with_ctx)
    in_specs = [
        tok,
        *mod_specs,
        _gain_spec(layer, 1),
        _resident((1, D_MODEL, D_FF), lambda b, j: (layer, 0, 0)),
        _resident((1, D_FF, D_MODEL), lambda b, j: (layer, 0, 0)),
    ]
    args = [h, *([mods] * len(mod_specs)), gains, w1, w2]
    if final:
        in_specs.append(pl.BlockSpec((1, D_MODEL), lambda b, j: (0, 0)))
        args.append(final_g)
    return pl.pallas_call(
        functools.partial(_ffn_kernel, tile=tile, with_ctx=with_ctx, final=final),
        grid=(BATCH, n_tiles),
        in_specs=in_specs,
        out_specs=tok,
        out_shape=jax.ShapeDtypeStruct((BATCH, tile * n_tiles, D_MODEL), F32),
        compiler_params=_params(2),
        name="ffn_final" if final else "ffn",
    )(*args)


def kernel(x, c, ctx, c_ctx, ada_w, ada_b, norm_g, final_g, w_in, w_branch, w_out, s5_lam_re, s5_lam_im, s5_log_dt, s5_b_re, s5_b_im, s5_c_re, s5_c_im, s5_d, s5_glu_w, s5_glu_b, na_rpb, conv_w, sgu_ln_g, sgu_ln_b, sgu_w, sgu_b, w_ff1, w_ff2):
    h = jnp.concatenate([x, ctx], axis=1)
    cc = jnp.concatenate([c, c_ctx[None], jnp.zeros((MOD_ROWS - BATCH - 1, D_MODEL), F32)], axis=0)
    b_rows = lambda p: jnp.tile(jnp.swapaxes(p, 3, 4).reshape(DEPTH, 2, BRANCH_W, S5_STATE), (1, 1, 1, LANES // S5_STATE))
    c_rows = lambda p: jnp.tile(jnp.swapaxes(p, 3, 4).reshape(DEPTH, 2, S5_LANES, S5_GROUP), (1, 1, 1, LANES // S5_GROUP))
    bre, bim, cre, cim = b_rows(s5_b_re), b_rows(s5_b_im), c_rows(s5_c_re), c_rows(s5_c_im)
    lam_re = s5_lam_re.reshape(DEPTH, 2, 1, S5_LANES)
    lam_im = s5_lam_im.reshape(DEPTH, 2, 1, S5_LANES)
    log_dt = jnp.repeat(s5_log_dt, S5_STATE, axis=-1).reshape(DEPTH, 2, 1, S5_LANES)
    sgu_bias = jnp.repeat(jnp.swapaxes(sgu_b, 1, 2), SGU_GROUP_W, axis=2)
    w_in_b = w_in.astype(BF16)
    w_branch_b, w_out_b = w_branch.astype(BF16), w_out.astype(BF16)
    w_ff1_b, w_ff2_b = w_ff1.astype(BF16), w_ff2.astype(BF16)
    glu_w_b = s5_glu_w.astype(BF16)
    sgu_w_b = jnp.swapaxes(sgu_w, 1, 2).reshape(DEPTH, CHUNK, SGU_GROUPS * CHUNK).astype(BF16)
    gains = norm_g.reshape(DEPTH * 2, 1, D_MODEL)
    vec3 = lambda p: p.reshape(DEPTH, 1, BRANCH_W)

    mods = _mod_call(cc, ada_w, ada_b).reshape(DEPTH * MOD_ROWS, 1, MOD_W)
    a_bar, bb, cm = _s5_prep_call(lam_re, lam_im, log_dt, bre, bim, cre, cim)
    na_bias = _na_bias_call(na_rpb)

    for i in range(DEPTH):
        last = i == DEPTH - 1
        att, loc = _proj_call(i, h, mods, gains, w_in_b)
        yf, yr = _s5_scan_call(i, att, a_bar, bb, cm)
        yb = _na_call(i, att, na_bias, **(dict(group=4, n_rows=SEQ) if last else dict(group=3, n_rows=TOK)))
        tiling = dict(tile=TILE_LATENT, n_tiles=SEQ // TILE_LATENT, with_ctx=False) if last else dict(
            tile=TILE, n_tiles=N_TILES, with_ctx=True)
        h = _merge_call(i, h, mods, gains, yf, yr, att, vec3(s5_d), glu_w_b, vec3(s5_glu_b), yb, loc, conv_w,
                        vec3(sgu_ln_g), vec3(sgu_ln_b), sgu_w_b, sgu_bias, w_in_b, w_branch_b, w_out_b, **tiling)
        h = _ffn_call(i, h, mods, gains, w_ff1_b, w_ff2_b, final_g[None] if last else None, **tiling)
    return h
```

```python
import functools
import math

import jax
import jax.numpy as jnp
from jax import lax
from jax.experimental import pallas as pl
from jax.experimental.pallas import tpu as pltpu

F32 = jnp.float32
BF16 = jnp.bfloat16

D_MODEL = 1024
BATCH = 8
SEQ = 2048
DEPTH = 4
GRID_W = 64
GRID_H = SEQ // GRID_W
CTX_LEN = 256
TOK = SEQ + CTX_LEN
N_BRANCH = 4
BRANCH_W = D_MODEL // 4
S5_GROUP = 16
S5_GROUPS = BRANCH_W // S5_GROUP
S5_STATE = 64
S5_LANES = S5_GROUPS * S5_STATE
NA_HEAD_DIM = 64
NA_HEADS = BRANCH_W // NA_HEAD_DIM
NA_WIN_ROWS = 8
NA_WIN_COLS = 16
NA_SCALE = NA_HEAD_DIM ** -0.5
CONV_W = 3
CHUNK = 128
SGU_GROUPS = 4
SGU_GROUP_W = BRANCH_W // SGU_GROUPS
D_FF = 4 * D_MODEL
N_MOD = 6
MOD_W = N_MOD * D_MODEL
EPS = 1e-6
NEG_INF = -1e30
N_SLOTS = 9
SLOT_W = N_SLOTS * BRANCH_W
ATT_W = 4 * BRANCH_W
LOC_W = 5 * BRANCH_W
GATE_W = N_BRANCH * D_MODEL

LANES = 128
MOD_ROWS = 16
MOD_CTX_ROW = BATCH
TILE = 768
N_TILES = TOK // TILE
TILE_LATENT = 1024
SUB = 256
S5_T = 128
S5_ROWS = S5_T * BATCH
S5_STEPS = TOK // S5_T
S5_LAT_STEPS = SEQ // S5_T
S5_SLABS = BRANCH_W // LANES
NA_QB = 4 * GRID_W
NA_WIN_KROWS = 12
NA_WIN_KEYS = NA_WIN_KROWS * GRID_W
NA_LAT_STEPS = SEQ // NA_QB
MIB = 1024 * 1024
VMEM_LIMIT_MIB = 56


def _params(n_axes):
    return pltpu.CompilerParams(
        dimension_semantics=("arbitrary",) * n_axes,
        vmem_limit_bytes=VMEM_LIMIT_MIB * MIB,
    )


def _resident(shape, index_map):
    return pl.BlockSpec(shape, index_map, pipeline_mode=pl.Buffered(1))


def _dot(a, b):
    return jnp.dot(a, b, preferred_element_type=F32)


def _dot_nt(a, b):
    return lax.dot_general(a, b, (((1,), (1,)), ((), ())), preferred_element_type=F32)


def _norm_modulate(h, gain, shift):
    return h * lax.rsqrt(jnp.mean(h * h, axis=-1, keepdims=True) + EPS) * gain + shift


def _sub_tiles(tile):
    ctx_row0 = SEQ % tile or tile
    bounds = list(range(0, ctx_row0, SUB)) + list(range(ctx_row0, tile, SUB)) + [tile]
    return [(lo, hi - lo) for lo, hi in zip(bounds[:-1], bounds[1:])]


def _sub_mods(j, row0, tile, ml_ref, mc_ref, ks):
    vec = lambda ref, k: ref[0, :, k * D_MODEL:(k + 1) * D_MODEL]
    ctx_row0 = SEQ % tile
    if mc_ref is None or row0 < ctx_row0:
        return [vec(ml_ref, k) for k in ks]
    in_ctx = j == TOK // tile - 1
    return [jnp.where(in_ctx, vec(mc_ref, k), vec(ml_ref, k)) for k in ks]


def _stream_specs(first):
    if not first:
        return [pl.BlockSpec((1, TILE, D_MODEL), lambda b, j: (b, j, 0))]
    tail = SEQ % TILE
    return [
        pl.BlockSpec((1, TILE, D_MODEL), lambda b, j: (b, jnp.minimum(j, SEQ // TILE - 1), 0)),
        pl.BlockSpec((1, tail, D_MODEL), lambda b, j: (b, SEQ // tail - 1, 0)),
        pl.BlockSpec((1, CTX_LEN, D_MODEL), lambda b, j: (b, 0, 0)),
    ]


def _stream_rows(src, j, r0, n):
    if len(src) == 1:
        return src[0][0, r0:r0 + n, :]
    x_ref, x_tail_ref, ctx_ref = src
    last = ctx_ref[0] if r0 >= SEQ % TILE else x_tail_ref[0, r0:r0 + n, :]
    return jnp.where(j == N_TILES - 1, last, x_ref[0, r0:r0 + n, :])


def _mod_specs(layer, with_ctx=True):
    specs = [pl.BlockSpec((1, 1, MOD_W), lambda b, j: (layer * MOD_ROWS + b, 0, 0))]
    if with_ctx:
        specs.append(pl.BlockSpec((1, 1, MOD_W), lambda b, j: (layer * MOD_ROWS + MOD_CTX_ROW, 0, 0)))
    return specs


def _gain_spec(layer, which):
    return pl.BlockSpec((1, 1, D_MODEL), lambda b, j: (2 * layer + which, 0, 0))


MOD_NT = 1536


def _mod_kernel(cc_ref, w_ref, b_ref, o_ref):
    s = jax.nn.silu(cc_ref[...])
    w = w_ref[0]
    s_hi = s.astype(BF16)
    s_lo = (s - s_hi.astype(F32)).astype(BF16)
    w_hi = w.astype(BF16)
    w_lo = (w - w_hi.astype(F32)).astype(BF16)
    o_ref[0] = _dot(s_hi, w_hi) + _dot(s_lo, w_hi) + _dot(s_hi, w_lo) + b_ref[0]


def _mod_call(cc, ada_w, ada_b):
    return pl.pallas_call(
        _mod_kernel,
        grid=(DEPTH, MOD_W // MOD_NT),
        in_specs=[
            pl.BlockSpec((MOD_ROWS, D_MODEL), lambda i, j: (0, 0)),
            pl.BlockSpec((1, D_MODEL, MOD_NT), lambda i, j: (i, 0, j)),
            pl.BlockSpec((1, 1, MOD_NT), lambda i, j: (i, 0, j)),
        ],
        out_specs=pl.BlockSpec((1, MOD_ROWS, MOD_NT), lambda i, j: (i, 0, j)),
        out_shape=jax.ShapeDtypeStruct((DEPTH, MOD_ROWS, MOD_W), F32),
        compiler_params=_params(2),
        name="mod",
    )(cc, ada_w, ada_b.reshape(DEPTH, 1, MOD_W))


def _proj_kernel(*refs, first):
    n_src = 3 if first else 1
    src, (ml_ref, mc_ref, g_ref, w_ref, att_ref, loc_ref) = refs[:n_src], refs[n_src:]
    j = pl.program_id(1)
    for r0, n in _sub_tiles(TILE):
        rs = slice(r0, r0 + n)
        shift, scale = _sub_mods(j, r0, TILE, ml_ref, mc_ref, (0, 1))
        a = _norm_modulate(_stream_rows(src, j, r0, n), g_ref[0] * (1.0 + scale), shift).astype(BF16)
        att_ref[0, rs, :] = _dot(a, w_ref[0, :, 0:ATT_W]).astype(BF16)
        p = _dot(a, w_ref[0, :, ATT_W:SLOT_W])
        loc_ref[0, rs, 0:BRANCH_W] = p[:, 0:BRANCH_W]
        loc_ref[0, rs, BRANCH_W:2 * BRANCH_W] = p[:, BRANCH_W:2 * BRANCH_W] * p[:, 2 * BRANCH_W:3 * BRANCH_W]
        loc_ref[0, rs, 2 * BRANCH_W:4 * BRANCH_W] = p[:, 3 * BRANCH_W:5 * BRANCH_W]


def _proj_call(layer, stream, mods, gains, w_in):
    first = len(stream) > 1
    return pl.pallas_call(
        functools.partial(_proj_kernel, first=first),
        grid=(BATCH, N_TILES),
        in_specs=[
            *_stream_specs(first),
            *_mod_specs(layer),
            _gain_spec(layer, 0),
            _resident((1, D_MODEL, SLOT_W), lambda b, j: (layer, 0, 0)),
        ],
        out_specs=[
            pl.BlockSpec((1, TILE, ATT_W), lambda b, j: (b, j, 0)),
            pl.BlockSpec((1, TILE, 4 * BRANCH_W), lambda b, j: (b, j, 0)),
        ],
        out_shape=[
            jax.ShapeDtypeStruct((BATCH, TOK, ATT_W), BF16),
            jax.ShapeDtypeStruct((BATCH, TOK, 4 * BRANCH_W), F32),
        ],
        compiler_params=_params(2),
        name="proj",
    )(*stream, mods, mods, gains, w_in)


def _s5_prep_kernel(lre_ref, lim_ref, ldt_ref, bre_ref, bim_ref, cre_ref, cim_ref, a_ref, bb_ref, cm_ref):
    lam_re = lre_ref[0, 0]
    lam_im = lim_ref[0, 0]
    dt = jnp.exp(ldt_ref[0, 0])
    zr, zi = lam_re * dt, lam_im * dt
    mag = jnp.exp(zr)
    ar, ai = mag * jnp.cos(zi), mag * jnp.sin(zi)
    nr, ni = ar - 1.0, ai
    den = lam_re * lam_re + lam_im * lam_im
    cr = (nr * lam_re + ni * lam_im) / den
    ci = (ni * lam_re - nr * lam_im) / den
    b_row_g = lax.shift_right_logical(lax.broadcasted_iota(jnp.int32, (BRANCH_W, 1), 0), int(math.log2(S5_GROUP)))
    b_col_g = lax.shift_right_logical(lax.broadcasted_iota(jnp.int32, (1, S5_LANES), 1), int(math.log2(S5_STATE)))
    b_own = b_row_g == b_col_g
    expand_b = lambda ref: jnp.where(b_own, jnp.concatenate([ref[0, 0]] * (S5_LANES // LANES), axis=1), 0.0)
    bre, bim = expand_b(bre_ref), expand_b(bim_ref)
    bb_ref[0, 0, :, 0:S5_LANES] = (cr * bre - ci * bim).astype(BF16)
    bb_ref[0, 0, :, S5_LANES:2 * S5_LANES] = (cr * bim + ci * bre).astype(BF16)
    a_ref[0, 0, :, 0:S5_LANES] = jnp.broadcast_to(ar, (BATCH, S5_LANES))
    a_ref[0, 0, :, S5_LANES:2 * S5_LANES] = jnp.broadcast_to(ai, (BATCH, S5_LANES))
    c_row_g = lax.shift_right_logical(lax.broadcasted_iota(jnp.int32, (S5_LANES, 1), 0), int(math.log2(S5_STATE)))
    c_col_g = lax.shift_right_logical(lax.broadcasted_iota(jnp.int32, (1, BRANCH_W), 1), int(math.log2(S5_GROUP)))
    c_own = c_row_g == c_col_g
    expand_c = lambda ref: jnp.where(c_own, jnp.concatenate([ref[0, 0]] * (BRANCH_W // LANES), axis=1), 0.0)
    cm_ref[0, 0, 0:S5_LANES, :] = expand_c(cre_ref).astype(BF16)
    cm_ref[0, 0, S5_LANES:2 * S5_LANES, :] = (-expand_c(cim_ref)).astype(BF16)


def _s5_prep_call(lam_re, lam_im, log_dt, bre, bim, cre, cim):
    vec = pl.BlockSpec((1, 1, 1, S5_LANES), lambda i, d: (i, d, 0, 0))
    bmat = pl.BlockSpec((1, 1, BRANCH_W, LANES), lambda i, d: (i, d, 0, 0))
    cmat = pl.BlockSpec((1, 1, S5_LANES, LANES), lambda i, d: (i, d, 0, 0))
    return pl.pallas_call(
        _s5_prep_kernel,
        grid=(DEPTH, 2),
        in_specs=[vec, vec, vec, bmat, bmat, cmat, cmat],
        out_specs=[
            pl.BlockSpec((1, 1, BATCH, 2 * S5_LANES), lambda i, d: (i, d, 0, 0)),
            pl.BlockSpec((1, 1, BRANCH_W, 2 * S5_LANES), lambda i, d: (i, d, 0, 0)),
            pl.BlockSpec((1, 1, 2 * S5_LANES, BRANCH_W), lambda i, d: (i, d, 0, 0)),
        ],
        out_shape=[
            jax.ShapeDtypeStruct((DEPTH, 2, BATCH, 2 * S5_LANES), F32),
            jax.ShapeDtypeStruct((DEPTH, 2, BRANCH_W, 2 * S5_LANES), BF16),
            jax.ShapeDtypeStruct((DEPTH, 2, 2 * S5_LANES, BRANCH_W), BF16),
        ],
        compiler_params=_params(2),
        name="s5_prep",
    )(lam_re, lam_im, log_dt, bre, bim, cre, cim)


def _s5_fwd_block(i):
    return jnp.where(i < S5_STEPS - S5_LAT_STEPS, S5_LAT_STEPS + i, i - (S5_STEPS - S5_LAT_STEPS))


def _s5_rev_block(i):
    return S5_STEPS - 1 - i


def _to_time_major(u_ref, slab_ref):
    for b in range(BATCH):
        ub = u_ref[b].astype(F32)
        for s in range(S5_SLABS):
            slab_ref[s, pl.ds(b, S5_T, stride=BATCH), :] = ub[:, s * LANES:(s + 1) * LANES]


def _s5_scan_kernel(uf_ref, ur_ref, a_ref, bb_ref, cm_ref, yf_ref, yr_ref, xf_ref, xr_ref, st_ref, usf_ref, usr_ref):
    i = pl.program_id(0)

    @pl.when(i == 0)
    def _():
        st_ref[...] = jnp.zeros_like(st_ref)

    mm_rows = 256
    dirs = ((uf_ref, yf_ref, xf_ref, usf_ref), (ur_ref, yr_ref, xr_ref, usr_ref))
    for d, (u_ref, y_ref, x_ref, us_ref) in enumerate(dirs):
        _to_time_major(u_ref, us_ref)
        for c in range(S5_ROWS // mm_rows):
            rs = slice(c * mm_rows, (c + 1) * mm_rows)
            u = jnp.concatenate([us_ref[s, rs, :] for s in range(S5_SLABS)], axis=1).astype(BF16)
            x_ref[rs, :] = _dot(u, bb_ref[0, d])
    for d, (u_ref, y_ref, x_ref, us_ref) in enumerate(dirs):
        ar = a_ref[0, d, :, 0:S5_LANES]
        ai = a_ref[0, d, :, S5_LANES:2 * S5_LANES]
        hr, hi = st_ref[d, :, 0:S5_LANES], st_ref[d, :, S5_LANES:2 * S5_LANES]
        for t in range(S5_T):
            r = (t if d == 0 else S5_T - 1 - t) * BATCH
            xr = x_ref[r:r + BATCH, 0:S5_LANES]
            xi = x_ref[r:r + BATCH, S5_LANES:2 * S5_LANES]
            hr, hi = ar * hr - ai * hi + xr, ar * hi + ai * hr + xi
            x_ref[r:r + BATCH, 0:S5_LANES] = hr
            x_ref[r:r + BATCH, S5_LANES:2 * S5_LANES] = hi
        st_ref[d, :, 0:S5_LANES] = hr
        st_ref[d, :, S5_LANES:2 * S5_LANES] = hi
        for c in range(S5_ROWS // mm_rows):
            rs = slice(c * mm_rows, (c + 1) * mm_rows)
            y = _dot(x_ref[rs, :].astype(BF16), cm_ref[0, d])
            for s in range(S5_SLABS):
                us_ref[s, rs, :] = y[:, s * LANES:(s + 1) * LANES]
        for b in range(BATCH):
            y_ref[b] = jnp.concatenate(
                [us_ref[s, pl.ds(b, S5_T, stride=BATCH), :] for s in range(S5_SLABS)], axis=1)


def _s5_scan_call(layer, att, a_bar, bb, cm):
    u_blk = lambda f: pl.BlockSpec((BATCH, S5_T, BRANCH_W), lambda i: (0, f(i), 0))
    y_blk = u_blk
    return pl.pallas_call(
        _s5_scan_kernel,
        grid=(S5_STEPS,),
        in_specs=[
            u_blk(_s5_fwd_block),
            u_blk(_s5_rev_block),
            pl.BlockSpec((1, 2, BATCH, 2 * S5_LANES), lambda i: (layer, 0, 0, 0)),
            pl.BlockSpec((1, 2, BRANCH_W, 2 * S5_LANES), lambda i: (layer, 0, 0, 0)),
            pl.BlockSpec((1, 2, 2 * S5_LANES, BRANCH_W), lambda i: (layer, 0, 0, 0)),
        ],
        out_specs=[y_blk(_s5_fwd_block), y_blk(_s5_rev_block)],
        out_shape=[jax.ShapeDtypeStruct((BATCH, TOK, BRANCH_W), F32)] * 2,
        scratch_shapes=[
            pltpu.VMEM((S5_ROWS, 2 * S5_LANES), F32),
            pltpu.VMEM((S5_ROWS, 2 * S5_LANES), F32),
            pltpu.VMEM((2, BATCH, 2 * S5_LANES), F32),
            pltpu.VMEM((S5_SLABS, S5_ROWS, LANES), F32),
            pltpu.VMEM((S5_SLABS, S5_ROWS, LANES), F32),
        ],
        compiler_params=_params(1),
        name="s5_scan",
    )(att, att, a_bar, bb, cm)


NA_CASE_FIRST, NA_CASE_INNER, NA_CASE_LAST, NA_CASE_CTX = 0, 1, 2, 3
NA_CASES = 4
N_RPB = (2 * NA_WIN_ROWS - 1) * (2 * NA_WIN_COLS - 1)


def _na_window_start(row0):
    return min(max(row0 - NA_WIN_ROWS // 2, 0), GRID_H - NA_WIN_KROWS)


def _na_bias_kernel(rpb_ref, o_ref):
    layer, h = pl.program_id(0), pl.program_id(1)
    base = (layer * NA_HEADS + h) * N_RPB
    qq = lax.broadcasted_iota(jnp.int32, (GRID_W, 2 * GRID_W), 0)
    lane = lax.broadcasted_iota(jnp.int32, (GRID_W, 2 * GRID_W), 1)
    kk = jnp.bitwise_and(lane, GRID_W - 1)
    second = lane >= GRID_W
    dcm = jnp.clip(kk - qq + NA_WIN_COLS - 1, 0, 2 * NA_WIN_COLS - 2)
    cs = jnp.clip(qq - NA_WIN_COLS // 2, 0, GRID_W - NA_WIN_COLS)
    col_ok = (kk >= cs) & (kk < cs + NA_WIN_COLS)
    neg = jnp.full((GRID_W, 2 * GRID_W), NEG_INF, F32)
    n_dr, n_dc = 2 * NA_WIN_ROWS - 1, 2 * NA_WIN_COLS - 1
    table = []
    for dr in range(n_dr):
        t = neg
        for dc in range(n_dc):
            t = jnp.where(dcm == dc, rpb_ref[base + dr * n_dc + dc], t)
        table.append(jnp.where(col_ok, t, NEG_INF))
    for case, row0 in ((NA_CASE_FIRST, 0), (NA_CASE_INNER, 4), (NA_CASE_LAST, GRID_H - 4)):
        start = _na_window_start(row0)
        for i in range(4):
            r = row0 + i
            row_start = min(max(r - NA_WIN_ROWS // 2, 0), GRID_H - NA_WIN_ROWS)
            for m in range(NA_WIN_KROWS // 2):
                halves = []
                for jj in (2 * m, 2 * m + 1):
                    kr = start + jj
                    valid = row_start <= kr < row_start + NA_WIN_ROWS
                    halves.append(table[kr - r + NA_WIN_ROWS - 1] if valid else neg)
                o_ref[0, case, 0, i * GRID_W:(i + 1) * GRID_W, m * 2 * GRID_W:(m + 1) * 2 * GRID_W] = (
                    jnp.where(second, halves[1], halves[0]))
    o_ref[0, NA_CASE_CTX, 0] = jnp.full((NA_QB, NA_WIN_KEYS), NEG_INF, F32)


def _na_bias_call(rpb):
    return pl.pallas_call(
        _na_bias_kernel,
        grid=(DEPTH, NA_HEADS),
        in_specs=[pl.BlockSpec(memory_space=pltpu.SMEM)],
        out_specs=pl.BlockSpec((1, NA_CASES, 1, NA_QB, NA_WIN_KEYS), lambda i, h: (i, 0, h, 0, 0)),
        out_shape=jax.ShapeDtypeStruct((DEPTH, NA_CASES, NA_HEADS, NA_QB, NA_WIN_KEYS), F32),
        compiler_params=_params(2),
        name="na_bias",
    )(rpb.reshape(-1))


def _na_attend(q, blocks):
    head = lax.shift_right_logical(lax.broadcasted_iota(jnp.int32, (1, BRANCH_W), 1), int(math.log2(NA_HEAD_DIM)))
    acc = jnp.zeros((NA_QB, BRANCH_W), F32)
    for h in range(NA_HEADS):
        in_head = head == h
        qh = jnp.where(in_head, q, jnp.zeros_like(q))
        scores = [_dot_nt(qh, k) if bias is None else _dot_nt(qh, k) + bias(h) for k, _, bias in blocks]
        m = functools.reduce(jnp.maximum, [jnp.max(s, axis=-1, keepdims=True) for s in scores])
        probs = [jnp.exp(s - m) for s in scores]
        norm = sum(jnp.sum(p, axis=-1, keepdims=True) for p in probs)
        o = sum(_dot(p.astype(BF16), v) for p, (_, v, _) in zip(probs, blocks))
        acc = jnp.where(in_head, o * (1.0 / norm), acc)
    return acc.astype(BF16)


def _na_kernel(q_ref, k_ref, v_ref, bias_ref, o_ref, *, group):
    j = pl.program_id(1)
    ctx_block = (k_ref[0, SEQ:TOK, :], v_ref[0, SEQ:TOK, :], None)
    for i in range(group):
        blk = j * group + i
        start = jnp.clip(blk * 4 - NA_WIN_ROWS // 2, 0, GRID_H - NA_WIN_KROWS)
        k0 = pl.multiple_of(start * GRID_W, GRID_W)
        case = jnp.where(blk == 0, NA_CASE_FIRST, jnp.where(
            blk < NA_LAT_STEPS - 1, NA_CASE_INNER, jnp.where(blk == NA_LAT_STEPS - 1, NA_CASE_LAST, NA_CASE_CTX)))
        bias = lambda h, case=case: bias_ref[0, case, h]
        window = (k_ref[0, pl.ds(k0, NA_WIN_KEYS), :], v_ref[0, pl.ds(k0, NA_WIN_KEYS), :], bias)
        rs = slice(i * NA_QB, (i + 1) * NA_QB)
        o_ref[0, rs, :] = _na_attend(q_ref[0, rs, :] * NA_SCALE, [window, ctx_block])


def _na_call(layer, att, bias, *, group, n_rows):
    rows = group * NA_QB
    return pl.pallas_call(
        functools.partial(_na_kernel, group=group),
        grid=(BATCH, n_rows // rows),
        in_specs=[
            pl.BlockSpec((1, rows, BRANCH_W), lambda b, j: (b, j, 1)),
            pl.BlockSpec((1, TOK, BRANCH_W), lambda b, j: (b, 0, 2)),
            pl.BlockSpec((1, TOK, BRANCH_W), lambda b, j: (b, 0, 3)),
            _resident((1, NA_CASES, NA_HEADS, NA_QB, NA_WIN_KEYS), lambda b, j: (layer, 0, 0, 0, 0)),
        ],
        out_specs=pl.BlockSpec((1, rows, BRANCH_W), lambda b, j: (b, j, 0)),
        out_shape=jax.ShapeDtypeStruct((BATCH, n_rows, BRANCH_W), BF16),
        compiler_params=_params(2),
        name="na",
    )(att, att, att, bias)


HALO = 8


def _short_conv(j, tile, cb_ref, z_ref, zp_ref, zn_ref, cw_ref):
    local = lax.broadcasted_iota(jnp.int32, (tile, 1), 0)
    rows = j * tile + local
    z = z_ref[0]
    prev = jnp.where(local == 0, zp_ref[0, HALO - 1:HALO, :], pltpu.roll(z, 1, 0))
    nxt = jnp.where(local == tile - 1, zn_ref[0, 0:1, :], pltpu.roll(z, tile - 1, 0))
    prev = jnp.where((rows == 0) | (rows == SEQ), 0.0, prev)
    nxt = jnp.where((rows == SEQ - 1) | (rows == TOK - 1), 0.0, nxt)
    y = cw_ref[0, 0:1, :] * prev + cw_ref[0, 1:2, :] * z + cw_ref[0, 2:3, :] * nxt
    return (cb_ref[0] * y).astype(BF16)


def _spatial_gating(u, v, lg_ref, lb_ref, sw_ref, sb_ref):
    mu = jnp.mean(v, axis=-1, keepdims=True)
    var = jnp.mean(jnp.square(v - mu), axis=-1, keepdims=True)
    vn = ((v - mu) * lax.rsqrt(var + EPS) * lg_ref[0] + lb_ref[0]).astype(BF16)
    group = lax.shift_right_logical(lax.broadcasted_iota(jnp.int32, (1, BRANCH_W), 1), int(math.log2(SGU_GROUP_W)))
    zero = jnp.zeros((CHUNK, BRANCH_W), BF16)
    out = []
    for c in range(v.shape[0] // CHUNK):
        rs = slice(c * CHUNK, (c + 1) * CHUNK)
        stacked = jnp.concatenate([jnp.where(group == g, vn[rs, :], zero) for g in range(SGU_GROUPS)], axis=0)
        out.append((u[rs, :] * (_dot(sw_ref[0], stacked) + sb_ref[0])).astype(BF16))
    return jnp.concatenate(out, axis=0)


def _s5_glu(y, w_ref, b_ref):
    g = jax.nn.gelu(y)
    return (g * jax.nn.sigmoid(_dot(g.astype(BF16), w_ref[0]) + b_ref[0])).astype(BF16)


def _merge_kernel(*refs, tile, with_ctx, first):
    n_src = 3 if first else 1
    src, refs = refs[:n_src], refs[n_src - 1:]
    ml_ref = refs[1]
    mc_ref = refs[2] if with_ctx else None
    (g_ref, yf_ref, yr_ref, u_ref, d_ref, gw_ref, gb_ref, yb_ref,
     cb_ref, z_ref, zp_ref, zn_ref, su_ref, sv_ref, cw_ref, lg_ref, lb_ref, sw_ref, sb_ref,
     wg_ref, wb_ref, wo_ref, o_ref) = refs[2 + with_ctx:]
    j = pl.program_id(1)
    yc = _short_conv(j, tile, cb_ref, z_ref, zp_ref, zn_ref, cw_ref)
    for r0, n in _sub_tiles(tile):
        rs = slice(r0, r0 + n)
        shift, scale, gate = _sub_mods(j, r0, tile, ml_ref, mc_ref, (0, 1, 2))
        h = _stream_rows(src, j, r0, n)
        a = _norm_modulate(h, g_ref[0] * (1.0 + scale), shift).astype(BF16)
        yd = _spatial_gating(su_ref[0, rs, :], sv_ref[0, rs, :], lg_ref, lb_ref, sw_ref, sb_ref)
        ya = _s5_glu(yf_ref[0, rs, :] + yr_ref[0, rs, :] + u_ref[0, rs, :].astype(F32) * d_ref[0], gw_ref, gb_ref)
        ys = (ya, yb_ref[0, rs, :], yc[rs, :], yd)
        acc = jnp.zeros((n, D_MODEL), F32)
        for k in range(N_BRANCH):
            logits = _dot(a, wg_ref[0, :, SLOT_W + k * D_MODEL:SLOT_W + (k + 1) * D_MODEL])
            acc = acc + jax.nn.sigmoid(logits) * _dot(ys[k], wb_ref[0, k])
        o_ref[0, rs, :] = h + gate * _dot(acc.astype(BF16), wo_ref[0])


def _merge_call(layer, stream, mods, gains, yf, yr, att, d_skip, glu_w, glu_b, yb, loc, conv_w, ln_g, ln_b, sgu_w, sgu_b,
                w_in, w_branch, w_out, *, tile, n_tiles, with_ctx):
    tok = pl.BlockSpec((1, tile, D_MODEL), lambda b, j: (b, j, 0))
    br = pl.BlockSpec((1, tile, BRANCH_W), lambda b, j: (b, j, 0))
    col = lambda k: pl.BlockSpec((1, tile, BRANCH_W), lambda b, j: (b, j, k))
    halo_per_tile = tile // HALO
    n_halo = TOK // HALO
    vec = pl.BlockSpec((1, 1, BRANCH_W), lambda b, j: (layer, 0, 0))
    mod_specs = _mod_specs(layer, with_ctx)
    first = len(stream) > 1
    return pl.pallas_call(
        functools.partial(_merge_kernel, tile=tile, with_ctx=with_ctx, first=first),
        grid=(BATCH, n_tiles),
        in_specs=[
            *(_stream_specs(True) if first else [tok]),
            *mod_specs,
            _gain_spec(layer, 0),
            br, br, br,
            vec,
            pl.BlockSpec((1, BRANCH_W, BRANCH_W), lambda b, j: (layer, 0, 0)),
            vec,
            br,
            col(0),
            col(1),
            pl.BlockSpec((1, HALO, BRANCH_W), lambda b, j: (b, jnp.maximum(j * halo_per_tile - 1, 0), 1)),
            pl.BlockSpec((1, HALO, BRANCH_W), lambda b, j: (b, jnp.minimum((j + 1) * halo_per_tile, n_halo - 1), 1)),
            col(2),
            col(3),
            pl.BlockSpec((1, CONV_W, BRANCH_W), lambda b, j: (layer, 0, 0)),
            vec,
            vec,
            pl.BlockSpec((1, CHUNK, SGU_GROUPS * CHUNK), lambda b, j: (layer, 0, 0)),
            pl.BlockSpec((1, CHUNK, BRANCH_W), lambda b, j: (layer, 0, 0)),
            _resident((1, D_MODEL, SLOT_W + GATE_W), lambda b, j: (layer, 0, 0)),
            _resident((1, N_BRANCH, BRANCH_W, D_MODEL), lambda b, j: (layer, 0, 0, 0)),
            _resident((1, D_MODEL, D_MODEL), lambda b, j: (layer, 0, 0)),
        ],
        out_specs=tok,
        out_shape=jax.ShapeDtypeStruct((BATCH, tile * n_tiles, D_MODEL), F32),
        compiler_params=_params(2),
        name="merge",
    )(*stream, *([mods] * len(mod_specs)), gains, yf, yr, att, d_skip, glu_w, glu_b, yb, loc, loc, loc, loc, loc, loc,
      conv_w, ln_g, ln_b, sgu_w, sgu_b, w_in, w_branch, w_out)


FF_CHUNK = 1024


def _ffn_kernel(*refs, tile, with_ctx, final):
    h_ref, ml_ref = refs[0], refs[1]
    mc_ref = refs[2] if with_ctx else None
    g_ref, w1_ref, w2_ref = refs[2 + with_ctx:5 + with_ctx]
    fg_ref = refs[5 + with_ctx] if final else None
    o_ref = refs[-1]
    j = pl.program_id(1)
    for r0, n in _sub_tiles(tile):
        rs = slice(r0, r0 + n)
        shift, scale, gate = _sub_mods(j, r0, tile, ml_ref, mc_ref, (3, 4, 5))
        h = h_ref[0, rs, :]
        a = _norm_modulate(h, g_ref[0] * (1.0 + scale), shift).astype(BF16)
        acc = jnp.zeros((n, D_MODEL), F32)
        for c in range(D_FF // FF_CHUNK):
            cs = slice(c * FF_CHUNK, (c + 1) * FF_CHUNK)
            t = jnp.maximum(_dot(a, w1_ref[0, :, cs]), 0.0)
            acc = acc + _dot((t * t).astype(BF16), w2_ref[0, cs, :])
        out = h + gate * acc
        if final:
            out = out * lax.rsqrt(jnp.mean(out * out, axis=-1, keepdims=True) + EPS) * fg_ref[...]
        o_ref[0, rs, :] = out


def _ffn_call(layer, h, mods, gains, w1, w2, final_g=None, *, tile, n_tiles, with_ctx):
    tok = pl.BlockSpec((1, tile, D_MODEL), lambda b, j: (b, j, 0))
    final = final_g is not None
    mod_specs = _mod_specs(layer, with_ctx)
    in_specs = [
        tok,
        *mod_specs,
        _gain_spec(layer, 1),
        _resident((1, D_MODEL, D_FF), lambda b, j: (layer, 0, 0)),
        _resident((1, D_FF, D_MODEL), lambda b, j: (layer, 0, 0)),
    ]
    args = [h, *([mods] * len(mod_specs)), gains, w1, w2]
    if final:
        in_specs.append(pl.BlockSpec((1, D_MODEL), lambda b, j: (0, 0)))
        args.append(final_g)
    return pl.pallas_call(
        functools.partial(_ffn_kernel, tile=tile, with_ctx=with_ctx, final=final),
        grid=(BATCH, n_tiles),
        in_specs=in_specs,
        out_specs=tok,
        out_shape=jax.ShapeDtypeStruct((BATCH, tile * n_tiles, D_MODEL), F32),
        compiler_params=_params(2),
        name="ffn_final" if final else "ffn",
    )(*args)


def kernel(x, c, ctx, c_ctx, ada_w, ada_b, norm_g, final_g, w_in, w_branch, w_out, s5_lam_re, s5_lam_im, s5_log_dt, s5_b_re, s5_b_im, s5_c_re, s5_c_im, s5_d, s5_glu_w, s5_glu_b, na_rpb, conv_w, sgu_ln_g, sgu_ln_b, sgu_w, sgu_b, w_ff1, w_ff2):
    cc = jnp.concatenate([c, c_ctx[None], jnp.zeros((MOD_ROWS - BATCH - 1, D_MODEL), F32)], axis=0)
    b_rows = lambda p: jnp.tile(jnp.swapaxes(p, 3, 4).reshape(DEPTH, 2, BRANCH_W, S5_STATE), (1, 1, 1, LANES // S5_STATE))
    c_rows = lambda p: jnp.tile(jnp.swapaxes(p, 3, 4).reshape(DEPTH, 2, S5_LANES, S5_GROUP), (1, 1, 1, LANES // S5_GROUP))
    bre, bim, cre, cim = b_rows(s5_b_re), b_rows(s5_b_im), c_rows(s5_c_re), c_rows(s5_c_im)
    lam_re = s5_lam_re.reshape(DEPTH, 2, 1, S5_LANES)
    lam_im = s5_lam_im.reshape(DEPTH, 2, 1, S5_LANES)
    log_dt = jnp.repeat(s5_log_dt, S5_STATE, axis=-1).reshape(DEPTH, 2, 1, S5_LANES)
    sgu_bias = jnp.repeat(jnp.swapaxes(sgu_b, 1, 2), SGU_GROUP_W, axis=2)
    w_in_b = w_in.astype(BF16)
    w_branch_b, w_out_b = w_branch.astype(BF16), w_out.astype(BF16)
    w_ff1_b, w_ff2_b = w_ff1.astype(BF16), w_ff2.astype(BF16)
    glu_w_b = s5_glu_w.astype(BF16)
    sgu_w_b = jnp.swapaxes(sgu_w, 1, 2).reshape(DEPTH, CHUNK, SGU_GROUPS * CHUNK).astype(BF16)
    gains = norm_g.reshape(DEPTH * 2, 1, D_MODEL)
    vec3 = lambda p: p.reshape(DEPTH, 1, BRANCH_W)

    mods = _mod_call(cc, ada_w, ada_b).reshape(DEPTH * MOD_ROWS, 1, MOD_W)
    a_bar, bb, cm = _s5_prep_call(lam_re, lam_im, log_dt, bre, bim, cre, cim)
    na_bias = _na_bias_call(na_rpb)

    stream = (x, x, ctx)
    for i in range(DEPTH):
        last = i == DEPTH - 1
        att, loc = _proj_call(i, stream, mods, gains, w_in_b)
        yf, yr = _s5_scan_call(i, att, a_bar, bb, cm)
        yb = _na_call(i, att, na_bias, **(dict(group=4, n_rows=SEQ) if last else dict(group=3, n_rows=TOK)))
        tiling = dict(tile=TILE_LATENT, n_tiles=SEQ // TILE_LATENT, with_ctx=False) if last else dict(
            tile=TILE, n_tiles=N_TILES, with_ctx=True)
        h = _merge_call(i, stream, mods, gains, yf, yr, att, vec3(s5_d), glu_w_b, vec3(s5_glu_b), yb, loc, conv_w,
                        vec3(sgu_ln_g), vec3(sgu_ln_b), sgu_w_b, sgu_bias, w_in_b, w_branch_b, w_out_b, **tiling)
        h = _ffn_call(i, h, mods, gains, w_ff1_b, w_ff2_b, final_g[None] if last else None, **tiling)
        stream = (h,)
    return h
```

```python
import functools
import math

import jax
import jax.numpy as jnp
from jax import lax
from jax.experimental import pallas as pl
from jax.experimental.pallas import tpu as pltpu

F32 = jnp.float32
BF16 = jnp.bfloat16

D_MODEL = 1024
BATCH = 8
SEQ = 2048
DEPTH = 4
GRID_W = 64
GRID_H = SEQ // GRID_W
CTX_LEN = 256
TOK = SEQ + CTX_LEN
N_BRANCH = 4
BRANCH_W = D_MODEL // 4
S5_GROUP = 16
S5_GROUPS = BRANCH_W // S5_GROUP
S5_STATE = 64
S5_LANES = S5_GROUPS * S5_STATE
NA_HEAD_DIM = 64
NA_HEADS = BRANCH_W // NA_HEAD_DIM
NA_WIN_ROWS = 8
NA_WIN_COLS = 16
NA_SCALE = NA_HEAD_DIM ** -0.5
CONV_W = 3
CHUNK = 128
SGU_GROUPS = 4
SGU_GROUP_W = BRANCH_W // SGU_GROUPS
D_FF = 4 * D_MODEL
N_MOD = 6
MOD_W = N_MOD * D_MODEL
EPS = 1e-6
NEG_INF = -1e30
N_SLOTS = 9
SLOT_W = N_SLOTS * BRANCH_W
ATT_W = 4 * BRANCH_W
LOC_W = 5 * BRANCH_W
GATE_W = N_BRANCH * D_MODEL

LANES = 128
MOD_ROWS = 16
MOD_CTX_ROW = BATCH
TILE = 768
N_TILES = TOK // TILE
TILE_LATENT = 1024
SUB = 256
S5_T = 128
S5_ROWS = S5_T * BATCH
S5_STEPS = TOK // S5_T
S5_LAT_STEPS = SEQ // S5_T
S5_SLABS = BRANCH_W // LANES
NA_QB = 4 * GRID_W
NA_WIN_KROWS = 12
NA_WIN_KEYS = NA_WIN_KROWS * GRID_W
NA_LAT_STEPS = SEQ // NA_QB
MIB = 1024 * 1024
VMEM_LIMIT_MIB = 56


def _params(n_axes):
    return pltpu.CompilerParams(
        dimension_semantics=("arbitrary",) * n_axes,
        vmem_limit_bytes=VMEM_LIMIT_MIB * MIB,
    )


def _resident(shape, index_map):
    return pl.BlockSpec(shape, index_map, pipeline_mode=pl.Buffered(1))


def _dot(a, b):
    return jnp.dot(a, b, preferred_element_type=F32)


def _dot_nt(a, b):
    return lax.dot_general(a, b, (((1,), (1,)), ((), ())), preferred_element_type=F32)


def _norm_modulate(h, gain, shift):
    return h * lax.rsqrt(jnp.mean(h * h, axis=-1, keepdims=True) + EPS) * gain + shift


def _sub_tiles(tile):
    ctx_row0 = SEQ % tile or tile
    bounds = list(range(0, ctx_row0, SUB)) + list(range(ctx_row0, tile, SUB)) + [tile]
    return [(lo, hi - lo) for lo, hi in zip(bounds[:-1], bounds[1:])]


def _sub_mods(j, row0, tile, ml_ref, mc_ref, ks):
    vec = lambda ref, k: ref[0, :, k * D_MODEL:(k + 1) * D_MODEL]
    ctx_row0 = SEQ % tile
    if mc_ref is None or row0 < ctx_row0:
        return [vec(ml_ref, k) for k in ks]
    in_ctx = j == TOK // tile - 1
    return [jnp.where(in_ctx, vec(mc_ref, k), vec(ml_ref, k)) for k in ks]


def _stream_specs(split, width=D_MODEL):
    if not split:
        return [pl.BlockSpec((1, TILE, width), lambda b, j: (b, j, 0))]
    tail = SEQ % TILE
    return [
        pl.BlockSpec((1, TILE, width), lambda b, j: (b, jnp.minimum(j, SEQ // TILE - 1), 0)),
        pl.BlockSpec((1, tail, width), lambda b, j: (b, SEQ // tail - 1, 0)),
        pl.BlockSpec((1, CTX_LEN, width), lambda b, j: (b, 0, 0)),
    ]


def _stream_rows(src, j, r0, n):
    if len(src) == 1:
        return src[0][0, r0:r0 + n, :]
    x_ref, x_tail_ref, ctx_ref = src
    last = ctx_ref[0] if r0 >= SEQ % TILE else x_tail_ref[0, r0:r0 + n, :]
    return jnp.where(j == N_TILES - 1, last, x_ref[0, r0:r0 + n, :])


def _mod_specs(layer, with_ctx=True):
    specs = [pl.BlockSpec((1, 1, MOD_W), lambda b, j: (layer * MOD_ROWS + b, 0, 0))]
    if with_ctx:
        specs.append(pl.BlockSpec((1, 1, MOD_W), lambda b, j: (layer * MOD_ROWS + MOD_CTX_ROW, 0, 0)))
    return specs


def _gain_spec(layer, which):
    return pl.BlockSpec((1, 1, D_MODEL), lambda b, j: (2 * layer + which, 0, 0))


MOD_NT = 1536


def _mod_kernel(cc_ref, w_ref, b_ref, o_ref):
    s = jax.nn.silu(cc_ref[...])
    w = w_ref[0].astype(BF16)
    s_hi = s.astype(BF16)
    s_lo = (s - s_hi.astype(F32)).astype(BF16)
    o_ref[0] = _dot(s_hi, w) + _dot(s_lo, w) + b_ref[0]


def _mod_call(cc, ada_w, ada_b):
    return pl.pallas_call(
        _mod_kernel,
        grid=(DEPTH, MOD_W // MOD_NT),
        in_specs=[
            pl.BlockSpec((MOD_ROWS, D_MODEL), lambda i, j: (0, 0)),
            pl.BlockSpec((1, D_MODEL, MOD_NT), lambda i, j: (i, 0, j)),
            pl.BlockSpec((1, 1, MOD_NT), lambda i, j: (i, 0, j)),
        ],
        out_specs=pl.BlockSpec((1, MOD_ROWS, MOD_NT), lambda i, j: (i, 0, j)),
        out_shape=jax.ShapeDtypeStruct((DEPTH, MOD_ROWS, MOD_W), F32),
        compiler_params=_params(2),
        name="mod",
    )(cc, ada_w, ada_b.reshape(DEPTH, 1, MOD_W))


def _proj_kernel(*refs, first):
    n_src = 3 if first else 1
    src, (ml_ref, mc_ref, g_ref, w_ref, att_ref, loc_ref) = refs[:n_src], refs[n_src:]
    j = pl.program_id(1)
    for r0, n in _sub_tiles(TILE):
        rs = slice(r0, r0 + n)
        shift, scale = _sub_mods(j, r0, TILE, ml_ref, mc_ref, (0, 1))
        a = _norm_modulate(_stream_rows(src, j, r0, n), g_ref[0] * (1.0 + scale), shift).astype(BF16)
        att_ref[0, rs, :] = _dot(a, w_ref[0, :, 0:ATT_W]).astype(BF16)
        p = _dot(a, w_ref[0, :, ATT_W:SLOT_W])
        loc_ref[0, rs, 0:BRANCH_W] = p[:, 0:BRANCH_W]
        loc_ref[0, rs, BRANCH_W:2 * BRANCH_W] = p[:, BRANCH_W:2 * BRANCH_W] * p[:, 2 * BRANCH_W:3 * BRANCH_W]
        loc_ref[0, rs, 2 * BRANCH_W:4 * BRANCH_W] = p[:, 3 * BRANCH_W:5 * BRANCH_W]


def _proj_call(layer, stream, mods, gains, w_in):
    first = len(stream) > 1
    return pl.pallas_call(
        functools.partial(_proj_kernel, first=first),
        grid=(BATCH, N_TILES),
        in_specs=[
            *_stream_specs(first),
            *_mod_specs(layer),
            _gain_spec(layer, 0),
            _resident((1, D_MODEL, SLOT_W), lambda b, j: (layer, 0, 0)),
        ],
        out_specs=[
            pl.BlockSpec((1, TILE, ATT_W), lambda b, j: (b, j, 0)),
            pl.BlockSpec((1, TILE, 4 * BRANCH_W), lambda b, j: (b, j, 0)),
        ],
        out_shape=[
            jax.ShapeDtypeStruct((BATCH, TOK, ATT_W), BF16),
            jax.ShapeDtypeStruct((BATCH, TOK, 4 * BRANCH_W), F32),
        ],
        compiler_params=_params(2),
        name="proj",
    )(*stream, mods, mods, gains, w_in)


def _s5_prep_kernel(lre_ref, lim_ref, ldt_ref, bre_ref, bim_ref, cre_ref, cim_ref, a_ref, bb_ref, cm_ref):
    lam_re = lre_ref[0, 0]
    lam_im = lim_ref[0, 0]
    dt = jnp.exp(ldt_ref[0, 0])
    zr, zi = lam_re * dt, lam_im * dt
    mag = jnp.exp(zr)
    ar, ai = mag * jnp.cos(zi), mag * jnp.sin(zi)
    nr, ni = ar - 1.0, ai
    den = lam_re * lam_re + lam_im * lam_im
    cr = (nr * lam_re + ni * lam_im) / den
    ci = (ni * lam_re - nr * lam_im) / den
    b_row_g = lax.shift_right_logical(lax.broadcasted_iota(jnp.int32, (BRANCH_W, 1), 0), int(math.log2(S5_GROUP)))
    b_col_g = lax.shift_right_logical(lax.broadcasted_iota(jnp.int32, (1, S5_LANES), 1), int(math.log2(S5_STATE)))
    b_own = b_row_g == b_col_g
    expand_b = lambda ref: jnp.where(b_own, jnp.concatenate([ref[0, 0]] * (S5_LANES // LANES), axis=1), 0.0)
    bre, bim = expand_b(bre_ref), expand_b(bim_ref)
    bb_ref[0, 0, :, 0:S5_LANES] = (cr * bre - ci * bim).astype(BF16)
    bb_ref[0, 0, :, S5_LANES:2 * S5_LANES] = (cr * bim + ci * bre).astype(BF16)
    a_ref[0, 0, :, 0:S5_LANES] = jnp.broadcast_to(ar, (BATCH, S5_LANES))
    a_ref[0, 0, :, S5_LANES:2 * S5_LANES] = jnp.broadcast_to(ai, (BATCH, S5_LANES))
    c_row_g = lax.shift_right_logical(lax.broadcasted_iota(jnp.int32, (S5_LANES, 1), 0), int(math.log2(S5_STATE)))
    c_col_g = lax.shift_right_logical(lax.broadcasted_iota(jnp.int32, (1, BRANCH_W), 1), int(math.log2(S5_GROUP)))
    c_own = c_row_g == c_col_g
    expand_c = lambda ref: jnp.where(c_own, jnp.concatenate([ref[0, 0]] * (BRANCH_W // LANES), axis=1), 0.0)
    cm_ref[0, 0, 0:S5_LANES, :] = expand_c(cre_ref).astype(BF16)
    cm_ref[0, 0, S5_LANES:2 * S5_LANES, :] = (-expand_c(cim_ref)).astype(BF16)


def _s5_prep_call(lam_re, lam_im, log_dt, bre, bim, cre, cim):
    vec = pl.BlockSpec((1, 1, 1, S5_LANES), lambda i, d: (i, d, 0, 0))
    bmat = pl.BlockSpec((1, 1, BRANCH_W, LANES), lambda i, d: (i, d, 0, 0))
    cmat = pl.BlockSpec((1, 1, S5_LANES, LANES), lambda i, d: (i, d, 0, 0))
    return pl.pallas_call(
        _s5_prep_kernel,
        grid=(DEPTH, 2),
        in_specs=[vec, vec, vec, bmat, bmat, cmat, cmat],
        out_specs=[
            pl.BlockSpec((1, 1, BATCH, 2 * S5_LANES), lambda i, d: (i, d, 0, 0)),
            pl.BlockSpec((1, 1, BRANCH_W, 2 * S5_LANES), lambda i, d: (i, d, 0, 0)),
            pl.BlockSpec((1, 1, 2 * S5_LANES, BRANCH_W), lambda i, d: (i, d, 0, 0)),
        ],
        out_shape=[
            jax.ShapeDtypeStruct((DEPTH, 2, BATCH, 2 * S5_LANES), F32),
            jax.ShapeDtypeStruct((DEPTH, 2, BRANCH_W, 2 * S5_LANES), BF16),
            jax.ShapeDtypeStruct((DEPTH, 2, 2 * S5_LANES, BRANCH_W), BF16),
        ],
        compiler_params=_params(2),
        name="s5_prep",
    )(lam_re, lam_im, log_dt, bre, bim, cre, cim)


def _s5_fwd_block(i):
    return jnp.where(i < S5_STEPS - S5_LAT_STEPS, S5_LAT_STEPS + i, i - (S5_STEPS - S5_LAT_STEPS))


def _s5_rev_block(i):
    return S5_STEPS - 1 - i


def _to_time_major(u_ref, slab_ref):
    for b in range(BATCH):
        ub = u_ref[b].astype(F32)
        for s in range(S5_SLABS):
            slab_ref[s, pl.ds(b, S5_T, stride=BATCH), :] = ub[:, s * LANES:(s + 1) * LANES]


def _s5_scan_kernel(uf_ref, ur_ref, a_ref, bb_ref, cm_ref, yf_ref, yr_ref, xf_ref, xr_ref, st_ref, usf_ref, usr_ref):
    i = pl.program_id(0)

    @pl.when(i == 0)
    def _():
        st_ref[...] = jnp.zeros_like(st_ref)

    mm_rows = 256
    dirs = ((uf_ref, yf_ref, xf_ref, usf_ref), (ur_ref, yr_ref, xr_ref, usr_ref))
    for d, (u_ref, y_ref, x_ref, us_ref) in enumerate(dirs):
        _to_time_major(u_ref, us_ref)
        for c in range(S5_ROWS // mm_rows):
            rs = slice(c * mm_rows, (c + 1) * mm_rows)
            u = jnp.concatenate([us_ref[s, rs, :] for s in range(S5_SLABS)], axis=1).astype(BF16)
            x_ref[rs, :] = _dot(u, bb_ref[0, d])
    for d, (u_ref, y_ref, x_ref, us_ref) in enumerate(dirs):
        ar = a_ref[0, d, :, 0:S5_LANES]
        ai = a_ref[0, d, :, S5_LANES:2 * S5_LANES]
        hr, hi = st_ref[d, :, 0:S5_LANES], st_ref[d, :, S5_LANES:2 * S5_LANES]
        for t in range(S5_T):
            r = (t if d == 0 else S5_T - 1 - t) * BATCH
            xr = x_ref[r:r + BATCH, 0:S5_LANES]
            xi = x_ref[r:r + BATCH, S5_LANES:2 * S5_LANES]
            hr, hi = ar * hr - ai * hi + xr, ar * hi + ai * hr + xi
            x_ref[r:r + BATCH, 0:S5_LANES] = hr
            x_ref[r:r + BATCH, S5_LANES:2 * S5_LANES] = hi
        st_ref[d, :, 0:S5_LANES] = hr
        st_ref[d, :, S5_LANES:2 * S5_LANES] = hi
        for c in range(S5_ROWS // mm_rows):
            rs = slice(c * mm_rows, (c + 1) * mm_rows)
            y = _dot(x_ref[rs, :].astype(BF16), cm_ref[0, d])
            for s in range(S5_SLABS):
                us_ref[s, rs, :] = y[:, s * LANES:(s + 1) * LANES]
        for b in range(BATCH):
            y_ref[b] = jnp.concatenate(
                [us_ref[s, pl.ds(b, S5_T, stride=BATCH), :] for s in range(S5_SLABS)], axis=1)


def _s5_scan_call(layer, att, a_bar, bb, cm):
    u_blk = lambda f: pl.BlockSpec((BATCH, S5_T, BRANCH_W), lambda i: (0, f(i), 0))
    y_blk = u_blk
    return pl.pallas_call(
        _s5_scan_kernel,
        grid=(S5_STEPS,),
        in_specs=[
            u_blk(_s5_fwd_block),
            u_blk(_s5_rev_block),
            pl.BlockSpec((1, 2, BATCH, 2 * S5_LANES), lambda i: (layer, 0, 0, 0)),
            pl.BlockSpec((1, 2, BRANCH_W, 2 * S5_LANES), lambda i: (layer, 0, 0, 0)),
            pl.BlockSpec((1, 2, 2 * S5_LANES, BRANCH_W), lambda i: (layer, 0, 0, 0)),
        ],
        out_specs=[y_blk(_s5_fwd_block), y_blk(_s5_rev_block)],
        out_shape=[jax.ShapeDtypeStruct((BATCH, TOK, BRANCH_W), F32)] * 2,
        scratch_shapes=[
            pltpu.VMEM((S5_ROWS, 2 * S5_LANES), F32),
            pltpu.VMEM((S5_ROWS, 2 * S5_LANES), F32),
            pltpu.VMEM((2, BATCH, 2 * S5_LANES), F32),
            pltpu.VMEM((S5_SLABS, S5_ROWS, LANES), F32),
            pltpu.VMEM((S5_SLABS, S5_ROWS, LANES), F32),
        ],
        compiler_params=_params(1),
        name="s5_scan",
    )(att, att, a_bar, bb, cm)


NA_CASE_FIRST, NA_CASE_INNER, NA_CASE_LAST, NA_CASE_CTX = 0, 1, 2, 3
NA_CASES = 4
N_RPB = (2 * NA_WIN_ROWS - 1) * (2 * NA_WIN_COLS - 1)


def _na_window_start(row0):
    return min(max(row0 - NA_WIN_ROWS // 2, 0), GRID_H - NA_WIN_KROWS)


def _na_bias_kernel(rpb_ref, o_ref):
    layer, h = pl.program_id(0), pl.program_id(1)
    base = (layer * NA_HEADS + h) * N_RPB
    qq = lax.broadcasted_iota(jnp.int32, (GRID_W, 2 * GRID_W), 0)
    lane = lax.broadcasted_iota(jnp.int32, (GRID_W, 2 * GRID_W), 1)
    kk = jnp.bitwise_and(lane, GRID_W - 1)
    second = lane >= GRID_W
    dcm = jnp.clip(kk - qq + NA_WIN_COLS - 1, 0, 2 * NA_WIN_COLS - 2)
    cs = jnp.clip(qq - NA_WIN_COLS // 2, 0, GRID_W - NA_WIN_COLS)
    col_ok = (kk >= cs) & (kk < cs + NA_WIN_COLS)
    neg = jnp.full((GRID_W, 2 * GRID_W), NEG_INF, F32)
    n_dr, n_dc = 2 * NA_WIN_ROWS - 1, 2 * NA_WIN_COLS - 1
    table = []
    for dr in range(n_dr):
        t = neg
        for dc in range(n_dc):
            t = jnp.where(dcm == dc, rpb_ref[base + dr * n_dc + dc], t)
        table.append(jnp.where(col_ok, t, NEG_INF))
    for case, row0 in ((NA_CASE_FIRST, 0), (NA_CASE_INNER, 4), (NA_CASE_LAST, GRID_H - 4)):
        start = _na_window_start(row0)
        for i in range(4):
            r = row0 + i
            row_start = min(max(r - NA_WIN_ROWS // 2, 0), GRID_H - NA_WIN_ROWS)
            for m in range(NA_WIN_KROWS // 2):
                halves = []
                for jj in (2 * m, 2 * m + 1):
                    kr = start + jj
                    valid = row_start <= kr < row_start + NA_WIN_ROWS
                    halves.append(table[kr - r + NA_WIN_ROWS - 1] if valid else neg)
                o_ref[0, case, 0, i * GRID_W:(i + 1) * GRID_W, m * 2 * GRID_W:(m + 1) * 2 * GRID_W] = (
                    jnp.where(second, halves[1], halves[0]))
    o_ref[0, NA_CASE_CTX, 0] = jnp.full((NA_QB, NA_WIN_KEYS), NEG_INF, F32)


def _na_bias_call(rpb):
    return pl.pallas_call(
        _na_bias_kernel,
        grid=(DEPTH, NA_HEADS),
        in_specs=[pl.BlockSpec(memory_space=pltpu.SMEM)],
        out_specs=pl.BlockSpec((1, NA_CASES, 1, NA_QB, NA_WIN_KEYS), lambda i, h: (i, 0, h, 0, 0)),
        out_shape=jax.ShapeDtypeStruct((DEPTH, NA_CASES, NA_HEADS, NA_QB, NA_WIN_KEYS), F32),
        compiler_params=_params(2),
        name="na_bias",
    )(rpb.reshape(-1))


def _na_attend(q, blocks):
    head = lax.shift_right_logical(lax.broadcasted_iota(jnp.int32, (1, BRANCH_W), 1), int(math.log2(NA_HEAD_DIM)))
    acc = jnp.zeros((NA_QB, BRANCH_W), F32)
    for h in range(NA_HEADS):
        in_head = head == h
        qh = jnp.where(in_head, q, jnp.zeros_like(q))
        scores = [_dot_nt(qh, k) if bias is None else _dot_nt(qh, k) + bias(h) for k, _, bias in blocks]
        m = functools.reduce(jnp.maximum, [jnp.max(s, axis=-1, keepdims=True) for s in scores])
        probs = [jnp.exp(s - m) for s in scores]
        norm = sum(jnp.sum(p, axis=-1, keepdims=True) for p in probs)
        o = sum(_dot(p.astype(BF16), v) for p, (_, v, _) in zip(probs, blocks))
        acc = jnp.where(in_head, o * (1.0 / norm), acc)
    return acc.astype(BF16)


def _na_kernel(q_ref, k_ref, v_ref, bias_ref, o_ref, *, group):
    j = pl.program_id(1)
    ctx_block = (k_ref[0, SEQ:TOK, :], v_ref[0, SEQ:TOK, :], None)
    for i in range(group):
        blk = j * group + i
        start = jnp.clip(blk * 4 - NA_WIN_ROWS // 2, 0, GRID_H - NA_WIN_KROWS)
        k0 = pl.multiple_of(start * GRID_W, GRID_W)
        case = jnp.where(blk == 0, NA_CASE_FIRST, jnp.where(
            blk < NA_LAT_STEPS - 1, NA_CASE_INNER, jnp.where(blk == NA_LAT_STEPS - 1, NA_CASE_LAST, NA_CASE_CTX)))
        bias = lambda h, case=case: bias_ref[0, case, h]
        window = (k_ref[0, pl.ds(k0, NA_WIN_KEYS), :], v_ref[0, pl.ds(k0, NA_WIN_KEYS), :], bias)
        rs = slice(i * NA_QB, (i + 1) * NA_QB)
        o_ref[0, rs, :] = _na_attend(q_ref[0, rs, :] * NA_SCALE, [window, ctx_block])


def _na_call(layer, att, bias, *, group, n_rows):
    rows = group * NA_QB
    return pl.pallas_call(
        functools.partial(_na_kernel, group=group),
        grid=(BATCH, n_rows // rows),
        in_specs=[
            pl.BlockSpec((1, rows, BRANCH_W), lambda b, j: (b, j, 1)),
            pl.BlockSpec((1, TOK, BRANCH_W), lambda b, j: (b, 0, 2)),
            pl.BlockSpec((1, TOK, BRANCH_W), lambda b, j: (b, 0, 3)),
            _resident((1, NA_CASES, NA_HEADS, NA_QB, NA_WIN_KEYS), lambda b, j: (layer, 0, 0, 0, 0)),
        ],
        out_specs=pl.BlockSpec((1, rows, BRANCH_W), lambda b, j: (b, j, 0)),
        out_shape=jax.ShapeDtypeStruct((BATCH, n_rows, BRANCH_W), BF16),
        compiler_params=_params(2),
        name="na",
    )(att, att, att, bias)


def _na_ctx_kernel(q_ref, k_ref, v_ref, o_ref):
    o_ref[0] = _na_attend(q_ref[0] * NA_SCALE, [(k_ref[0], v_ref[0], None)])


def _na_ctx_call(att):
    ctx_blk = lambda col: pl.BlockSpec((1, CTX_LEN, BRANCH_W), lambda b: (b, SEQ // CTX_LEN, col))
    return pl.pallas_call(
        _na_ctx_kernel,
        grid=(BATCH,),
        in_specs=[ctx_blk(1), ctx_blk(2), ctx_blk(3)],
        out_specs=pl.BlockSpec((1, CTX_LEN, BRANCH_W), lambda b: (b, 0, 0)),
        out_shape=jax.ShapeDtypeStruct((BATCH, CTX_LEN, BRANCH_W), BF16),
        compiler_params=_params(1),
        name="na_ctx",
    )(att, att, att)


HALO = 8


def _short_conv(j, tile, cb_ref, z_ref, zp_ref, zn_ref, cw_ref):
    local = lax.broadcasted_iota(jnp.int32, (tile, 1), 0)
    rows = j * tile + local
    z = z_ref[0]
    prev = jnp.where(local == 0, zp_ref[0, HALO - 1:HALO, :], pltpu.roll(z, 1, 0))
    nxt = jnp.where(local == tile - 1, zn_ref[0, 0:1, :], pltpu.roll(z, tile - 1, 0))
    prev = jnp.where((rows == 0) | (rows == SEQ), 0.0, prev)
    nxt = jnp.where((rows == SEQ - 1) | (rows == TOK - 1), 0.0, nxt)
    y = cw_ref[0, 0:1, :] * prev + cw_ref[0, 1:2, :] * z + cw_ref[0, 2:3, :] * nxt
    return (cb_ref[0] * y).astype(BF16)


def _spatial_gating(u, v, lg_ref, lb_ref, sw_ref, sb_ref):
    mu = jnp.mean(v, axis=-1, keepdims=True)
    var = jnp.mean(jnp.square(v - mu), axis=-1, keepdims=True)
    vn = ((v - mu) * lax.rsqrt(var + EPS) * lg_ref[0] + lb_ref[0]).astype(BF16)
    group = lax.shift_right_logical(lax.broadcasted_iota(jnp.int32, (1, BRANCH_W), 1), int(math.log2(SGU_GROUP_W)))
    zero = jnp.zeros((CHUNK, BRANCH_W), BF16)
    out = []
    for c in range(v.shape[0] // CHUNK):
        rs = slice(c * CHUNK, (c + 1) * CHUNK)
        stacked = jnp.concatenate([jnp.where(group == g, vn[rs, :], zero) for g in range(SGU_GROUPS)], axis=0)
        out.append((u[rs, :] * (_dot(sw_ref[0], stacked) + sb_ref[0])).astype(BF16))
    return jnp.concatenate(out, axis=0)


def _s5_glu(y, w_ref, b_ref):
    g = jax.nn.gelu(y)
    return (g * jax.nn.sigmoid(_dot(g.astype(BF16), w_ref[0]) + b_ref[0])).astype(BF16)


def _merge_kernel(*refs, tile, with_ctx, first):
    n_src = 3 if first else 1
    src, refs = refs[:n_src], refs[n_src - 1:]
    ml_ref = refs[1]
    mc_ref = refs[2] if with_ctx else None
    refs = refs[2 + with_ctx:]
    g_ref, yf_ref, yr_ref, u_ref, d_ref, gw_ref, gb_ref = refs[:7]
    n_yb = 3 if with_ctx else 1
    yb_src = refs[7:7 + n_yb]
    (cb_ref, z_ref, zp_ref, zn_ref, su_ref, sv_ref, cw_ref, lg_ref, lb_ref, sw_ref, sb_ref,
     wg_ref, wb_ref, wo_ref, o_ref) = refs[7 + n_yb:]
    j = pl.program_id(1)
    yc = _short_conv(j, tile, cb_ref, z_ref, zp_ref, zn_ref, cw_ref)
    for r0, n in _sub_tiles(tile):
        rs = slice(r0, r0 + n)
        shift, scale, gate = _sub_mods(j, r0, tile, ml_ref, mc_ref, (0, 1, 2))
        h = _stream_rows(src, j, r0, n)
        a = _norm_modulate(h, g_ref[0] * (1.0 + scale), shift).astype(BF16)
        yd = _spatial_gating(su_ref[0, rs, :], sv_ref[0, rs, :], lg_ref, lb_ref, sw_ref, sb_ref)
        ya = _s5_glu(yf_ref[0, rs, :] + yr_ref[0, rs, :] + u_ref[0, rs, :].astype(F32) * d_ref[0], gw_ref, gb_ref)
        ys = (ya, _stream_rows(yb_src, j, r0, n), yc[rs, :], yd)
        acc = jnp.zeros((n, D_MODEL), F32)
        for k in range(N_BRANCH):
            logits = _dot(a, wg_ref[0, :, SLOT_W + k * D_MODEL:SLOT_W + (k + 1) * D_MODEL])
            acc = acc + jax.nn.sigmoid(logits) * _dot(ys[k], wb_ref[0, k])
        o_ref[0, rs, :] = h + gate * _dot(acc.astype(BF16), wo_ref[0])


def _merge_call(layer, stream, mods, gains, yf, yr, att, d_skip, glu_w, glu_b, yb, loc, conv_w, ln_g, ln_b, sgu_w, sgu_b,
                w_in, w_branch, w_out, *, tile, n_tiles, with_ctx):
    tok = pl.BlockSpec((1, tile, D_MODEL), lambda b, j: (b, j, 0))
    br = pl.BlockSpec((1, tile, BRANCH_W), lambda b, j: (b, j, 0))
    col = lambda k: pl.BlockSpec((1, tile, BRANCH_W), lambda b, j: (b, j, k))
    halo_per_tile = tile // HALO
    n_halo = TOK // HALO
    vec = pl.BlockSpec((1, 1, BRANCH_W), lambda b, j: (layer, 0, 0))
    mod_specs = _mod_specs(layer, with_ctx)
    first = len(stream) > 1
    return pl.pallas_call(
        functools.partial(_merge_kernel, tile=tile, with_ctx=with_ctx, first=first),
        grid=(BATCH, n_tiles),
        in_specs=[
            *(_stream_specs(True) if first else [tok]),
            *mod_specs,
            _gain_spec(layer, 0),
            br, br, br,
            vec,
            pl.BlockSpec((1, BRANCH_W, BRANCH_W), lambda b, j: (layer, 0, 0)),
            vec,
            *(_stream_specs(True, BRANCH_W) if with_ctx else [br]),
            col(0),
            col(1),
            pl.BlockSpec((1, HALO, BRANCH_W), lambda b, j: (b, jnp.maximum(j * halo_per_tile - 1, 0), 1)),
            pl.BlockSpec((1, HALO, BRANCH_W), lambda b, j: (b, jnp.minimum((j + 1) * halo_per_tile, n_halo - 1), 1)),
            col(2),
            col(3),
            pl.BlockSpec((1, CONV_W, BRANCH_W), lambda b, j: (layer, 0, 0)),
            vec,
            vec,
            pl.BlockSpec((1, CHUNK, SGU_GROUPS * CHUNK), lambda b, j: (layer, 0, 0)),
            pl.BlockSpec((1, CHUNK, BRANCH_W), lambda b, j: (layer, 0, 0)),
            _resident((1, D_MODEL, SLOT_W + GATE_W), lambda b, j: (layer, 0, 0)),
            _resident((1, N_BRANCH, BRANCH_W, D_MODEL), lambda b, j: (layer, 0, 0, 0)),
            _resident((1, D_MODEL, D_MODEL), lambda b, j: (layer, 0, 0)),
        ],
        out_specs=tok,
        out_shape=jax.ShapeDtypeStruct((BATCH, tile * n_tiles, D_MODEL), F32),
        compiler_params=_params(2),
        name="merge",
    )(*stream, *([mods] * len(mod_specs)), gains, yf, yr, att, d_skip, glu_w, glu_b, *yb, loc, loc, loc, loc, loc, loc,
      conv_w, ln_g, ln_b, sgu_w, sgu_b, w_in, w_branch, w_out)


FF_CHUNK = 1024


def _ffn_kernel(*refs, tile, with_ctx, final):
    h_ref, ml_ref = refs[0], refs[1]
    mc_ref = refs[2] if with_ctx else None
    g_ref, w1_ref, w2_ref = refs[2 + with_ctx:5 + with_ctx]
    fg_ref = refs[5 + with_ctx] if final else None
    o_ref = refs[-1]
    j = pl.program_id(1)
    for r0, n in _sub_tiles(tile):
        rs = slice(r0, r0 + n)
        shift, scale, gate = _sub_mods(j, r0, tile, ml_ref, mc_ref, (3, 4, 5))
        h = h_ref[0, rs, :]
        a = _norm_modulate(h, g_ref[0] * (1.0 + scale), shift).astype(BF16)
        acc = jnp.zeros((n, D_MODEL), F32)
        for c in range(D_FF // FF_CHUNK):
            cs = slice(c * FF_CHUNK, (c + 1) * FF_CHUNK)
            t = jnp.maximum(_dot(a, w1_ref[0, :, cs]), 0.0)
            acc = acc + _dot((t * t).astype(BF16), w2_ref[0, cs, :])
        out = h + gate * acc
        if final:
            out = out * lax.rsqrt(jnp.mean(out * out, axis=-1, keepdims=True) + EPS) * fg_ref[...]
        o_ref[0, rs, :] = out


def _ffn_call(layer, h, mods, gains, w1, w2, final_g=None, *, tile, n_tiles, with_ctx):
    tok = pl.BlockSpec((1, tile, D_MODEL), lambda b, j: (b, j, 0))
    final = final_g is not None
    mod_specs = _mod_specs(layer, with_ctx)
    in_specs = [
        tok,
        *mod_specs,
        _gain_spec(layer, 1),
        _resident((1, D_MODEL, D_FF), lambda b, j: (layer, 0, 0)),
        _resident((1, D_FF, D_MODEL), lambda b, j: (layer, 0, 0)),
    ]
    args = [h, *([mods] * len(mod_specs)), gains, w1, w2]
    if final:
        in_specs.append(pl.BlockSpec((1, D_MODEL), lambda b, j: (0, 0)))
        args.append(final_g)
    return pl.pallas_call(
        functools.partial(_ffn_kernel, tile=tile, with_ctx=with_ctx, final=final),
        grid=(BATCH, n_tiles),
        in_specs=in_specs,
        out_specs=tok,
        out_shape=jax.ShapeDtypeStruct((BATCH, tile * n_tiles, D_MODEL), F32),
        compiler_params=_params(2),
        name="ffn_final" if final else "ffn",
    )(*args)


def kernel(x, c, ctx, c_ctx, ada_w, ada_b, norm_g, final_g, w_in, w_branch, w_out, s5_lam_re, s5_lam_im, s5_log_dt, s5_b_re, s5_b_im, s5_c_re, s5_c_im, s5_d, s5_glu_w, s5_glu_b, na_rpb, conv_w, sgu_ln_g, sgu_ln_b, sgu_w, sgu_b, w_ff1, w_ff2):
    cc = jnp.concatenate([c, c_ctx[None], jnp.zeros((MOD_ROWS - BATCH - 1, D_MODEL), F32)], axis=0)
    b_rows = lambda p: jnp.tile(jnp.swapaxes(p, 3, 4).reshape(DEPTH, 2, BRANCH_W, S5_STATE), (1, 1, 1, LANES // S5_STATE))
    c_rows = lambda p: jnp.tile(jnp.swapaxes(p, 3, 4).reshape(DEPTH, 2, S5_LANES, S5_GROUP), (1, 1, 1, LANES // S5_GROUP))
    bre, bim, cre, cim = b_rows(s5_b_re), b_rows(s5_b_im), c_rows(s5_c_re), c_rows(s5_c_im)
    lam_re = s5_lam_re.reshape(DEPTH, 2, 1, S5_LANES)
    lam_im = s5_lam_im.reshape(DEPTH, 2, 1, S5_LANES)
    log_dt = jnp.repeat(s5_log_dt, S5_STATE, axis=-1).reshape(DEPTH, 2, 1, S5_LANES)
    sgu_bias = jnp.repeat(jnp.swapaxes(sgu_b, 1, 2), SGU_GROUP_W, axis=2)
    w_in_b = w_in.astype(BF16)
    w_branch_b, w_out_b = w_branch.astype(BF16), w_out.astype(BF16)
    w_ff1_b, w_ff2_b = w_ff1.astype(BF16), w_ff2.astype(BF16)
    glu_w_b = s5_glu_w.astype(BF16)
    sgu_w_b = jnp.swapaxes(sgu_w, 1, 2).reshape(DEPTH, CHUNK, SGU_GROUPS * CHUNK).astype(BF16)
    gains = norm_g.reshape(DEPTH * 2, 1, D_MODEL)
    vec3 = lambda p: p.reshape(DEPTH, 1, BRANCH_W)

    mods = _mod_call(cc, ada_w, ada_b).reshape(DEPTH * MOD_ROWS, 1, MOD_W)
    a_bar, bb, cm = _s5_prep_call(lam_re, lam_im, log_dt, bre, bim, cre, cim)
    na_bias = _na_bias_call(na_rpb)

    stream = (x, x, ctx)
    for i in range(DEPTH):
        last = i == DEPTH - 1
        att, loc = _proj_call(i, stream, mods, gains, w_in_b)
        yf, yr = _s5_scan_call(i, att, a_bar, bb, cm)
        yb_lat = _na_call(i, att, na_bias, group=4, n_rows=SEQ)
        yb = (yb_lat,) if last else (yb_lat, yb_lat, _na_ctx_call(att))
        tiling = dict(tile=TILE_LATENT, n_tiles=SEQ // TILE_LATENT, with_ctx=False) if last else dict(
            tile=TILE, n_tiles=N_TILES, with_ctx=True)
        h = _merge_call(i, stream, mods, gains, yf, yr, att, vec3(s5_d), glu_w_b, vec3(s5_glu_b), yb, loc, conv_w,
                        vec3(sgu_ln_g), vec3(sgu_ln_b), sgu_w_b, sgu_bias, w_in_b, w_branch_b, w_out_b, **tiling)
        h = _ffn_call(i, h, mods, gains, w_ff1_b, w_ff2_b, final_g[None] if last else None, **tiling)
        stream = (h,)
    return h
```

```python
import functools
import math

import jax
import jax.numpy as jnp
from jax import lax
from jax.experimental import pallas as pl
from jax.experimental.pallas import tpu as pltpu

F32 = jnp.float32
BF16 = jnp.bfloat16

D_MODEL = 1024
BATCH = 8
SEQ = 2048
DEPTH = 4
GRID_W = 64
GRID_H = SEQ // GRID_W
CTX_LEN = 256
TOK = SEQ + CTX_LEN
N_BRANCH = 4
BRANCH_W = D_MODEL // 4
S5_GROUP = 16
S5_GROUPS = BRANCH_W // S5_GROUP
S5_STATE = 64
S5_LANES = S5_GROUPS * S5_STATE
NA_HEAD_DIM = 64
NA_HEADS = BRANCH_W // NA_HEAD_DIM
NA_WIN_ROWS = 8
NA_WIN_COLS = 16
NA_SCALE = NA_HEAD_DIM ** -0.5
CONV_W = 3
CHUNK = 128
SGU_GROUPS = 4
SGU_GROUP_W = BRANCH_W // SGU_GROUPS
D_FF = 4 * D_MODEL
N_MOD = 6
MOD_W = N_MOD * D_MODEL
EPS = 1e-6
NEG_INF = -1e30
N_SLOTS = 9
SLOT_W = N_SLOTS * BRANCH_W
ATT_W = 4 * BRANCH_W
LOC_W = 5 * BRANCH_W
GATE_W = N_BRANCH * D_MODEL

LANES = 128
MOD_ROWS = 16
MOD_CTX_ROW = BATCH
TILE = 768
N_TILES = TOK // TILE
TILE_LATENT = 1024
SUB = 256
S5_T = 128
S5_ROWS = S5_T * BATCH
S5_STEPS = TOK // S5_T
S5_LAT_STEPS = SEQ // S5_T
S5_SLABS = BRANCH_W // LANES
NA_QB = 4 * GRID_W
NA_WIN_KROWS = 12
NA_WIN_KEYS = NA_WIN_KROWS * GRID_W
NA_LAT_STEPS = SEQ // NA_QB
NA_GROUP = 4
MIB = 1024 * 1024
VMEM_LIMIT_MIB = 56


def _params(n_axes):
    return pltpu.CompilerParams(
        dimension_semantics=("arbitrary",) * n_axes,
        vmem_limit_bytes=VMEM_LIMIT_MIB * MIB,
    )


def _resident(shape, index_map):
    return pl.BlockSpec(shape, index_map, pipeline_mode=pl.Buffered(1))


def _dot(a, b):
    return jnp.dot(a, b, preferred_element_type=F32)


def _dot_nt(a, b):
    return lax.dot_general(a, b, (((1,), (1,)), ((), ())), preferred_element_type=F32)


def _norm_modulate(h, gain, shift):
    return h * lax.rsqrt(jnp.mean(h * h, axis=-1, keepdims=True) + EPS) * gain + shift


def _sub_tiles(tile):
    ctx_row0 = SEQ % tile or tile
    bounds = list(range(0, ctx_row0, SUB)) + list(range(ctx_row0, tile, SUB)) + [tile]
    return [(lo, hi - lo) for lo, hi in zip(bounds[:-1], bounds[1:])]


def _sub_mods(j, row0, tile, ml_ref, mc_ref, ks):
    vec = lambda ref, k: ref[0, :, k * D_MODEL:(k + 1) * D_MODEL]
    ctx_row0 = SEQ % tile
    if mc_ref is None or row0 < ctx_row0:
        return [vec(ml_ref, k) for k in ks]
    in_ctx = j == TOK // tile - 1
    return [jnp.where(in_ctx, vec(mc_ref, k), vec(ml_ref, k)) for k in ks]


def _stream_specs(split, width=D_MODEL):
    if not split:
        return [pl.BlockSpec((1, TILE, width), lambda b, j: (b, j, 0))]
    tail = SEQ % TILE
    return [
        pl.BlockSpec((1, TILE, width), lambda b, j: (b, jnp.minimum(j, SEQ // TILE - 1), 0)),
        pl.BlockSpec((1, tail, width), lambda b, j: (b, SEQ // tail - 1, 0)),
        pl.BlockSpec((1, CTX_LEN, width), lambda b, j: (b, 0, 0)),
    ]


def _stream_rows(src, j, r0, n):
    if len(src) == 1:
        return src[0][0, r0:r0 + n, :]
    x_ref, x_tail_ref, ctx_ref = src
    last = ctx_ref[0] if r0 >= SEQ % TILE else x_tail_ref[0, r0:r0 + n, :]
    return jnp.where(j == N_TILES - 1, last, x_ref[0, r0:r0 + n, :])


def _mod_specs(layer, with_ctx=True):
    specs = [pl.BlockSpec((1, 1, MOD_W), lambda b, j: (layer * MOD_ROWS + b, 0, 0))]
    if with_ctx:
        specs.append(pl.BlockSpec((1, 1, MOD_W), lambda b, j: (layer * MOD_ROWS + MOD_CTX_ROW, 0, 0)))
    return specs


def _gain_spec(layer, which):
    return pl.BlockSpec((1, 1, D_MODEL), lambda b, j: (2 * layer + which, 0, 0))


MOD_NT = 1536


def _mod_kernel(cc_ref, w_ref, b_ref, o_ref):
    s = jax.nn.silu(cc_ref[...])
    w = w_ref[0].astype(BF16)
    s_hi = s.astype(BF16)
    s_lo = (s - s_hi.astype(F32)).astype(BF16)
    o_ref[0] = _dot(s_hi, w) + _dot(s_lo, w) + b_ref[0]


def _mod_call(cc, ada_w, ada_b):
    return pl.pallas_call(
        _mod_kernel,
        grid=(DEPTH, MOD_W // MOD_NT),
        in_specs=[
            pl.BlockSpec((MOD_ROWS, D_MODEL), lambda i, j: (0, 0)),
            pl.BlockSpec((1, D_MODEL, MOD_NT), lambda i, j: (i, 0, j)),
            pl.BlockSpec((1, 1, MOD_NT), lambda i, j: (i, 0, j)),
        ],
        out_specs=pl.BlockSpec((1, MOD_ROWS, MOD_NT), lambda i, j: (i, 0, j)),
        out_shape=jax.ShapeDtypeStruct((DEPTH, MOD_ROWS, MOD_W), F32),
        compiler_params=_params(2),
        name="mod",
    )(cc, ada_w, ada_b.reshape(DEPTH, 1, MOD_W))


def _proj_kernel(*refs, first):
    n_src = 3 if first else 1
    src, (ml_ref, mc_ref, g_ref, w_ref, att_ref, loc_ref) = refs[:n_src], refs[n_src:]
    j = pl.program_id(1)
    for r0, n in _sub_tiles(TILE):
        rs = slice(r0, r0 + n)
        shift, scale = _sub_mods(j, r0, TILE, ml_ref, mc_ref, (0, 1))
        a = _norm_modulate(_stream_rows(src, j, r0, n), g_ref[0] * (1.0 + scale), shift).astype(BF16)
        att_ref[0, rs, :] = _dot(a, w_ref[0, :, 0:ATT_W]).astype(BF16)
        p = _dot(a, w_ref[0, :, ATT_W:SLOT_W])
        loc_ref[0, rs, 0:BRANCH_W] = p[:, 0:BRANCH_W]
        loc_ref[0, rs, BRANCH_W:2 * BRANCH_W] = p[:, BRANCH_W:2 * BRANCH_W] * p[:, 2 * BRANCH_W:3 * BRANCH_W]
        loc_ref[0, rs, 2 * BRANCH_W:4 * BRANCH_W] = p[:, 3 * BRANCH_W:5 * BRANCH_W]


def _proj_call(layer, stream, mods, gains, w_in):
    first = len(stream) > 1
    return pl.pallas_call(
        functools.partial(_proj_kernel, first=first),
        grid=(BATCH, N_TILES),
        in_specs=[
            *_stream_specs(first),
            *_mod_specs(layer),
            _gain_spec(layer, 0),
            _resident((1, D_MODEL, SLOT_W), lambda b, j: (layer, 0, 0)),
        ],
        out_specs=[
            pl.BlockSpec((1, TILE, ATT_W), lambda b, j: (b, j, 0)),
            pl.BlockSpec((1, TILE, 4 * BRANCH_W), lambda b, j: (b, j, 0)),
        ],
        out_shape=[
            jax.ShapeDtypeStruct((BATCH, TOK, ATT_W), BF16),
            jax.ShapeDtypeStruct((BATCH, TOK, 4 * BRANCH_W), F32),
        ],
        compiler_params=_params(2),
        name="proj",
    )(*stream, mods, mods, gains, w_in)


def _s5_prep_kernel(lre_ref, lim_ref, ldt_ref, bre_ref, bim_ref, cre_ref, cim_ref, a_ref, bb_ref, cm_ref):
    lam_re = lre_ref[0, 0]
    lam_im = lim_ref[0, 0]
    dt = jnp.exp(ldt_ref[0, 0])
    zr, zi = lam_re * dt, lam_im * dt
    mag = jnp.exp(zr)
    ar, ai = mag * jnp.cos(zi), mag * jnp.sin(zi)
    nr, ni = ar - 1.0, ai
    den = lam_re * lam_re + lam_im * lam_im
    cr = (nr * lam_re + ni * lam_im) / den
    ci = (ni * lam_re - nr * lam_im) / den
    b_row_g = lax.shift_right_logical(lax.broadcasted_iota(jnp.int32, (BRANCH_W, 1), 0), int(math.log2(S5_GROUP)))
    b_col_g = lax.shift_right_logical(lax.broadcasted_iota(jnp.int32, (1, S5_LANES), 1), int(math.log2(S5_STATE)))
    b_own = b_row_g == b_col_g
    expand_b = lambda ref: jnp.where(b_own, jnp.concatenate([ref[0, 0]] * (S5_LANES // LANES), axis=1), 0.0)
    bre, bim = expand_b(bre_ref), expand_b(bim_ref)
    bb_ref[0, 0, :, 0:S5_LANES] = (cr * bre - ci * bim).astype(BF16)
    bb_ref[0, 0, :, S5_LANES:2 * S5_LANES] = (cr * bim + ci * bre).astype(BF16)
    a_ref[0, 0, :, 0:S5_LANES] = jnp.broadcast_to(ar, (BATCH, S5_LANES))
    a_ref[0, 0, :, S5_LANES:2 * S5_LANES] = jnp.broadcast_to(ai, (BATCH, S5_LANES))
    c_row_g = lax.shift_right_logical(lax.broadcasted_iota(jnp.int32, (S5_LANES, 1), 0), int(math.log2(S5_STATE)))
    c_col_g = lax.shift_right_logical(lax.broadcasted_iota(jnp.int32, (1, BRANCH_W), 1), int(math.log2(S5_GROUP)))
    c_own = c_row_g == c_col_g
    expand_c = lambda ref: jnp.where(c_own, jnp.concatenate([ref[0, 0]] * (BRANCH_W // LANES), axis=1), 0.0)
    cm_ref[0, 0, 0:S5_LANES, :] = expand_c(cre_ref).astype(BF16)
    cm_ref[0, 0, S5_LANES:2 * S5_LANES, :] = (-expand_c(cim_ref)).astype(BF16)


def _s5_prep_call(lam_re, lam_im, log_dt, bre, bim, cre, cim):
    vec = pl.BlockSpec((1, 1, 1, S5_LANES), lambda i, d: (i, d, 0, 0))
    bmat = pl.BlockSpec((1, 1, BRANCH_W, LANES), lambda i, d: (i, d, 0, 0))
    cmat = pl.BlockSpec((1, 1, S5_LANES, LANES), lambda i, d: (i, d, 0, 0))
    return pl.pallas_call(
        _s5_prep_kernel,
        grid=(DEPTH, 2),
        in_specs=[vec, vec, vec, bmat, bmat, cmat, cmat],
        out_specs=[
            pl.BlockSpec((1, 1, BATCH, 2 * S5_LANES), lambda i, d: (i, d, 0, 0)),
            pl.BlockSpec((1, 1, BRANCH_W, 2 * S5_LANES), lambda i, d: (i, d, 0, 0)),
            pl.BlockSpec((1, 1, 2 * S5_LANES, BRANCH_W), lambda i, d: (i, d, 0, 0)),
        ],
        out_shape=[
            jax.ShapeDtypeStruct((DEPTH, 2, BATCH, 2 * S5_LANES), F32),
            jax.ShapeDtypeStruct((DEPTH, 2, BRANCH_W, 2 * S5_LANES), BF16),
            jax.ShapeDtypeStruct((DEPTH, 2, 2 * S5_LANES, BRANCH_W), BF16),
        ],
        compiler_params=_params(2),
        name="s5_prep",
    )(lam_re, lam_im, log_dt, bre, bim, cre, cim)


def _s5_fwd_block(i):
    return jnp.where(i < S5_STEPS - S5_LAT_STEPS, S5_LAT_STEPS + i, i - (S5_STEPS - S5_LAT_STEPS))


def _s5_rev_block(i):
    return S5_STEPS - 1 - i


def _to_time_major(u_ref, slab_ref):
    for b in range(BATCH):
        ub = u_ref[b].astype(F32)
        for s in range(S5_SLABS):
            slab_ref[s, pl.ds(b, S5_T, stride=BATCH), :] = ub[:, s * LANES:(s + 1) * LANES]


def _s5_scan_kernel(uf_ref, ur_ref, a_ref, bb_ref, cm_ref, yf_ref, yr_ref, xf_ref, xr_ref, st_ref, usf_ref, usr_ref):
    i = pl.program_id(0)

    @pl.when(i == 0)
    def _():
        st_ref[...] = jnp.zeros_like(st_ref)

    mm_rows = 256
    dirs = ((uf_ref, yf_ref, xf_ref, usf_ref), (ur_ref, yr_ref, xr_ref, usr_ref))
    for d, (u_ref, y_ref, x_ref, us_ref) in enumerate(dirs):
        _to_time_major(u_ref, us_ref)
        for c in range(S5_ROWS // mm_rows):
            rs = slice(c * mm_rows, (c + 1) * mm_rows)
            u = jnp.concatenate([us_ref[s, rs, :] for s in range(S5_SLABS)], axis=1).astype(BF16)
            x_ref[rs, :] = _dot(u, bb_ref[0, d])
    for d, (u_ref, y_ref, x_ref, us_ref) in enumerate(dirs):
        ar = a_ref[0, d, :, 0:S5_LANES]
        ai = a_ref[0, d, :, S5_LANES:2 * S5_LANES]
        hr, hi = st_ref[d, :, 0:S5_LANES], st_ref[d, :, S5_LANES:2 * S5_LANES]
        for t in range(S5_T):
            r = (t if d == 0 else S5_T - 1 - t) * BATCH
            xr = x_ref[r:r + BATCH, 0:S5_LANES]
            xi = x_ref[r:r + BATCH, S5_LANES:2 * S5_LANES]
            hr, hi = ar * hr - ai * hi + xr, ar * hi + ai * hr + xi
            x_ref[r:r + BATCH, 0:S5_LANES] = hr
            x_ref[r:r + BATCH, S5_LANES:2 * S5_LANES] = hi
        st_ref[d, :, 0:S5_LANES] = hr
        st_ref[d, :, S5_LANES:2 * S5_LANES] = hi
        for c in range(S5_ROWS // mm_rows):
            rs = slice(c * mm_rows, (c + 1) * mm_rows)
            y = _dot(x_ref[rs, :].astype(BF16), cm_ref[0, d])
            for s in range(S5_SLABS):
                us_ref[s, rs, :] = y[:, s * LANES:(s + 1) * LANES]
        for b in range(BATCH):
            y_ref[b] = jnp.concatenate(
                [us_ref[s, pl.ds(b, S5_T, stride=BATCH), :] for s in range(S5_SLABS)], axis=1)


def _s5_scan_call(layer, att, a_bar, bb, cm):
    u_blk = lambda f: pl.BlockSpec((BATCH, S5_T, BRANCH_W), lambda i: (0, f(i), 0))
    y_blk = u_blk
    return pl.pallas_call(
        _s5_scan_kernel,
        grid=(S5_STEPS,),
        in_specs=[
            u_blk(_s5_fwd_block),
            u_blk(_s5_rev_block),
            pl.BlockSpec((1, 2, BATCH, 2 * S5_LANES), lambda i: (layer, 0, 0, 0)),
            pl.BlockSpec((1, 2, BRANCH_W, 2 * S5_LANES), lambda i: (layer, 0, 0, 0)),
            pl.BlockSpec((1, 2, 2 * S5_LANES, BRANCH_W), lambda i: (layer, 0, 0, 0)),
        ],
        out_specs=[y_blk(_s5_fwd_block), y_blk(_s5_rev_block)],
        out_shape=[jax.ShapeDtypeStruct((BATCH, TOK, BRANCH_W), F32)] * 2,
        scratch_shapes=[
            pltpu.VMEM((S5_ROWS, 2 * S5_LANES), F32),
            pltpu.VMEM((S5_ROWS, 2 * S5_LANES), F32),
            pltpu.VMEM((2, BATCH, 2 * S5_LANES), F32),
            pltpu.VMEM((S5_SLABS, S5_ROWS, LANES), F32),
            pltpu.VMEM((S5_SLABS, S5_ROWS, LANES), F32),
        ],
        compiler_params=_params(1),
        name="s5_scan",
    )(att, att, a_bar, bb, cm)


NA_CASE_FIRST, NA_CASE_INNER, NA_CASE_LAST = 0, 1, 2
NA_CASES = 3
N_RPB = (2 * NA_WIN_ROWS - 1) * (2 * NA_WIN_COLS - 1)


def _na_window_start(row0):
    return min(max(row0 - NA_WIN_ROWS // 2, 0), GRID_H - NA_WIN_KROWS)


def _na_bias_kernel(rpb_ref, o_ref):
    layer, h = pl.program_id(0), pl.program_id(1)
    base = (layer * NA_HEADS + h) * N_RPB
    qq = lax.broadcasted_iota(jnp.int32, (GRID_W, 2 * GRID_W), 0)
    lane = lax.broadcasted_iota(jnp.int32, (GRID_W, 2 * GRID_W), 1)
    kk = jnp.bitwise_and(lane, GRID_W - 1)
    second = lane >= GRID_W
    dcm = jnp.clip(kk - qq + NA_WIN_COLS - 1, 0, 2 * NA_WIN_COLS - 2)
    cs = jnp.clip(qq - NA_WIN_COLS // 2, 0, GRID_W - NA_WIN_COLS)
    col_ok = (kk >= cs) & (kk < cs + NA_WIN_COLS)
    neg = jnp.full((GRID_W, 2 * GRID_W), NEG_INF, F32)
    n_dr, n_dc = 2 * NA_WIN_ROWS - 1, 2 * NA_WIN_COLS - 1
    table = []
    for dr in range(n_dr):
        t = neg
        for dc in range(n_dc):
            t = jnp.where(dcm == dc, rpb_ref[base + dr * n_dc + dc], t)
        table.append(jnp.where(col_ok, t, NEG_INF))
    for case, row0 in ((NA_CASE_FIRST, 0), (NA_CASE_INNER, 4), (NA_CASE_LAST, GRID_H - 4)):
        start = _na_window_start(row0)
        for i in range(4):
            r = row0 + i
            row_start = min(max(r - NA_WIN_ROWS // 2, 0), GRID_H - NA_WIN_ROWS)
            for m in range(NA_WIN_KROWS // 2):
                halves = []
                for jj in (2 * m, 2 * m + 1):
                    kr = start + jj
                    valid = row_start <= kr < row_start + NA_WIN_ROWS
                    halves.append(table[kr - r + NA_WIN_ROWS - 1] if valid else neg)
                o_ref[0, case, 0, i * GRID_W:(i + 1) * GRID_W, m * 2 * GRID_W:(m + 1) * 2 * GRID_W] = (
                    jnp.where(second, halves[1], halves[0]))


def _na_bias_call(rpb):
    return pl.pallas_call(
        _na_bias_kernel,
        grid=(DEPTH, NA_HEADS),
        in_specs=[pl.BlockSpec(memory_space=pltpu.SMEM)],
        out_specs=pl.BlockSpec((1, NA_CASES, 1, NA_QB, NA_WIN_KEYS), lambda i, h: (i, 0, h, 0, 0)),
        out_shape=jax.ShapeDtypeStruct((DEPTH, NA_CASES, NA_HEADS, NA_QB, NA_WIN_KEYS), F32),
        compiler_params=_params(2),
        name="na_bias",
    )(rpb.reshape(-1))


def _na_attend(q, blocks):
    head = lax.shift_right_logical(lax.broadcasted_iota(jnp.int32, (1, BRANCH_W), 1), int(math.log2(NA_HEAD_DIM)))
    acc = jnp.zeros((NA_QB, BRANCH_W), F32)
    for h in range(NA_HEADS):
        in_head = head == h
        qh = jnp.where(in_head, q, jnp.zeros_like(q))
        scores = [_dot_nt(qh, k) if bias is None else _dot_nt(qh, k) + bias(h) for k, _, bias in blocks]
        m = functools.reduce(jnp.maximum, [jnp.max(s, axis=-1, keepdims=True) for s in scores])
        probs = [jnp.exp(s - m) for s in scores]
        norm = sum(jnp.sum(p, axis=-1, keepdims=True) for p in probs)
        o = sum(_dot(p.astype(BF16), v) for p, (_, v, _) in zip(probs, blocks))
        acc = jnp.where(in_head, o * (1.0 / norm), acc)
    return acc.astype(BF16)


def _na_kernel(q_ref, k_ref, v_ref, bias_ref, o_ref):
    j = pl.program_id(1)
    ctx_block = (k_ref[0, SEQ:TOK, :], v_ref[0, SEQ:TOK, :], None)
    for i in range(NA_GROUP):
        blk = j * NA_GROUP + i
        start = jnp.clip(blk * 4 - NA_WIN_ROWS // 2, 0, GRID_H - NA_WIN_KROWS)
        k0 = pl.multiple_of(start * GRID_W, GRID_W)
        case = jnp.where(blk == 0, NA_CASE_FIRST, jnp.where(blk == NA_LAT_STEPS - 1, NA_CASE_LAST, NA_CASE_INNER))
        bias = lambda h, case=case: bias_ref[0, case, h]
        window = (k_ref[0, pl.ds(k0, NA_WIN_KEYS), :], v_ref[0, pl.ds(k0, NA_WIN_KEYS), :], bias)
        rs = slice(i * NA_QB, (i + 1) * NA_QB)
        o_ref[0, rs, :] = _na_attend(q_ref[0, rs, :] * NA_SCALE, [window, ctx_block])


def _na_call(layer, att, bias):
    rows = NA_GROUP * NA_QB
    return pl.pallas_call(
        _na_kernel,
        grid=(BATCH, SEQ // rows),
        in_specs=[
            pl.BlockSpec((1, rows, BRANCH_W), lambda b, j: (b, j, 1)),
            pl.BlockSpec((1, TOK, BRANCH_W), lambda b, j: (b, 0, 2)),
            pl.BlockSpec((1, TOK, BRANCH_W), lambda b, j: (b, 0, 3)),
            _resident((1, NA_CASES, NA_HEADS, NA_QB, NA_WIN_KEYS), lambda b, j: (layer, 0, 0, 0, 0)),
        ],
        out_specs=pl.BlockSpec((1, rows, BRANCH_W), lambda b, j: (b, j, 0)),
        out_shape=jax.ShapeDtypeStruct((BATCH, SEQ, BRANCH_W), BF16),
        compiler_params=_params(2),
        name="na",
    )(att, att, att, bias)


def _na_ctx_kernel(q_ref, k_ref, v_ref, o_ref):
    o_ref[0] = _na_attend(q_ref[0] * NA_SCALE, [(k_ref[0], v_ref[0], None)])


def _na_ctx_call(att):
    ctx_blk = lambda col: pl.BlockSpec((1, CTX_LEN, BRANCH_W), lambda b: (b, SEQ // CTX_LEN, col))
    return pl.pallas_call(
        _na_ctx_kernel,
        grid=(BATCH,),
        in_specs=[ctx_blk(1), ctx_blk(2), ctx_blk(3)],
        out_specs=pl.BlockSpec((1, CTX_LEN, BRANCH_W), lambda b: (b, 0, 0)),
        out_shape=jax.ShapeDtypeStruct((BATCH, CTX_LEN, BRANCH_W), BF16),
        compiler_params=_params(1),
        name="na_ctx",
    )(att, att, att)


HALO = 8


def _short_conv(j, tile, cb_ref, z_ref, zp_ref, zn_ref, cw_ref):
    local = lax.broadcasted_iota(jnp.int32, (tile, 1), 0)
    rows = j * tile + local
    z = z_ref[0]
    prev = jnp.where(local == 0, zp_ref[0, HALO - 1:HALO, :], pltpu.roll(z, 1, 0))
    nxt = jnp.where(local == tile - 1, zn_ref[0, 0:1, :], pltpu.roll(z, tile - 1, 0))
    prev = jnp.where((rows == 0) | (rows == SEQ), 0.0, prev)
    nxt = jnp.where((rows == SEQ - 1) | (rows == TOK - 1), 0.0, nxt)
    y = cw_ref[0, 0:1, :] * prev + cw_ref[0, 1:2, :] * z + cw_ref[0, 2:3, :] * nxt
    return (cb_ref[0] * y).astype(BF16)


def _spatial_gating(u, v, lg_ref, lb_ref, sw_ref, sb_ref):
    mu = jnp.mean(v, axis=-1, keepdims=True)
    var = jnp.mean(jnp.square(v - mu), axis=-1, keepdims=True)
    vn = ((v - mu) * lax.rsqrt(var + EPS) * lg_ref[0] + lb_ref[0]).astype(BF16)
    group = lax.shift_right_logical(lax.broadcasted_iota(jnp.int32, (1, BRANCH_W), 1), int(math.log2(SGU_GROUP_W)))
    zero = jnp.zeros((CHUNK, BRANCH_W), BF16)
    out = []
    for c in range(v.shape[0] // CHUNK):
        rs = slice(c * CHUNK, (c + 1) * CHUNK)
        stacked = jnp.concatenate([jnp.where(group == g, vn[rs, :], zero) for g in range(SGU_GROUPS)], axis=0)
        out.append((u[rs, :] * (_dot(sw_ref[0], stacked) + sb_ref[0])).astype(BF16))
    return jnp.concatenate(out, axis=0)


def _s5_glu(y, w_ref, b_ref):
    g = jax.nn.gelu(y)
    return (g * jax.nn.sigmoid(_dot(g.astype(BF16), w_ref[0]) + b_ref[0])).astype(BF16)


def _merge_kernel(*refs, tile, with_ctx, first):
    n_src = 3 if first else 1
    src, refs = refs[:n_src], refs[n_src - 1:]
    ml_ref = refs[1]
    mc_ref = refs[2] if with_ctx else None
    refs = refs[2 + with_ctx:]
    g_ref, yf_ref, yr_ref, u_ref, d_ref, gw_ref, gb_ref = refs[:7]
    n_yb = 3 if with_ctx else 1
    yb_src = refs[7:7 + n_yb]
    (cb_ref, z_ref, zp_ref, zn_ref, su_ref, sv_ref, cw_ref, lg_ref, lb_ref, sw_ref, sb_ref,
     wg_ref, wb_ref, wo_ref, o_ref) = refs[7 + n_yb:]
    j = pl.program_id(1)
    yc = _short_conv(j, tile, cb_ref, z_ref, zp_ref, zn_ref, cw_ref)
    for r0, n in _sub_tiles(tile):
        rs = slice(r0, r0 + n)
        shift, scale, gate = _sub_mods(j, r0, tile, ml_ref, mc_ref, (0, 1, 2))
        h = _stream_rows(src, j, r0, n)
        a = _norm_modulate(h, g_ref[0] * (1.0 + scale), shift).astype(BF16)
        yd = _spatial_gating(su_ref[0, rs, :], sv_ref[0, rs, :], lg_ref, lb_ref, sw_ref, sb_ref)
        ya = _s5_glu(yf_ref[0, rs, :] + yr_ref[0, rs, :] + u_ref[0, rs, :].astype(F32) * d_ref[0], gw_ref, gb_ref)
        ys = (ya, _stream_rows(yb_src, j, r0, n), yc[rs, :], yd)
        acc = jnp.zeros((n, D_MODEL), F32)
        for k in range(N_BRANCH):
            logits = _dot(a, wg_ref[0, :, SLOT_W + k * D_MODEL:SLOT_W + (k + 1) * D_MODEL])
            acc = acc + jax.nn.sigmoid(logits) * _dot(ys[k], wb_ref[0, k])
        o_ref[0, rs, :] = h + gate * _dot(acc.astype(BF16), wo_ref[0])


def _merge_call(layer, stream, mods, gains, yf, yr, att, d_skip, glu_w, glu_b, yb, loc, conv_w, ln_g, ln_b, sgu_w, sgu_b,
                w_in, w_branch, w_out, *, tile, n_tiles, with_ctx):
    tok = pl.BlockSpec((1, tile, D_MODEL), lambda b, j: (b, j, 0))
    br = pl.BlockSpec((1, tile, BRANCH_W), lambda b, j: (b, j, 0))
    col = lambda k: pl.BlockSpec((1, tile, BRANCH_W), lambda b, j: (b, j, k))
    halo_per_tile = tile // HALO
    n_halo = TOK // HALO
    vec = pl.BlockSpec((1, 1, BRANCH_W), lambda b, j: (layer, 0, 0))
    mod_specs = _mod_specs(layer, with_ctx)
    first = len(stream) > 1
    return pl.pallas_call(
        functools.partial(_merge_kernel, tile=tile, with_ctx=with_ctx, first=first),
        grid=(BATCH, n_tiles),
        in_specs=[
            *(_stream_specs(True) if first else [tok]),
            *mod_specs,
            _gain_spec(layer, 0),
            br, br, br,
            vec,
            pl.BlockSpec((1, BRANCH_W, BRANCH_W), lambda b, j: (layer, 0, 0)),
            vec,
            *(_stream_specs(True, BRANCH_W) if with_ctx else [br]),
            col(0),
            col(1),
            pl.BlockSpec((1, HALO, BRANCH_W), lambda b, j: (b, jnp.maximum(j * halo_per_tile - 1, 0), 1)),
            pl.BlockSpec((1, HALO, BRANCH_W), lambda b, j: (b, jnp.minimum((j + 1) * halo_per_tile, n_halo - 1), 1)),
            col(2),
            col(3),
            pl.BlockSpec((1, CONV_W, BRANCH_W), lambda b, j: (layer, 0, 0)),
            vec,
            vec,
            pl.BlockSpec((1, CHUNK, SGU_GROUPS * CHUNK), lambda b, j: (layer, 0, 0)),
            pl.BlockSpec((1, CHUNK, BRANCH_W), lambda b, j: (layer, 0, 0)),
            _resident((1, D_MODEL, SLOT_W + GATE_W), lambda b, j: (layer, 0, 0)),
            _resident((1, N_BRANCH, BRANCH_W, D_MODEL), lambda b, j: (layer, 0, 0, 0)),
            _resident((1, D_MODEL, D_MODEL), lambda b, j: (layer, 0, 0)),
        ],
        out_specs=tok,
        out_shape=jax.ShapeDtypeStruct((BATCH, tile * n_tiles, D_MODEL), F32),
        compiler_params=_params(2),
        name="merge",
    )(*stream, *([mods] * len(mod_specs)), gains, yf, yr, att, d_skip, glu_w, glu_b, *yb, loc, loc, loc, loc, loc, loc,
      conv_w, ln_g, ln_b, sgu_w, sgu_b, w_in, w_branch, w_out)


FF_CHUNK = 1024


def _ffn_kernel(*refs, tile, with_ctx, final):
    h_ref, ml_ref = refs[0], refs[1]
    mc_ref = refs[2] if with_ctx else None
    g_ref, w1_ref, w2_ref = refs[2 + with_ctx:5 + with_ctx]
    fg_ref = refs[5 + with_ctx] if final else None
    o_ref = refs[-1]
    j = pl.program_id(1)
    for r0, n in _sub_tiles(tile):
        rs = slice(r0, r0 + n)
        shift, scale, gate = _sub_mods(j, r0, tile, ml_ref, mc_ref, (3, 4, 5))
        h = h_ref[0, rs, :]
        a = _norm_modulate(h, g_ref[0] * (1.0 + scale), shift).astype(BF16)
        acc = jnp.zeros((n, D_MODEL), F32)
        for c in range(D_FF // FF_CHUNK):
            cs = slice(c * FF_CHUNK, (c + 1) * FF_CHUNK)
            t = jnp.maximum(_dot(a, w1_ref[0, :, cs]), 0.0)
            acc = acc + _dot((t * t).astype(BF16), w2_ref[0, cs, :])
        out = h + gate * acc
        if final:
            out = out * lax.rsqrt(jnp.mean(out * out, axis=-1, keepdims=True) + EPS) * fg_ref[...]
        o_ref[0, rs, :] = out


def _ffn_call(layer, h, mods, gains, w1, w2, final_g=None, *, tile, n_tiles, with_ctx):
    tok = pl.BlockSpec((1, tile, D_MODEL), lambda b, j: (b, j, 0))
    final = final_g is not None
    mod_specs = _mod_specs(layer, with_ctx)
    in_specs = [
        tok,
        *mod_specs,
        _gain_spec(layer, 1),
        _resident((1, D_MODEL, D_FF), lambda b, j: (layer, 0, 0)),
        _resident((1, D_FF, D_MODEL), lambda b, j: (layer, 0, 0)),
    ]
    args = [h, *([mods] * len(mod_specs)), gains, w1, w2]
    if final:
        in_specs.append(pl.BlockSpec((1, D_MODEL), lambda b, j: (0, 0)))
        args.append(final_g)
    return pl.pallas_call(
        functools.partial(_ffn_kernel, tile=tile, with_ctx=with_ctx, final=final),
        grid=(BATCH, n_tiles),
        in_specs=in_specs,
        out_specs=tok,
        out_shape=jax.ShapeDtypeStruct((BATCH, tile * n_tiles, D_MODEL), F32),
        compiler_params=_params(2),
        name="ffn_final" if final else "ffn",
    )(*args)


def kernel(x, c, ctx, c_ctx, ada_w, ada_b, norm_g, final_g, w_in, w_branch, w_out, s5_lam_re, s5_lam_im, s5_log_dt, s5_b_re, s5_b_im, s5_c_re, s5_c_im, s5_d, s5_glu_w, s5_glu_b, na_rpb, conv_w, sgu_ln_g, sgu_ln_b, sgu_w, sgu_b, w_ff1, w_ff2):
    cc = jnp.concatenate([c, c_ctx[None], jnp.zeros((MOD_ROWS - BATCH - 1, D_MODEL), F32)], axis=0)
    b_rows = lambda p: jnp.tile(jnp.swapaxes(p, 3, 4).reshape(DEPTH, 2, BRANCH_W, S5_STATE), (1, 1, 1, LANES // S5_STATE))
    c_rows = lambda p: jnp.tile(jnp.swapaxes(p, 3, 4).reshape(DEPTH, 2, S5_LANES, S5_GROUP), (1, 1, 1, LANES // S5_GROUP))
    bre, bim, cre, cim = b_rows(s5_b_re), b_rows(s5_b_im), c_rows(s5_c_re), c_rows(s5_c_im)
    lam_re = s5_lam_re.reshape(DEPTH, 2, 1, S5_LANES)
    lam_im = s5_lam_im.reshape(DEPTH, 2, 1, S5_LANES)
    log_dt = jnp.repeat(s5_log_dt, S5_STATE, axis=-1).reshape(DEPTH, 2, 1, S5_LANES)
    sgu_bias = jnp.repeat(jnp.swapaxes(sgu_b, 1, 2), SGU_GROUP_W, axis=2)
    w_in_b = w_in.astype(BF16)
    w_branch_b, w_out_b = w_branch.astype(BF16), w_out.astype(BF16)
    w_ff1_b, w_ff2_b = w_ff1.astype(BF16), w_ff2.astype(BF16)
    glu_w_b = s5_glu_w.astype(BF16)
    sgu_w_b = jnp.swapaxes(sgu_w, 1, 2).reshape(DEPTH, CHUNK, SGU_GROUPS * CHUNK).astype(BF16)
    gains = norm_g.reshape(DEPTH * 2, 1, D_MODEL)
    vec3 = lambda p: p.reshape(DEPTH, 1, BRANCH_W)

    mods = _mod_call(cc, ada_w, ada_b).reshape(DEPTH * MOD_ROWS, 1, MOD_W)
    a_bar, bb, cm = _s5_prep_call(lam_re, lam_im, log_dt, bre, bim, cre, cim)
    na_bias = _na_bias_call(na_rpb)

    stream = (x, x, ctx)
    for i in range(DEPTH):
        last = i == DEPTH - 1
        att, loc = _proj_call(i, stream, mods, gains, w_in_b)
        yf, yr = _s5_scan_call(i, att, a_bar, bb, cm)
        yb_lat = _na_call(i, att, na_bias)
        yb = (yb_lat,) if last else (yb_lat, yb_lat, _na_ctx_call(att))
        tiling = dict(tile=TILE_LATENT, n_tiles=SEQ // TILE_LATENT, with_ctx=False) if last else dict(
            tile=TILE, n_tiles=N_TILES, with_ctx=True)
        h = _merge_call(i, stream, mods, gains, yf, yr, att, vec3(s5_d), glu_w_b, vec3(s5_glu_b), yb, loc, conv_w,
                        vec3(sgu_ln_g), vec3(sgu_ln_b), sgu_w_b, sgu_bias, w_in_b, w_branch_b, w_out_b, **tiling)
        h = _ffn_call(i, h, mods, gains, w_ff1_b, w_ff2_b, final_g[None] if last else None, **tiling)
        stream = (h,)
    return h
```
